```python
import math
import jax
import jax.numpy as jnp
from jax import lax
import numpy as np

D_MODEL = 2048
BATCH = 4
SEQ = 2048
DEPTH = 1

MEM_LEN = 256
RWKV_HEAD_DIM = 64
RWKV_WIDTH = D_MODEL // 2
RWKV_HEADS = RWKV_WIDTH // RWKV_HEAD_DIM
DECAY_RANK = 64
A_RANK = 64
GATE_RANK = 160
SB_HEADS = 4
SB_WIDTH = D_MODEL // 4
SB_HEAD_DIM = SB_WIDTH // SB_HEADS
SB_BLOCK = 128
MEM_HEADS = 4
MEM_WIDTH = D_MODEL // 4
MEM_HEAD_DIM = MEM_WIDTH // MEM_HEADS
MIX_WIDTH = RWKV_WIDTH + SB_WIDTH + MEM_WIDTH
RWKV_COLS = 3 * RWKV_WIDTH + DECAY_RANK + A_RANK + GATE_RANK
IN_COLS = RWKV_COLS + 3 * SB_WIDTH + MEM_WIDTH
N_GROUPS = 8
EXPERTS_PER_GROUP = 8
N_EXPERTS = N_GROUPS * EXPERTS_PER_GROUP
TOP_K_INNER = 2
D_EXPERT = D_MODEL // 2
MOE_BLOCK = 128
DEEPNORM_ALPHA = (2.0 * DEPTH) ** 0.25
DEEPNORM_BETA = (8.0 * DEPTH) ** -0.25
LN_EPS = 1e-5
GN_EPS = 64e-5

kernel_name = 'hybrid_rwkv7_stickbreak_memxattn_hmoe_deepnorm'


def layer_norm(x, g, b):
    xf = x.astype(jnp.float32)
    mu = jnp.mean(xf, -1, keepdims=True)
    var = jnp.mean(jnp.square(xf - mu), -1, keepdims=True)
    return ((xf - mu) * lax.rsqrt(var + LN_EPS) * g + b).astype(x.dtype)


def wkv7_step(state, inp):
    r, w, k, v, aa, bb = inp
    sa = jnp.einsum('bhvk,bhk->bhv', state, aa)
    state = state * w[:, :, None, :] + sa[..., None] * bb[:, :, None, :] + v[..., None] * k[:, :, None, :]
    y = jnp.einsum('bhvk,bhk->bhv', state, r)
    return state, y


def rwkv7_group(p, mu, w0, w_decay_up, a0, w_a_up, w_g_up, k_k, k_a, r_k, lnx_g, lnx_b):
    bsz, seq, _ = p.shape
    p = p.astype(jnp.float32)
    prev = jnp.pad(p, ((0, 0), (1, 0), (0, 0)))[:, :-1]
    p = p + (prev - p) * mu
    c = RWKV_WIDTH
    cuts = [c, 2 * c, 3 * c, 3 * c + DECAY_RANK, 3 * c + DECAY_RANK + A_RANK]
    r, k, v, d_low, a_low, g_low = jnp.split(p, cuts, axis=-1)
    w_log = -jax.nn.softplus(-(w0 + jnp.tanh(d_low) @ w_decay_up)) - 0.5
    decay = jnp.exp(-jnp.exp(w_log))
    a = jax.nn.sigmoid(a0 + a_low @ w_a_up)
    g = jax.nn.sigmoid(g_low) @ w_g_up

    def heads(t):
        return t.reshape(bsz, seq, RWKV_HEADS, RWKV_HEAD_DIM)

    kk = heads(k * k_k)
    kk = kk / jnp.maximum(jnp.sqrt(jnp.sum(kk * kk, -1, keepdims=True)), 1e-12)
    k = k * (1.0 + (a - 1.0) * k_a)
    r_h, k_h, v_h, w_h, a_h = heads(r), heads(k), heads(v), heads(decay), heads(a)
    xs = (r_h, w_h, k_h, v_h, -kk, kk * a_h)
    xs = [jnp.swapaxes(t, 0, 1) for t in xs]
    s0 = jnp.zeros((bsz, RWKV_HEADS, RWKV_HEAD_DIM, RWKV_HEAD_DIM), jnp.float32)
    _, y = lax.scan(wkv7_step, s0, xs)
    y = jnp.swapaxes(y, 0, 1)
    mu_y = jnp.mean(y, -1, keepdims=True)
    var_y = jnp.mean(jnp.square(y - mu_y), -1, keepdims=True)
    y = ((y - mu_y) * lax.rsqrt(var_y + GN_EPS)).reshape(bsz, seq, c) * lnx_g + lnx_b
    bonus = jnp.sum(r_h * k_h * r_k, -1, keepdims=True) * v_h
    return (y + bonus.reshape(bsz, seq, c)) * g


def stick_breaking_attention(q, k, v):
    seq = q.shape[1]
    scale = SB_HEAD_DIM ** -0.5
    outs = []
    for q0 in range(0, seq, SB_BLOCK):
        end = q0 + SB_BLOCK
        qb = q[:, q0:end].astype(jnp.float32)
        kb = k[:, :end].astype(jnp.float32)
        vb = v[:, :end].astype(jnp.float32)
        z = jnp.einsum('bqhd,bkhd->bhqk', qb, kb) * scale
        t_pos = q0 + jnp.arange(SB_BLOCK)[:, None]
        s_pos = jnp.arange(end)[None, :]
        causal = s_pos < t_pos
        log_keep = jnp.where(causal, jax.nn.log_sigmoid(-z), 0.0)
        after = lax.cumsum(log_keep, axis=3, reverse=True) - log_keep
        attn = jnp.where(causal, jnp.exp(jax.nn.log_sigmoid(z) + after), 0.0)
        outs.append(jnp.einsum('bhqk,bkhd->bqhd', attn, vb))
    return jnp.concatenate(outs, axis=1)


def memory_attention(q, kv):
    bsz, seq, _ = q.shape
    q = q.reshape(bsz, seq, MEM_HEADS, MEM_HEAD_DIM).astype(jnp.float32)
    k, v = jnp.split(kv.astype(jnp.float32), 2, axis=-1)
    k = k.reshape(bsz, -1, MEM_HEADS, MEM_HEAD_DIM)
    v = v.reshape(bsz, -1, MEM_HEADS, MEM_HEAD_DIM)
    scores = jnp.einsum('bshd,bmhd->bhsm', q, k) * (MEM_HEAD_DIM ** -0.5)
    probs = jax.nn.softmax(scores, axis=-1)
    return jnp.einsum('bhsm,bmhd->bshd', probs, v).reshape(bsz, seq, MEM_WIDTH)


def hybrid_mixer(h, mem, w_in, mu, w0, w_decay_up, a0, w_a_up, w_g_up, k_k, k_a, r_k,
                 lnx_g, lnx_b, w_mem_kv, w_out):
    bsz, seq, _ = h.shape
    proj = h @ w_in
    p_rwkv, p_sb, q_mem = jnp.split(proj, [RWKV_COLS, RWKV_COLS + 3 * SB_WIDTH], axis=-1)
    y_rwkv = rwkv7_group(p_rwkv, mu, w0, w_decay_up, a0, w_a_up, w_g_up, k_k, k_a, r_k, lnx_g, lnx_b)
    q_sb, k_sb, v_sb = [t.reshape(bsz, seq, SB_HEADS, SB_HEAD_DIM) for t in jnp.split(p_sb, 3, axis=-1)]
    y_sb = stick_breaking_attention(q_sb, k_sb, v_sb).reshape(bsz, seq, SB_WIDTH)
    y_mem = memory_attention(q_mem, mem @ w_mem_kv)
    y = jnp.concatenate([y_rwkv, y_sb, y_mem], axis=-1)
    return (y @ w_out).astype(h.dtype)


def hierarchical_moe(h, router_group, router_group_b, router_expert, router_expert_b, w_gate, w_up, w_down):
    bsz, seq, d = h.shape
    tokens = h.reshape(-1, d)
    n_tok = tokens.shape[0]
    group_logits = (tokens @ router_group).astype(jnp.float32) + router_group_b
    group_prob = jax.nn.softmax(group_logits, axis=-1)
    group_w, group_idx = lax.top_k(group_prob, 1)
    expert_logits = ((tokens @ router_expert).astype(jnp.float32) + router_expert_b)
    expert_logits = expert_logits.reshape(n_tok, N_GROUPS, EXPERTS_PER_GROUP)
    in_group = jnp.take_along_axis(expert_logits, group_idx[:, :, None], axis=1)[:, 0]
    top_vals, top_idx = lax.top_k(in_group, TOP_K_INNER)
    top_w = jax.nn.softmax(top_vals, axis=-1) * group_w
    expert_id = group_idx * EXPERTS_PER_GROUP + top_idx
    n_assign = n_tok * TOP_K_INNER
    flat_e = expert_id.reshape(-1)
    flat_tok = jnp.repeat(jnp.arange(n_tok, dtype=jnp.int32), TOP_K_INNER)
    flat_w = top_w.reshape(-1)
    order = jnp.argsort(flat_e)
    sorted_e = flat_e[order]
    counts = jnp.bincount(flat_e, length=N_EXPERTS)
    padded = (counts + MOE_BLOCK - 1) // MOE_BLOCK * MOE_BLOCK
    pad_end = jnp.cumsum(padded)
    pad_start = pad_end - padded
    start = jnp.cumsum(counts) - counts
    dest = pad_start[sorted_e] + jnp.arange(n_assign) - start[sorted_e]
    n_blocks = -(-n_assign // MOE_BLOCK) + N_EXPERTS
    n_slots = n_blocks * MOE_BLOCK
    slot_tok = jnp.full((n_slots,), n_tok, jnp.int32).at[dest].set(flat_tok[order])
    slot_w = jnp.zeros((n_slots,), jnp.float32).at[dest].set(flat_w[order])
    block_expert = jnp.minimum(
        jnp.searchsorted(pad_end, jnp.arange(n_blocks) * MOE_BLOCK, side='right'), N_EXPERTS - 1)
    tokens_pad = jnp.concatenate([tokens, jnp.zeros((1, d), tokens.dtype)], axis=0)

    def expert_block(args):
        tok, e = args
        xb = tokens_pad[tok]
        hid = jax.nn.silu(xb @ w_gate[e]) * (xb @ w_up[e])
        return hid @ w_down[e]

    ys = lax.map(expert_block, (slot_tok.reshape(n_blocks, MOE_BLOCK), block_expert))
    ys = ys.reshape(n_slots, d).astype(jnp.float32) * slot_w[:, None]
    out = jnp.zeros((n_tok + 1, d), jnp.float32).at[slot_tok].add(ys)[:n_tok]
    return out.reshape(bsz, seq, d).astype(h.dtype)


def setup_inputs(seed: int = 0) -> dict:
    key = jax.random.key(seed)
    ks = iter(jax.random.split(key, 40))
    f32 = jnp.float32

    def nrm(shape, scale):
        return jax.random.normal(next(ks), shape, f32) * scale

    L, C = DEPTH, RWKV_WIDTH
    ratio = jnp.linspace(0.0, 1.0, C, dtype=f32)
    return {
        'x': nrm((BATCH, SEQ, D_MODEL), 1.0),
        'mem': nrm((BATCH, MEM_LEN, D_MODEL), 1.0),
        'ln_in_g': 1.0 + nrm((D_MODEL,), 0.02),
        'ln_in_b': nrm((D_MODEL,), 0.02),
        'w_in': nrm((L, D_MODEL, IN_COLS), D_MODEL ** -0.5),
        'tshift_mu': jax.random.uniform(next(ks), (L, RWKV_COLS), f32),
        'w0': -6.0 + 5.0 * ratio ** 0.85 + nrm((L, C), 0.1),
        'w_decay_up': nrm((L, DECAY_RANK, C), 0.5 * DECAY_RANK ** -0.5),
        'a0': nrm((L, C), 0.1),
        'w_a_up': nrm((L, A_RANK, C), 0.5 * A_RANK ** -0.5),
        'w_g_up': nrm((L, GATE_RANK, C), GATE_RANK ** -0.5),
        'k_k': 0.85 + nrm((L, C), 0.02),
        'k_a': 1.0 + nrm((L, C), 0.02),
        'r_k': nrm((L, RWKV_HEADS, RWKV_HEAD_DIM), 0.1),
        'lnx_g': 1.0 + nrm((L, C), 0.02),
        'lnx_b': nrm((L, C), 0.02),
        'w_mem_kv': nrm((L, D_MODEL, 2 * MEM_WIDTH), D_MODEL ** -0.5),
        'w_out': nrm((L, MIX_WIDTH, D_MODEL), DEEPNORM_BETA * MIX_WIDTH ** -0.5),
        'ln1_g': 1.0 + nrm((L, D_MODEL), 0.02),
        'ln1_b': nrm((L, D_MODEL), 0.02),
        'router_group': nrm((L, D_MODEL, N_GROUPS), D_MODEL ** -0.5),
        'router_group_b': nrm((L, N_GROUPS), 0.01),
        'router_expert': nrm((L, D_MODEL, N_EXPERTS), D_MODEL ** -0.5),
        'router_expert_b': nrm((L, N_EXPERTS), 0.01),
        'w_e_gate': nrm((L, N_EXPERTS, D_MODEL, D_EXPERT), D_MODEL ** -0.5),
        'w_e_up': nrm((L, N_EXPERTS, D_MODEL, D_EXPERT), D_MODEL ** -0.5),
        'w_e_down': nrm((L, N_EXPERTS, D_EXPERT, D_MODEL), DEEPNORM_BETA * D_EXPERT ** -0.5),
        'ln2_g': 1.0 + nrm((L, D_MODEL), 0.02),
        'ln2_b': nrm((L, D_MODEL), 0.02),
    }


def reference(x, mem, ln_in_g, ln_in_b, w_in, tshift_mu, w0, w_decay_up, a0, w_a_up, w_g_up,
              k_k, k_a, r_k, lnx_g, lnx_b, w_mem_kv, w_out, ln1_g, ln1_b,
              router_group, router_group_b, router_expert, router_expert_b,
              w_e_gate, w_e_up, w_e_down, ln2_g, ln2_b):
    h = layer_norm(x, ln_in_g, ln_in_b)
    for l in range(DEPTH):
        mix = hybrid_mixer(h, mem, w_in[l], tshift_mu[l], w0[l], w_decay_up[l], a0[l], w_a_up[l],
                           w_g_up[l], k_k[l], k_a[l], r_k[l], lnx_g[l], lnx_b[l], w_mem_kv[l], w_out[l])
        h = layer_norm(DEEPNORM_ALPHA * h + mix, ln1_g[l], ln1_b[l])
        ffn = hierarchical_moe(h, router_group[l], router_group_b[l], router_expert[l], router_expert_b[l],
                               w_e_gate[l], w_e_up[l], w_e_down[l])
        h = layer_norm(DEEPNORM_ALPHA * h + ffn, ln2_g[l], ln2_b[l])
    return h
```

```python
import functools

import jax
import jax.numpy as jnp
from jax import lax
from jax.experimental import pallas as pl
from jax.experimental.pallas import tpu as pltpu

F32 = jnp.float32
BF16 = jnp.bfloat16

SB_HEADS = 4
MEM_HEADS = 4
N_GROUPS = 8
DEPTH = 1
DEEPNORM_ALPHA = (2.0 * DEPTH) ** 0.25
LN_EPS = 1e-5
GN_EPS = 64e-5

LANES = 128
WKV_CHUNK = 64
MOE_ROWS = 512
MOE_SUB = 128
VMEM_LIMIT = 56 * 1024 * 1024


def _cparams(sem):
    return pltpu.CompilerParams(dimension_semantics=sem, vmem_limit_bytes=VMEM_LIMIT)


def _layer_norm(x, g, b):
    mu = jnp.mean(x, axis=-1, keepdims=True)
    xc = x - mu
    var = jnp.mean(xc * xc, axis=-1, keepdims=True)
    return xc * lax.rsqrt(var + LN_EPS) * g + b


def _split3(x):
    hi = x.astype(BF16)
    r1 = x - hi.astype(F32)
    mid = r1.astype(BF16)
    lo = (r1 - mid.astype(F32)).astype(BF16)
    return hi, mid, lo


def _dot(a, b):
    return jnp.dot(a, b, preferred_element_type=F32)


def _dot_nt(a, b):
    return lax.dot_general(a, b, (((1,), (1,)), ((), ())), preferred_element_type=F32)


def _dot_tn(a, b):
    return lax.dot_general(a, b, (((0,), (0,)), ((), ())), preferred_element_type=F32)


def _dot_f32_by_exact(x, m):
    hi, mid, lo = _split3(x)
    return _dot(hi, m) + _dot(mid, m) + _dot(lo, m)


def _dot_exact_by_f32(m, x):
    hi, mid, lo = _split3(x)
    return _dot(m, hi) + _dot(m, mid) + _dot(m, lo)


def _dot_hp(a, b):
    return jnp.dot(a, b, preferred_element_type=F32, precision=lax.Precision.HIGHEST)


def _ln_inproj_kernel(x_ref, g_ref, b_ref, w_ref, h_ref, pr_ref, pa_ref, hb_ref, *, n_f32_tiles):
    n = pl.program_id(1)

    @pl.when(n == 0)
    def _():
        h = _layer_norm(x_ref[...], g_ref[...], b_ref[...])
        h_ref[...] = h
        hb_ref[...] = h.astype(BF16)

    p = _dot(hb_ref[...], w_ref[...])

    @pl.when(n < n_f32_tiles)
    def _():
        pr_ref[...] = p

    @pl.when(n >= n_f32_tiles)
    def _():
        pa_ref[...] = p.astype(BF16)


def _ln_inproj(x2, g, b, w_packed, n_rwkv_cols, tm, tn):
    m, d = x2.shape
    n_total = w_packed.shape[1]
    n_attn_cols = n_total - n_rwkv_cols
    nf = n_rwkv_cols // tn
    kern = functools.partial(_ln_inproj_kernel, n_f32_tiles=nf)
    return pl.pallas_call(
        kern,
        grid=(m // tm, n_total // tn),
        in_specs=[
            pl.BlockSpec((tm, d), lambda i, n: (i, 0)),
            pl.BlockSpec((1, d), lambda i, n: (0, 0)),
            pl.BlockSpec((1, d), lambda i, n: (0, 0)),
            pl.BlockSpec((d, tn), lambda i, n: (0, n)),
        ],
        out_specs=[
            pl.BlockSpec((tm, d), lambda i, n: (i, 0)),
            pl.BlockSpec((tm, tn), lambda i, n: (i, jnp.minimum(n, nf - 1))),
            pl.BlockSpec((tm, tn), lambda i, n: (i, jnp.maximum(n - nf, 0))),
        ],
        out_shape=[
            jax.ShapeDtypeStruct((m, d), F32),
            jax.ShapeDtypeStruct((m, n_rwkv_cols), F32),
            jax.ShapeDtypeStruct((m, n_attn_cols), BF16),
        ],
        scratch_shapes=[pltpu.VMEM((tm, d), BF16)],
        compiler_params=_cparams(("parallel", "arbitrary")),
        name="ln_inproj",
    )(x2, g, b, w_packed)


def _wkv_kernel(pr_ref, pk_ref, pv_ref, pl_ref,
                mur_ref, muk_ref, muv_ref, mul_ref,
                w0_ref, a0_ref, kk_ref, ka_ref, rk_ref, lg_ref, lb_ref,
                wd_ref, wa_ref, wg_ref,
                y_ref,
                s_ref, cr_ref, ck_ref, cv_ref, cl_ref,
                r_s, lw_s, k_s, v_s, a_s, b_s, y_s, g_s, bo_s,
                *, n_pairs, tt):
    C = WKV_CHUNK
    t_idx = pl.program_id(2)

    @pl.when(t_idx == 0)
    def _():
        s_ref[...] = jnp.zeros_like(s_ref)
        cr_ref[...] = jnp.zeros_like(cr_ref)
        ck_ref[...] = jnp.zeros_like(ck_ref)
        cv_ref[...] = jnp.zeros_like(cv_ref)
        cl_ref[...] = jnp.zeros_like(cl_ref)

    row = lax.broadcasted_iota(jnp.int32, (tt, 1), 0)

    def shifted(p_ref, carry_ref, mu_ref):
        p = p_ref[...]
        prev = pltpu.roll(p, shift=1, axis=0)
        prev = jnp.where(row == 0, carry_ref[...], prev)
        carry_ref[...] = p[tt - 1:tt, :]
        return p + (prev - p) * mu_ref[...]

    lane = lax.broadcasted_iota(jnp.int32, (LANES, LANES), 1)
    sub = lax.broadcasted_iota(jnp.int32, (LANES, LANES), 0)
    head_ones = jnp.where((lane // 64) == (sub // 64), 1.0, 0.0).astype(BF16)

    def head_sum(x):
        return _dot_f32_by_exact(x, head_ones)

    low = shifted(pl_ref, cl_ref, mul_ref)
    da = low[:, 0:LANES]
    th = jnp.tanh(da)
    sg = jax.nn.sigmoid(low[:, LANES:])
    r = shifted(pr_ref, cr_ref, mur_ref)
    k = shifted(pk_ref, ck_ref, muk_ref)
    v = shifted(pv_ref, cv_ref, muv_ref)
    for g in range(n_pairs):
        cs = slice(g * LANES, (g + 1) * LANES)
        rg, kg, vg = r[:, cs], k[:, cs], v[:, cs]
        pre = w0_ref[:, cs] + _dot_hp(th, wd_ref[:, cs])
        w_log = -jax.nn.softplus(-pre) - 0.5
        lw = -jnp.exp(w_log)
        a = jax.nn.sigmoid(a0_ref[:, cs] + _dot_hp(da, wa_ref[:, cs]))
        gate = _dot_hp(sg, wg_ref[:, cs])
        kk = kg * kk_ref[:, cs]
        kk = kk / jnp.maximum(jnp.sqrt(head_sum(kk * kk)), 1e-12)
        k2 = kg * (1.0 + (a - 1.0) * ka_ref[:, cs])
        bonus = head_sum(rg * k2 * rk_ref[:, cs]) * vg
        r_s[:, cs] = rg
        lw_s[:, cs] = lw
        k_s[:, cs] = k2
        v_s[:, cs] = vg
        a_s[:, cs] = -kk
        b_s[:, cs] = kk * a
        g_s[:, cs] = gate
        bo_s[:, cs] = bonus

    ci = lax.broadcasted_iota(jnp.int32, (C, 2 * C), 0)
    cj = lax.broadcasted_iota(jnp.int32, (C, 2 * C), 1)
    left = cj < C
    strict = (cj % C) < ci
    incl = (cj % C) <= ci
    tri_incl = jnp.where(lax.broadcasted_iota(jnp.int32, (C, C), 1)
                         <= lax.broadcasted_iota(jnp.int32, (C, C), 0), 1.0, 0.0).astype(BF16)
    lane_c = lax.broadcasted_iota(jnp.int32, (C, LANES), 1)
    m0 = lane_c < 64
    eye = jnp.where(lane == sub, 1.0, 0.0).astype(F32)
    blockdiag = (lane // 64) == (sub // 64)

    def chunk(c, _):
        rows = pl.ds(pl.multiple_of(c * C, C), C)
        for g in range(n_pairs):
            cs = slice(g * LANES, (g + 1) * LANES)
            rc, lwc, kc, vc = r_s[rows, cs], lw_s[rows, cs], k_s[rows, cs], v_s[rows, cs]
            ac, bc = a_s[rows, cs], b_s[rows, cs]
            cum = _dot_exact_by_f32(tri_incl, lwc)
            last = cum[C - 1:C, :]
            rt = rc * jnp.exp(cum)
            at = ac * jnp.exp(cum - lwc)
            ginv = jnp.exp(-cum)
            bt = bc * ginv
            kt = kc * ginv
            ghat = jnp.exp(last - cum)
            bh = (bc * ghat).astype(BF16)
            kh = (kc * ghat).astype(BF16)
            zero = jnp.zeros_like(at)
            at0 = jnp.where(m0, at, zero)
            at1 = jnp.where(m0, zero, at)
            rt0 = jnp.where(m0, rt, zero)
            rt1 = jnp.where(m0, zero, rt)
            btb, ktb, vb = bt.astype(BF16), kt.astype(BF16), vc.astype(BF16)
            lhs0 = jnp.concatenate([at0, rt0], axis=0).astype(BF16)
            lhs1 = jnp.concatenate([at1, rt1], axis=0).astype(BF16)
            x0 = _dot_nt(lhs0, jnp.concatenate([btb, ktb], axis=0))
            x1 = _dot_nt(lhs1, jnp.concatenate([ktb, btb], axis=0))
            a0x, m0x = x0[:C], x0[C:]
            a1x, m1x = x1[:C], x1[C:]
            n_bd = jnp.concatenate([jnp.where(left & strict, a0x, 0.0),
                                    jnp.where((~left) & strict, a1x, 0.0)], axis=0)
            ak = jnp.concatenate([jnp.where((~left) & strict, a0x, 0.0),
                                  jnp.where(left & strict, a1x, 0.0)], axis=0).astype(BF16)
            mm0 = jnp.where(incl, m0x, 0.0).astype(BF16)
            mm1 = jnp.where(incl, m1x, 0.0).astype(BF16)
            t_bd = eye + n_bd
            pw = n_bd
            for _ in range(5):
                pwb = pw.astype(BF16)
                pw = _dot(pwb, pwb)
                t_bd = t_bd + _dot(t_bd.astype(BF16), pw.astype(BF16))
            av = _dot(ak, jnp.concatenate([vb, vb], axis=0))
            av = jnp.concatenate([jnp.where(m0, av[:C], 0.0), jnp.where(m0, 0.0, av[C:])], axis=0)
            at_bd = jnp.concatenate([at0, at1], axis=0)
            tb = t_bd.astype(BF16)
            w_bd = _dot(tb, at_bd.astype(BF16))
            u0_bd = _dot(tb, av.astype(BF16))
            w = w_bd[:C] + w_bd[C:]
            u0 = u0_bd[:C] + u0_bd[C:]
            s = s_ref[g]
            sb = s.astype(BF16)
            u = _dot_nt(w.astype(BF16), sb) + u0
            ub = u.astype(BF16)
            uv = jnp.concatenate([ub, vb], axis=0)
            vu = jnp.concatenate([vb, ub], axis=0)
            y = _dot_nt(rt.astype(BF16), sb) + jnp.where(m0, _dot(mm0, uv), _dot(mm1, vu))
            upd = _dot_tn(uv, jnp.concatenate([bh, kh], axis=0))
            s_ref[g] = s * jnp.exp(last) + jnp.where(blockdiag, upd, 0.0)
            y_s[rows, cs] = y
        return 0

    lax.fori_loop(0, tt // C, chunk, 0)

    for g in range(n_pairs):
        cs = slice(g * LANES, (g + 1) * LANES)
        y = y_s[:, cs]
        mean = head_sum(y) * (1.0 / 64.0)
        yc = y - mean
        var = head_sum(yc * yc) * (1.0 / 64.0)
        yn = yc * lax.rsqrt(var + GN_EPS) * lg_ref[:, cs] + lb_ref[:, cs]
        y_ref[:, cs] = ((yn + bo_s[:, cs]) * g_s[:, cs]).astype(BF16)


def _wkv(p_rwkv, prm, batch, seq, n_pairs, tt):
    m = p_rwkv.shape[0]
    c = prm["w0"].shape[1]
    gw = n_pairs * LANES
    n_col_blocks = c // gw
    nt = seq // tt
    low_w = prm["mu_low"].shape[1]

    def pspec(off):
        return pl.BlockSpec((tt, gw), lambda b, g, t: (b * nt + t, off * n_col_blocks + g))

    def vspec():
        return pl.BlockSpec((1, gw), lambda b, g, t: (0, g))

    def wspec(rows):
        return pl.BlockSpec((rows, gw), lambda b, g, t: (0, g))

    kern = functools.partial(_wkv_kernel, n_pairs=n_pairs, tt=tt)
    tile = pltpu.VMEM((tt, gw), F32)
    return pl.pallas_call(
        kern,
        grid=(batch, n_col_blocks, nt),
        in_specs=[
            pspec(0), pspec(1), pspec(2),
            pl.BlockSpec((tt, low_w), lambda b, g, t: (b * nt + t, (3 * c) // low_w)),
            vspec(), vspec(), vspec(),
            pl.BlockSpec((1, low_w), lambda b, g, t: (0, 0)),
            vspec(), vspec(), vspec(), vspec(), vspec(), vspec(), vspec(),
            wspec(LANES), wspec(LANES), wspec(low_w - LANES),
        ],
        out_specs=pl.BlockSpec((tt, gw), lambda b, g, t: (b * nt + t, g)),
        out_shape=jax.ShapeDtypeStruct((m, c), BF16),
        scratch_shapes=[
            pltpu.VMEM((n_pairs, LANES, LANES), F32),
            pltpu.VMEM((1, gw), F32), pltpu.VMEM((1, gw), F32), pltpu.VMEM((1, gw), F32),
            pltpu.VMEM((1, low_w), F32),
            tile, tile, tile, tile, tile, tile, tile, tile, tile,
        ],
        compiler_params=_cparams(("parallel", "parallel", "arbitrary")),
        name="wkv7",
    )(p_rwkv, p_rwkv, p_rwkv, p_rwkv,
      prm["mu_r"], prm["mu_k"], prm["mu_v"], prm["mu_low"],
      prm["w0"], prm["a0"], prm["k_k"], prm["k_a"], prm["r_k"], prm["lnx_g"], prm["lnx_b"],
      prm["wd"], prm["wa"], prm["wg"])


def _sb_kernel(q_ref, k_ref, v_ref, o_ref, *, tq, scale):
    i = pl.program_id(2)
    q = q_ref[...]
    row = lax.broadcasted_iota(jnp.int32, (tq, tq), 0)
    col = lax.broadcasted_iota(jnp.int32, (tq, tq), 1)
    r2 = lax.broadcasted_iota(jnp.int32, (tq, 2 * tq), 0)
    c2 = lax.broadcasted_iota(jnp.int32, (tq, 2 * tq), 1)
    after_and_total = jnp.where((c2 >= tq) | (r2 > c2), 1.0, 0.0).astype(BF16)

    def body(jj, carry):
        acc, right = carry
        j = i - jj
        ks = pl.ds(pl.multiple_of(j * tq, tq), tq)
        z = _dot_nt(q, k_ref[ks, :]) * scale
        causal = (col + j * tq) < (row + i * tq)
        sp = jnp.maximum(z, 0.0) + jnp.log1p(jnp.exp(-jnp.abs(z)))
        log_keep = jnp.where(causal, -sp, 0.0)
        hi = log_keep.astype(BF16)
        lo = (log_keep - hi.astype(F32)).astype(BF16)
        sums = _dot(hi, after_and_total) + _dot(lo, after_and_total)
        after = sums[:, :tq] + right
        attn = jnp.where(causal, jnp.exp(z - sp + after), 0.0)
        acc = acc + _dot(attn.astype(BF16), v_ref[ks, :])
        return acc, right + sums[:, tq:]

    zero = jnp.zeros((tq, tq), F32)
    acc, _ = lax.fori_loop(0, i + 1, body, (zero, zero))
    o_ref[...] = acc.astype(BF16)


def _sb_attn(p_attn, batch, seq, tq):
    m = p_attn.shape[0]
    hd = LANES
    nq = seq // tq
    kern = functools.partial(_sb_kernel, tq=tq, scale=hd ** -0.5)
    return pl.pallas_call(
        kern,
        grid=(batch, SB_HEADS, nq),
        in_specs=[
            pl.BlockSpec((tq, hd), lambda b, h, i: (b * nq + i, h)),
            pl.BlockSpec((seq, hd), lambda b, h, i: (b, SB_HEADS + h)),
            pl.BlockSpec((seq, hd), lambda b, h, i: (b, 2 * SB_HEADS + h)),
        ],
        out_specs=pl.BlockSpec((tq, hd), lambda b, h, i: (b * nq + i, h)),
        out_shape=jax.ShapeDtypeStruct((m, SB_HEADS * hd), BF16),
        compiler_params=_cparams(("parallel", "parallel", "arbitrary")),
        name="sb_attn",
    )(p_attn, p_attn, p_attn)


def _mem_kv_kernel(m_ref, w_ref, o_ref):
    o_ref[...] = _dot(m_ref[...].astype(BF16), w_ref[...]).astype(BF16)


def _mem_kv(mem2, w_bf16, tm):
    m, d = mem2.shape
    n = w_bf16.shape[1]
    return pl.pallas_call(
        _mem_kv_kernel,
        grid=(m // tm,),
        in_specs=[pl.BlockSpec((tm, d), lambda i: (i, 0)),
                  pl.BlockSpec((d, n), lambda i: (0, 0))],
        out_specs=pl.BlockSpec((tm, n), lambda i: (i, 0)),
        out_shape=jax.ShapeDtypeStruct((m, n), BF16),
        compiler_params=_cparams(("parallel",)),
        name="mem_kv",
    )(mem2, w_bf16)


def _mem_attn_kernel(q_ref, k_ref, v_ref, o_ref, *, scale):
    s = _dot_nt(q_ref[...], k_ref[...]) * scale
    s = s - jnp.max(s, axis=-1, keepdims=True)
    e = jnp.exp(s)
    p = e / jnp.sum(e, axis=-1, keepdims=True)
    o_ref[...] = _dot(p.astype(BF16), v_ref[...]).astype(BF16)


def _mem_attn(p_attn, kv, batch, seq, mem_len, tq):
    m = p_attn.shape[0]
    hd = LANES
    nq = seq // tq
    q_off = 3 * SB_HEADS
    kern = functools.partial(_mem_attn_kernel, scale=hd ** -0.5)
    return pl.pallas_call(
        kern,
        grid=(batch, MEM_HEADS, nq),
        in_specs=[
            pl.BlockSpec((tq, hd), lambda b, h, i: (b * nq + i, q_off + h)),
            pl.BlockSpec((mem_len, hd), lambda b, h, i: (b, h)),
            pl.BlockSpec((mem_len, hd), lambda b, h, i: (b, MEM_HEADS + h)),
        ],
        out_specs=pl.BlockSpec((tq, hd), lambda b, h, i: (b * nq + i, h)),
        out_shape=jax.ShapeDtypeStruct((m, MEM_HEADS * hd), BF16),
        compiler_params=_cparams(("parallel", "parallel", "parallel")),
        name="mem_attn",
    )(p_attn, kv, kv)


def _outproj_kernel(yr_ref, ys_ref, ym_ref, h_ref, wr_ref, ws_ref, wm_ref, g_ref, b_ref,
                    rw_ref, rb_ref, h1_ref, lg_ref):
    mix = _dot(yr_ref[...], wr_ref[...]) + _dot(ys_ref[...], ws_ref[...]) + _dot(ym_ref[...], wm_ref[...])
    h1 = _layer_norm(DEEPNORM_ALPHA * h_ref[...] + mix, g_ref[...], b_ref[...])
    h1_ref[...] = h1
    lg_ref[...] = _dot_hp(h1, rw_ref[...]) + rb_ref[...]


def _outproj(y_r, y_s, y_m, h, w_r, w_s, w_m, g, b, r_w, r_b, tm):
    m, d = h.shape
    full = lambda a: pl.BlockSpec(a.shape, lambda i: (0, 0))
    rows = lambda a: pl.BlockSpec((tm, a.shape[1]), lambda i: (i, 0))
    return pl.pallas_call(
        _outproj_kernel,
        grid=(m // tm,),
        in_specs=[rows(y_r), rows(y_s), rows(y_m), rows(h), full(w_r), full(w_s), full(w_m),
                  full(g), full(b), full(r_w), full(r_b)],
        out_specs=[pl.BlockSpec((tm, d), lambda i: (i, 0)),
                   pl.BlockSpec((tm, LANES), lambda i: (i, 0))],
        out_shape=[jax.ShapeDtypeStruct((m, d), F32),
                   jax.ShapeDtypeStruct((m, LANES), F32)],
        compiler_params=_cparams(("parallel",)),
        name="outproj",
    )(y_r, y_s, y_m, h, w_r, w_s, w_m, g, b, r_w, r_b)


def _route_kernel(lg_ref, id_ref, wt_ref, *, n_groups, per_group):
    lg = lg_ref[...]
    lane_i = lax.broadcasted_iota(jnp.int32, lg.shape, 1)
    lane = lane_i.astype(F32)
    neg = jnp.float32(-jnp.inf)
    big = jnp.float32(2 ** 20)

    def first_max(vals):
        mx = jnp.max(vals, axis=-1, keepdims=True)
        idx = jnp.min(jnp.where(vals == mx, lane, big), axis=-1, keepdims=True)
        return mx, idx

    is_group = lane < n_groups
    gmax, gidx = first_max(jnp.where(is_group, lg, neg))
    gsum = jnp.sum(jnp.where(is_group, jnp.exp(lg - gmax), 0.0), axis=-1, keepdims=True)
    group_w = 1.0 / gsum
    lo = n_groups + gidx * per_group
    in_group = (lane >= lo) & (lane < lo + per_group)
    v1, i1 = first_max(jnp.where(in_group, lg, neg))
    v2, i2 = first_max(jnp.where(in_group & (lane != i1), lg, neg))
    e2 = jnp.exp(v2 - v1)
    w1 = group_w / (1.0 + e2)
    w2 = group_w * e2 / (1.0 + e2)
    e1 = (i1 - n_groups).astype(jnp.int32)
    e2i = (i2 - n_groups).astype(jnp.int32)
    id_ref[...] = jnp.where(lane_i == 0, e1, jnp.where(lane_i == 1, e2i, 0))
    wt_ref[...] = jnp.where(lane_i == 0, w1, jnp.where(lane_i == 1, w2, 0.0))


def _route(logits, n_groups, per_group, tm):
    m = logits.shape[0]
    kern = functools.partial(_route_kernel, n_groups=n_groups, per_group=per_group)
    spec = pl.BlockSpec((tm, LANES), lambda i: (i, 0))
    return pl.pallas_call(
        kern,
        grid=(m // tm,),
        in_specs=[spec],
        out_specs=[spec, spec],
        out_shape=[jax.ShapeDtypeStruct((m, LANES), jnp.int32),
                   jax.ShapeDtypeStruct((m, LANES), F32)],
        compiler_params=_cparams(("parallel",)),
        name="route",
    )(logits)


def _row_copy(src_hbm, dst_vmem, sem, src_row, dst_row):
    return pltpu.make_async_copy(src_hbm.at[pl.ds(src_row, 1)], dst_vmem.at[pl.ds(dst_row, 1)], sem)


def _gather_kernel(tok_ref, nused_ref, h_hbm, o_ref, sem, *, rows):
    i = pl.program_id(0)

    @pl.when(i < nused_ref[0])
    def _():
        base = i * rows

        def issue(r, _):
            _row_copy(h_hbm, o_ref, sem, tok_ref[base + r], r).start()
            return 0

        lax.fori_loop(0, rows, issue, 0)

        def drain(r, _):
            _row_copy(h_hbm, o_ref, sem, 0, r).wait()
            return 0

        lax.fori_loop(0, rows, drain, 0)

    @pl.when(i >= nused_ref[0])
    def _():
        o_ref[...] = jnp.zeros_like(o_ref)


def _moe_gather(slot_tok, n_used, h1, n_blocks, rows):
    d = h1.shape[1]
    kern = functools.partial(_gather_kernel, rows=rows)
    return pl.pallas_call(
        kern,
        grid_spec=pltpu.PrefetchScalarGridSpec(
            num_scalar_prefetch=2,
            grid=(n_blocks,),
            in_specs=[pl.BlockSpec(memory_space=pl.ANY)],
            out_specs=pl.BlockSpec((rows, d), lambda i, tok, nu: (i, 0)),
            scratch_shapes=[pltpu.SemaphoreType.DMA],
        ),
        out_shape=jax.ShapeDtypeStruct((n_blocks * rows, d), F32),
        compiler_params=_cparams(("arbitrary",)),
        name="moe_gather",
    )(slot_tok, n_used, h1)


def _ffn_kernel(be_ref, ns_ref, nu_ref, x_ref, wg_ref, wu_ref, wd_ref, o_ref, xb_ref, *, rows, sub):
    i = pl.program_id(0)
    j = pl.program_id(1)
    used = i < nu_ref[0]
    n_sub = ns_ref[i]

    @pl.when(used & (j == 0))
    def _():
        xb_ref[...] = x_ref[...].astype(BF16)
        o_ref[...] = jnp.zeros_like(o_ref)

    @pl.when(jnp.logical_not(used) & (j == 0))
    def _():
        o_ref[...] = jnp.zeros_like(o_ref)

    for n in range(1, rows // sub + 1):
        @pl.when(used & (n_sub == n))
        def _(n=n):
            r = n * sub
            xb = xb_ref[0:r, :]
            gate = _dot(xb, wg_ref[...].astype(BF16))
            up = _dot(xb, wu_ref[...].astype(BF16))
            hid = (gate * jax.nn.sigmoid(gate)) * up
            o_ref[0:r, :] += _dot(hid.astype(BF16), wd_ref[...].astype(BF16))


def _moe_ffn(block_expert, n_sub, n_used, xs, w_gate, w_up, w_down, rows, tf):
    n_slots, d = xs.shape
    n_blocks = n_slots // rows
    de = w_gate.shape[2]
    nj = de // tf
    kern = functools.partial(_ffn_kernel, rows=rows, sub=MOE_SUB)

    def blk(i, nu):
        return jnp.minimum(i, nu[0] - 1)

    def jidx(i, j, nu):
        return jnp.where(i < nu[0], j, nj - 1)

    return pl.pallas_call(
        kern,
        grid_spec=pltpu.PrefetchScalarGridSpec(
            num_scalar_prefetch=3,
            grid=(n_blocks, nj),
            in_specs=[
                pl.BlockSpec((rows, d), lambda i, j, be, ns, nu: (blk(i, nu), 0)),
                pl.BlockSpec((None, d, tf), lambda i, j, be, ns, nu: (be[i], 0, jidx(i, j, nu))),
                pl.BlockSpec((None, d, tf), lambda i, j, be, ns, nu: (be[i], 0, jidx(i, j, nu))),
                pl.BlockSpec((None, tf, d), lambda i, j, be, ns, nu: (be[i], jidx(i, j, nu), 0)),
            ],
            out_specs=pl.BlockSpec((rows, d), lambda i, j, be, ns, nu: (i, 0)),
            scratch_shapes=[pltpu.VMEM((rows, d), BF16)],
        ),
        out_shape=jax.ShapeDtypeStruct((n_slots, d), F32),
        compiler_params=_cparams(("arbitrary", "arbitrary")),
        name="moe_ffn",
    )(block_expert, n_sub, n_used, xs, w_gate, w_up, w_down)


def _combine_kernel(pos_ref, h_ref, wt_ref, g_ref, b_ref, y_hbm, o_ref, buf_ref, sem, *, tm):
    base = pl.program_id(0) * tm

    def issue(t, _):
        _row_copy(y_hbm, buf_ref.at[0], sem, pos_ref[2 * (base + t)], t).start()
        _row_copy(y_hbm, buf_ref.at[1], sem, pos_ref[2 * (base + t) + 1], t).start()
        return 0

    lax.fori_loop(0, tm, issue, 0)

    def drain(t, _):
        _row_copy(y_hbm, buf_ref.at[0], sem, 0, t).wait()
        _row_copy(y_hbm, buf_ref.at[1], sem, 0, t).wait()
        return 0

    lax.fori_loop(0, tm, drain, 0)
    wt = wt_ref[...]
    ffn = buf_ref[0] * wt[:, 0:1] + buf_ref[1] * wt[:, 1:2]
    o_ref[...] = _layer_norm(DEEPNORM_ALPHA * h_ref[...] + ffn, g_ref[...], b_ref[...])


def _combine(pos, h1, wts, g, b, ys, tm):
    m, d = h1.shape
    kern = functools.partial(_combine_kernel, tm=tm)
    return pl.pallas_call(
        kern,
        grid_spec=pltpu.PrefetchScalarGridSpec(
            num_scalar_prefetch=1,
            grid=(m // tm,),
            in_specs=[
                pl.BlockSpec((tm, d), lambda i, pos: (i, 0)),
                pl.BlockSpec((tm, LANES), lambda i, pos: (i, 0)),
                pl.BlockSpec((1, d), lambda i, pos: (0, 0)),
                pl.BlockSpec((1, d), lambda i, pos: (0, 0)),
                pl.BlockSpec(memory_space=pl.ANY),
            ],
            out_specs=pl.BlockSpec((tm, d), lambda i, pos: (i, 0)),
            scratch_shapes=[pltpu.VMEM((2, tm, d), F32), pltpu.SemaphoreType.DMA],
        ),
        out_shape=jax.ShapeDtypeStruct((m, d), F32),
        compiler_params=_cparams(("arbitrary",)),
        name="moe_combine",
    )(pos, h1, wts, g, b, ys)


def _dispatch_plan(expert_ids, n_experts, rows, sub):
    n_tok = expert_ids.shape[0]
    n_assign = n_tok * 2
    n_blocks = -(-n_assign // rows) + n_experts
    flat_e = expert_ids.reshape(-1)
    counts = jnp.sum(flat_e[:, None] == jnp.arange(n_experts, dtype=jnp.int32)[None, :], axis=0,
                     dtype=jnp.int32)
    padded = (counts + rows - 1) // rows * rows
    pad_end = jnp.cumsum(padded)
    pad_start = pad_end - padded
    start = jnp.cumsum(counts) - counts
    order = jnp.argsort(flat_e, stable=True).astype(jnp.int32)
    sorted_e = flat_e[order]
    dest = pad_start[sorted_e] + jnp.arange(n_assign, dtype=jnp.int32) - start[sorted_e]
    slot_tok = jnp.zeros((n_blocks * rows,), jnp.int32).at[dest].set(order // 2)
    pos = jnp.zeros((n_assign,), jnp.int32).at[order].set(dest)
    n_used = (pad_end[-1] // rows).astype(jnp.int32)
    blk = jnp.arange(n_blocks, dtype=jnp.int32)
    blk_c = jnp.minimum(blk, n_used - 1)
    block_expert = jnp.minimum(jnp.searchsorted(pad_end, blk_c * rows, side="right"),
                               n_experts - 1).astype(jnp.int32)
    valid = jnp.clip(counts[block_expert] - (blk_c * rows - pad_start[block_expert]), 0, rows)
    n_sub = jnp.where(blk < n_used, (valid + sub - 1) // sub, 0).astype(jnp.int32)
    return slot_tok, pos, n_used.reshape(1), block_expert, n_sub, n_blocks


def _pick(n, pref):
    t = min(pref, n)
    while n % t:
        t //= 2
    return t


def kernel(x, mem, ln_in_g, ln_in_b, w_in, tshift_mu, w0, w_decay_up, a0, w_a_up, w_g_up, k_k, k_a, r_k,
           lnx_g, lnx_b, w_mem_kv, w_out, ln1_g, ln1_b, router_group, router_group_b, router_expert,
           router_expert_b, w_e_gate, w_e_up, w_e_down, ln2_g, ln2_b):
    batch, seq, d = x.shape
    mem_len = mem.shape[1]
    m = batch * seq
    c = w0.shape[1]
    dr, ar, gr = w_decay_up.shape[1], w_a_up.shape[1], w_g_up.shape[1]
    rwkv_cols = 3 * c + dr + ar + gr
    sb_w = SB_HEADS * LANES
    mem_w = MEM_HEADS * LANES
    assert dr + ar == LANES and c % (2 * LANES) == 0 and w_in.shape[0] == DEPTH
    assert w_in.shape[2] == rwkv_cols + 3 * sb_w + mem_w
    n_experts = router_expert.shape[2]
    row = lambda a: a.reshape(1, -1)

    tn = 512
    low_w = -(-(dr + ar + gr) // LANES) * LANES
    rw_pad = -(-(3 * c + low_w) // tn) * tn
    wi = w_in[0]
    zpad = lambda n: jnp.zeros((d, n), F32)
    w_packed = jnp.concatenate(
        [wi[:, :rwkv_cols], zpad(rw_pad - rwkv_cols), wi[:, rwkv_cols:]], axis=1).astype(BF16)
    mu = tshift_mu[0]
    lp = low_w - (dr + ar + gr)
    prm = dict(
        mu_r=row(mu[:c]), mu_k=row(mu[c:2 * c]), mu_v=row(mu[2 * c:3 * c]),
        mu_low=row(jnp.concatenate([mu[3 * c:rwkv_cols], jnp.zeros((lp,), F32)])),
        w0=row(w0[0]), a0=row(a0[0]), k_k=row(k_k[0]), k_a=row(k_a[0]), r_k=row(r_k[0]),
        lnx_g=row(lnx_g[0]), lnx_b=row(lnx_b[0]),
        wd=jnp.concatenate([w_decay_up[0], jnp.zeros((ar, c), F32)], axis=0),
        wa=jnp.concatenate([jnp.zeros((dr, c), F32), w_a_up[0]], axis=0),
        wg=jnp.concatenate([w_g_up[0], jnp.zeros((lp, c), F32)], axis=0),
    )
    assert (3 * c) % low_w == 0

    x2 = x.reshape(m, d)
    tm = _pick(m, 512)
    h, p_rwkv, p_attn = _ln_inproj(x2, row(ln_in_g), row(ln_in_b), w_packed, rw_pad, tm, tn)

    y_rwkv = _wkv(p_rwkv, prm, batch, seq, n_pairs=2, tt=_pick(seq, 512))
    y_sb = _sb_attn(p_attn, batch, seq, tq=LANES)
    kv = _mem_kv(mem.reshape(batch * mem_len, d), w_mem_kv[0].astype(BF16), _pick(batch * mem_len, 256))
    y_mem = _mem_attn(p_attn, kv, batch, seq, mem_len, tq=_pick(seq, 512))

    wo = w_out[0].astype(BF16)
    r_w = jnp.concatenate([router_group[0], router_expert[0],
                           jnp.zeros((d, LANES - N_GROUPS - n_experts), F32)], axis=1)
    r_b = jnp.concatenate([router_group_b[0], router_expert_b[0],
                           jnp.zeros((LANES - N_GROUPS - n_experts,), F32)]).reshape(1, LANES)
    h1, logits = _outproj(y_rwkv, y_sb, y_mem, h, wo[:c], wo[c:c + sb_w], wo[c + sb_w:],
                          row(ln1_g[0]), row(ln1_b[0]), r_w, r_b, _pick(m, 256))

    ids, wts = _route(logits, N_GROUPS, n_experts // N_GROUPS, _pick(m, 512))
    slot_tok, pos, n_used, block_expert, n_sub, n_blocks = _dispatch_plan(
        ids[:, :2], n_experts, MOE_ROWS, MOE_SUB)
    xs = _moe_gather(slot_tok, n_used, h1, n_blocks, MOE_ROWS)
    ys = _moe_ffn(block_expert, n_sub, n_used, xs, w_e_gate[0], w_e_up[0], w_e_down[0], MOE_ROWS, tf=256)
    out = _combine(pos, h1, wts, row(ln2_g[0]), row(ln2_b[0]), ys, _pick(m, 256))
    return out.reshape(batch, seq, d)
```

```python
import functools

import jax
import jax.numpy as jnp
from jax import lax
from jax.experimental import pallas as pl
from jax.experimental.pallas import tpu as pltpu

F32 = jnp.float32
BF16 = jnp.bfloat16

SB_HEADS = 4
MEM_HEADS = 4
N_GROUPS = 8
DEPTH = 1
DEEPNORM_ALPHA = (2.0 * DEPTH) ** 0.25
LN_EPS = 1e-5
GN_EPS = 64e-5

LANES = 128
WKV_CHUNK = 64
MOE_ROWS = 512
MOE_SUB = 128
VMEM_LIMIT = 56 * 1024 * 1024


def _cparams(sem):
    return pltpu.CompilerParams(dimension_semantics=sem, vmem_limit_bytes=VMEM_LIMIT)


def _layer_norm(x, g, b):
    mu = jnp.mean(x, axis=-1, keepdims=True)
    xc = x - mu
    var = jnp.mean(xc * xc, axis=-1, keepdims=True)
    return xc * lax.rsqrt(var + LN_EPS) * g + b


def _split3(x):
    hi = x.astype(BF16)
    r1 = x - hi.astype(F32)
    mid = r1.astype(BF16)
    lo = (r1 - mid.astype(F32)).astype(BF16)
    return hi, mid, lo


def _dot(a, b):
    return jnp.dot(a, b, preferred_element_type=F32)


def _dot_nt(a, b):
    return lax.dot_general(a, b, (((1,), (1,)), ((), ())), preferred_element_type=F32)


def _dot_tn(a, b):
    return lax.dot_general(a, b, (((0,), (0,)), ((), ())), preferred_element_type=F32)


def _dot_f32_by_exact(x, m):
    hi, mid, lo = _split3(x)
    return _dot(hi, m) + _dot(mid, m) + _dot(lo, m)


def _dot_exact_by_f32(m, x):
    hi, mid, lo = _split3(x)
    return _dot(m, hi) + _dot(m, mid) + _dot(m, lo)


def _dot_hp(a, b):
    return jnp.dot(a, b, preferred_element_type=F32, precision=lax.Precision.HIGHEST)


def _ln_inproj_kernel(x_ref, g_ref, b_ref, w_ref, h_ref, pr_ref, pa_ref, hb_ref, *, n_f32_tiles):
    n = pl.program_id(1)

    @pl.when(n == 0)
    def _():
        h = _layer_norm(x_ref[...], g_ref[...], b_ref[...])
        h_ref[...] = h
        hb_ref[...] = h.astype(BF16)

    p = _dot(hb_ref[...], w_ref[...])

    @pl.when(n < n_f32_tiles)
    def _():
        pr_ref[...] = p

    @pl.when(n >= n_f32_tiles)
    def _():
        pa_ref[...] = p.astype(BF16)


def _ln_inproj(x2, g, b, w_packed, n_rwkv_cols, tm, tn):
    m, d = x2.shape
    n_total = w_packed.shape[1]
    n_attn_cols = n_total - n_rwkv_cols
    nf = n_rwkv_cols // tn
    kern = functools.partial(_ln_inproj_kernel, n_f32_tiles=nf)
    return pl.pallas_call(
        kern,
        grid=(m // tm, n_total // tn),
        in_specs=[
            pl.BlockSpec((tm, d), lambda i, n: (i, 0)),
            pl.BlockSpec((1, d), lambda i, n: (0, 0)),
            pl.BlockSpec((1, d), lambda i, n: (0, 0)),
            pl.BlockSpec((d, tn), lambda i, n: (0, n)),
        ],
        out_specs=[
            pl.BlockSpec((tm, d), lambda i, n: (i, 0)),
            pl.BlockSpec((tm, tn), lambda i, n: (i, jnp.minimum(n, nf - 1))),
            pl.BlockSpec((tm, tn), lambda i, n: (i, jnp.maximum(n - nf, 0))),
        ],
        out_shape=[
            jax.ShapeDtypeStruct((m, d), F32),
            jax.ShapeDtypeStruct((m, n_rwkv_cols), F32),
            jax.ShapeDtypeStruct((m, n_attn_cols), BF16),
        ],
        scratch_shapes=[pltpu.VMEM((tm, d), BF16)],
        compiler_params=_cparams(("parallel", "arbitrary")),
        name="ln_inproj",
    )(x2, g, b, w_packed)


def _wkv_kernel(pr_ref, pk_ref, pv_ref, pl_ref,
                mur_ref, muk_ref, muv_ref, mul_ref,
                w0_ref, a0_ref, kk_ref, ka_ref, rk_ref, lg_ref, lb_ref,
                wd_ref, wa_ref, wg_ref,
                y_ref,
                s_ref, cr_ref, ck_ref, cv_ref, cl_ref,
                r_s, lw_s, k_s, v_s, a_s, b_s, y_s, g_s, bo_s,
                *, n_pairs, tt):
    C = WKV_CHUNK
    t_idx = pl.program_id(2)

    @pl.when(t_idx == 0)
    def _():
        s_ref[...] = jnp.zeros_like(s_ref)
        cr_ref[...] = jnp.zeros_like(cr_ref)
        ck_ref[...] = jnp.zeros_like(ck_ref)
        cv_ref[...] = jnp.zeros_like(cv_ref)
        cl_ref[...] = jnp.zeros_like(cl_ref)

    row = lax.broadcasted_iota(jnp.int32, (tt, 1), 0)

    def shifted(p_ref, carry_ref, mu_ref):
        p = p_ref[...]
        prev = pltpu.roll(p, shift=1, axis=0)
        prev = jnp.where(row == 0, carry_ref[...], prev)
        carry_ref[...] = p[tt - 1:tt, :]
        return p + (prev - p) * mu_ref[...]

    lane = lax.broadcasted_iota(jnp.int32, (LANES, LANES), 1)
    sub = lax.broadcasted_iota(jnp.int32, (LANES, LANES), 0)
    head_ones = jnp.where((lane // 64) == (sub // 64), 1.0, 0.0).astype(BF16)

    def head_sum(x):
        hi = x.astype(BF16)
        lo = (x - hi.astype(F32)).astype(BF16)
        return _dot(hi, head_ones) + _dot(lo, head_ones)

    low = shifted(pl_ref, cl_ref, mul_ref)
    da = low[:, 0:LANES]
    th = jnp.tanh(da).astype(BF16)
    sg = jax.nn.sigmoid(low[:, LANES:]).astype(BF16)
    da = da.astype(BF16)
    r = shifted(pr_ref, cr_ref, mur_ref)
    k = shifted(pk_ref, ck_ref, muk_ref)
    v = shifted(pv_ref, cv_ref, muv_ref)
    for g in range(n_pairs):
        cs = slice(g * LANES, (g + 1) * LANES)
        rg, kg, vg = r[:, cs], k[:, cs], v[:, cs]
        pre = w0_ref[:, cs] + _dot(th, wd_ref[:, cs])
        w_log = -jax.nn.softplus(-pre) - 0.5
        lw = -jnp.exp(w_log)
        a = jax.nn.sigmoid(a0_ref[:, cs] + _dot(da, wa_ref[:, cs]))
        gate = _dot(sg, wg_ref[:, cs])
        kk = kg * kk_ref[:, cs]
        kk = kk / jnp.maximum(jnp.sqrt(head_sum(kk * kk)), 1e-12)
        k2 = kg * (1.0 + (a - 1.0) * ka_ref[:, cs])
        bonus = head_sum(rg * k2 * rk_ref[:, cs]) * vg
        r_s[:, cs] = rg
        lw_s[:, cs] = lw
        k_s[:, cs] = k2
        v_s[:, cs] = vg
        a_s[:, cs] = -kk
        b_s[:, cs] = kk * a
        g_s[:, cs] = gate
        bo_s[:, cs] = bonus

    ci = lax.broadcasted_iota(jnp.int32, (C, 2 * C), 0)
    cj = lax.broadcasted_iota(jnp.int32, (C, 2 * C), 1)
    left = cj < C
    strict = (cj % C) < ci
    incl = (cj % C) <= ci
    tri_incl = jnp.where(lax.broadcasted_iota(jnp.int32, (C, C), 1)
                         <= lax.broadcasted_iota(jnp.int32, (C, C), 0), 1.0, 0.0).astype(BF16)
    lane_c = lax.broadcasted_iota(jnp.int32, (C, LANES), 1)
    m0 = lane_c < 64
    eye = jnp.where(lane == sub, 1.0, 0.0).astype(F32)
    blockdiag = (lane // 64) == (sub // 64)

    csl = [slice(g * LANES, (g + 1) * LANES) for g in range(n_pairs)]
    P = range(n_pairs)
    cat0 = lambda *xs: jnp.concatenate(xs, axis=0)
    cat1 = lambda *xs: jnp.concatenate(xs, axis=1)
    bf = lambda x: x.astype(BF16)

    def chunk(c, _):
        rows = pl.ds(pl.multiple_of(c * C, C), C)
        ld = lambda ref: [ref[rows, csl[g]] for g in P]
        rc, lwc, kc, vc, ac, bc = ld(r_s), ld(lw_s), ld(k_s), ld(v_s), ld(a_s), ld(b_s)
        cum = [_dot_exact_by_f32(tri_incl, lwc[g]) for g in P]
        last = [cum[g][C - 1:C, :] for g in P]
        rt = [rc[g] * jnp.exp(cum[g]) for g in P]
        at = [ac[g] * jnp.exp(cum[g] - lwc[g]) for g in P]
        ginv = [jnp.exp(-cum[g]) for g in P]
        btb = [bf(bc[g] * ginv[g]) for g in P]
        ktb = [bf(kc[g] * ginv[g]) for g in P]
        ghat = [jnp.exp(last[g] - cum[g]) for g in P]
        bk = [cat0(bf(bc[g] * ghat[g]), bf(kc[g] * ghat[g])) for g in P]
        vb = [bf(vc[g]) for g in P]
        rtb = [bf(rt[g]) for g in P]
        lhs0 = [bf(cat0(jnp.where(m0, at[g], 0.0), jnp.where(m0, rt[g], 0.0))) for g in P]
        lhs1 = [bf(cat0(jnp.where(m0, 0.0, at[g]), jnp.where(m0, 0.0, rt[g]))) for g in P]
        x0 = [_dot_nt(lhs0[g], cat0(btb[g], ktb[g])) for g in P]
        x1 = [_dot_nt(lhs1[g], cat0(ktb[g], btb[g])) for g in P]
        n_bd = [cat0(jnp.where(left & strict, x0[g][:C], 0.0),
                     jnp.where((~left) & strict, x1[g][:C], 0.0)) for g in P]
        ak = [bf(cat0(jnp.where((~left) & strict, x0[g][:C], 0.0),
                      jnp.where(left & strict, x1[g][:C], 0.0))) for g in P]
        mm0 = [bf(jnp.where(incl, x0[g][C:], 0.0)) for g in P]
        mm1 = [bf(jnp.where(incl, x1[g][C:], 0.0)) for g in P]
        t = [eye + n_bd[g] for g in P]
        pw = [bf(n_bd[g]) for g in P]
        pw = [bf(_dot(pw[g], pw[g])) for g in P]
        for _ in range(4):
            res = [_dot(pw[g], cat1(pw[g], bf(t[g]))) for g in P]
            t = [t[g] + res[g][:, LANES:] for g in P]
            pw = [bf(res[g][:, :LANES]) for g in P]
        t = [t[g] + _dot(pw[g], bf(t[g])) for g in P]
        av = [_dot(ak[g], cat0(vb[g], vb[g])) for g in P]
        av = [cat0(jnp.where(m0, av[g][:C], 0.0), jnp.where(m0, 0.0, av[g][C:])) for g in P]
        wu = [_dot(bf(t[g]), cat1(cat0(lhs0[g][:C], lhs1[g][:C]), bf(av[g]))) for g in P]
        w = [bf(wu[g][:C, :LANES] + wu[g][C:, :LANES]) for g in P]
        u0 = [wu[g][:C, LANES:] + wu[g][C:, LANES:] for g in P]
        s = [s_ref[g] for g in P]
        sb = [bf(s[g]) for g in P]
        ub = [bf(_dot_nt(w[g], sb[g]) + u0[g]) for g in P]
        uv = [cat0(ub[g], vb[g]) for g in P]
        y = [_dot_nt(rtb[g], sb[g])
             + jnp.where(m0, _dot(mm0[g], uv[g]), _dot(mm1[g], cat0(vb[g], ub[g]))) for g in P]
        upd = [_dot_tn(uv[g], bk[g]) for g in P]
        for g in P:
            s_ref[g] = s[g] * jnp.exp(last[g]) + jnp.where(blockdiag, upd[g], 0.0)
            y_s[rows, csl[g]] = y[g]
        return 0

    lax.fori_loop(0, tt // C, chunk, 0)

    for g in range(n_pairs):
        cs = slice(g * LANES, (g + 1) * LANES)
        y = y_s[:, cs]
        mean = head_sum(y) * (1.0 / 64.0)
        yc = y - mean
        var = head_sum(yc * yc) * (1.0 / 64.0)
        yn = yc * lax.rsqrt(var + GN_EPS) * lg_ref[:, cs] + lb_ref[:, cs]
        y_ref[:, cs] = ((yn + bo_s[:, cs]) * g_s[:, cs]).astype(BF16)


def _wkv(p_rwkv, prm, batch, seq, n_pairs, tt):
    m = p_rwkv.shape[0]
    c = prm["w0"].shape[1]
    gw = n_pairs * LANES
    n_col_blocks = c // gw
    nt = seq // tt
    low_w = prm["mu_low"].shape[1]

    def pspec(off):
        return pl.BlockSpec((tt, gw), lambda b, g, t: (b * nt + t, off * n_col_blocks + g))

    def vspec():
        return pl.BlockSpec((1, gw), lambda b, g, t: (0, g))

    def wspec(rows):
        return pl.BlockSpec((rows, gw), lambda b, g, t: (0, g))

    kern = functools.partial(_wkv_kernel, n_pairs=n_pairs, tt=tt)
    tile = pltpu.VMEM((tt, gw), F32)
    return pl.pallas_call(
        kern,
        grid=(batch, n_col_blocks, nt),
        in_specs=[
            pspec(0), pspec(1), pspec(2),
            pl.BlockSpec((tt, low_w), lambda b, g, t: (b * nt + t, (3 * c) // low_w)),
            vspec(), vspec(), vspec(),
            pl.BlockSpec((1, low_w), lambda b, g, t: (0, 0)),
            vspec(), vspec(), vspec(), vspec(), vspec(), vspec(), vspec(),
            wspec(LANES), wspec(LANES), wspec(low_w - LANES),
        ],
        out_specs=pl.BlockSpec((tt, gw), lambda b, g, t: (b * nt + t, g)),
        out_shape=jax.ShapeDtypeStruct((m, c), BF16),
        scratch_shapes=[
            pltpu.VMEM((n_pairs, LANES, LANES), F32),
            pltpu.VMEM((1, gw), F32), pltpu.VMEM((1, gw), F32), pltpu.VMEM((1, gw), F32),
            pltpu.VMEM((1, low_w), F32),
            tile, tile, tile, tile, tile, tile, tile, tile, tile,
        ],
        compiler_params=_cparams(("parallel", "parallel", "arbitrary")),
        name="wkv7",
    )(p_rwkv, p_rwkv, p_rwkv, p_rwkv,
      prm["mu_r"], prm["mu_k"], prm["mu_v"], prm["mu_low"],
      prm["w0"], prm["a0"], prm["k_k"], prm["k_a"], prm["r_k"], prm["lnx_g"], prm["lnx_b"],
      prm["wd"], prm["wa"], prm["wg"])


def _sb_kernel(q_ref, k_ref, v_ref, o_ref, acc_ref, right_ref, *, tq, scale):
    i = pl.program_id(1)
    row = lax.broadcasted_iota(jnp.int32, (tq, tq), 0)
    col = lax.broadcasted_iota(jnp.int32, (tq, tq), 1)
    r2 = lax.broadcasted_iota(jnp.int32, (tq, 2 * tq), 0)
    c2 = lax.broadcasted_iota(jnp.int32, (tq, 2 * tq), 1)
    after_and_total = jnp.where((c2 >= tq) | (r2 > c2), 1.0, 0.0).astype(BF16)
    diag = col < row

    heads = range(SB_HEADS)
    hsl = [slice(h * LANES, (h + 1) * LANES) for h in heads]

    def block(j, first):
        ks = pl.ds(pl.multiple_of(j * tq, tq), tq)
        z = [_dot_nt(q_ref[:, hsl[h]], k_ref[ks, hsl[h]]) * scale for h in heads]
        sp = [jnp.maximum(z[h], 0.0) + jnp.log(1.0 + jnp.exp(-jnp.abs(z[h]))) for h in heads]
        log_keep = [jnp.where(diag, -sp[h], 0.0) if first else -sp[h] for h in heads]
        hi = [log_keep[h].astype(BF16) for h in heads]
        lo = [(log_keep[h] - hi[h].astype(F32)).astype(BF16) for h in heads]
        sums = [_dot(hi[h], after_and_total) + _dot(lo[h], after_and_total) for h in heads]
        after = [sums[h][:, :tq] if first else sums[h][:, :tq] + right_ref[h] for h in heads]
        attn = [jnp.exp(z[h] - sp[h] + after[h]) for h in heads]
        if first:
            attn = [jnp.where(diag, attn[h], 0.0) for h in heads]
        pv = [_dot(attn[h].astype(BF16), v_ref[ks, hsl[h]]) for h in heads]
        for h in heads:
            if first:
                acc_ref[h] = pv[h]
                right_ref[h] = sums[h][:, tq:]
            else:
                acc_ref[h] += pv[h]
                right_ref[h] += sums[h][:, tq:]

    block(i, True)

    def body(jj, _):
        block(i - 1 - jj, False)
        return 0

    lax.fori_loop(0, i, body, 0)
    for h in heads:
        o_ref[:, hsl[h]] = acc_ref[h].astype(BF16)


def _sb_attn(p_attn, batch, seq, tq):
    m = p_attn.shape[0]
    w = SB_HEADS * LANES
    nq = seq // tq
    kern = functools.partial(_sb_kernel, tq=tq, scale=LANES ** -0.5)
    return pl.pallas_call(
        kern,
        grid=(batch, nq),
        in_specs=[
            pl.BlockSpec((tq, w), lambda b, i: (b * nq + i, 0)),
            pl.BlockSpec((seq, w), lambda b, i: (b, 1)),
            pl.BlockSpec((seq, w), lambda b, i: (b, 2)),
        ],
        out_specs=pl.BlockSpec((tq, w), lambda b, i: (b * nq + i, 0)),
        out_shape=jax.ShapeDtypeStruct((m, w), BF16),
        scratch_shapes=[pltpu.VMEM((SB_HEADS, tq, tq), F32), pltpu.VMEM((SB_HEADS, tq, tq), F32)],
        compiler_params=_cparams(("parallel", "arbitrary")),
        name="sb_attn",
    )(p_attn, p_attn, p_attn)


def _mem_kv_kernel(m_ref, w_ref, o_ref):
    o_ref[...] = _dot(m_ref[...].astype(BF16), w_ref[...]).astype(BF16)


def _mem_kv(mem2, w_bf16, tm):
    m, d = mem2.shape
    n = w_bf16.shape[1]
    return pl.pallas_call(
        _mem_kv_kernel,
        grid=(m // tm,),
        in_specs=[pl.BlockSpec((tm, d), lambda i: (i, 0)),
                  pl.BlockSpec((d, n), lambda i: (0, 0))],
        out_specs=pl.BlockSpec((tm, n), lambda i: (i, 0)),
        out_shape=jax.ShapeDtypeStruct((m, n), BF16),
        compiler_params=_cparams(("parallel",)),
        name="mem_kv",
    )(mem2, w_bf16)


def _mem_attn_kernel(q_ref, k_ref, v_ref, o_ref, *, scale):
    s = _dot_nt(q_ref[...], k_ref[...]) * scale
    s = s - jnp.max(s, axis=-1, keepdims=True)
    e = jnp.exp(s)
    p = e / jnp.sum(e, axis=-1, keepdims=True)
    o_ref[...] = _dot(p.astype(BF16), v_ref[...]).astype(BF16)


def _mem_attn(p_attn, kv, batch, seq, mem_len, tq):
    m = p_attn.shape[0]
    hd = LANES
    nq = seq // tq
    q_off = 3 * SB_HEADS
    kern = functools.partial(_mem_attn_kernel, scale=hd ** -0.5)
    return pl.pallas_call(
        kern,
        grid=(batch, MEM_HEADS, nq),
        in_specs=[
            pl.BlockSpec((tq, hd), lambda b, h, i: (b * nq + i, q_off + h)),
            pl.BlockSpec((mem_len, hd), lambda b, h, i: (b, h)),
            pl.BlockSpec((mem_len, hd), lambda b, h, i: (b, MEM_HEADS + h)),
        ],
        out_specs=pl.BlockSpec((tq, hd), lambda b, h, i: (b * nq + i, h)),
        out_shape=jax.ShapeDtypeStruct((m, MEM_HEADS * hd), BF16),
        compiler_params=_cparams(("parallel", "parallel", "parallel")),
        name="mem_attn",
    )(p_attn, kv, kv)


def _outproj_kernel(yr_ref, ys_ref, ym_ref, h_ref, wr_ref, ws_ref, wm_ref, g_ref, b_ref,
                    rw_ref, rb_ref, h1_ref, h1f_ref, lg_ref):
    mix = _dot(yr_ref[...], wr_ref[...]) + _dot(ys_ref[...], ws_ref[...]) + _dot(ym_ref[...], wm_ref[...])
    h1 = _layer_norm(DEEPNORM_ALPHA * h_ref[...] + mix, g_ref[...], b_ref[...])
    h1_ref[...] = h1
    _store_rowmajor(h1f_ref, h1)
    lg_ref[...] = _dot_hp(h1, rw_ref[...]) + rb_ref[...]


def _store_rowmajor(flat_ref, x):
    rows, width = x.shape
    s_per_row = width // LANES
    for s in range(s_per_row):
        flat_ref[pl.ds(s, rows, stride=s_per_row), :] = x[:, s * LANES:(s + 1) * LANES]


def _load_rowmajor(flat_ref, rows, s_per_row, s):
    return flat_ref[pl.ds(s, rows, stride=s_per_row), :]


def _outproj(y_r, y_s, y_m, h, w_r, w_s, w_m, g, b, r_w, r_b, tm):
    m, d = h.shape
    spr = d // LANES
    full = lambda a: pl.BlockSpec(a.shape, lambda i: (0, 0))
    rows = lambda a: pl.BlockSpec((tm, a.shape[1]), lambda i: (i, 0))
    return pl.pallas_call(
        _outproj_kernel,
        grid=(m // tm,),
        in_specs=[rows(y_r), rows(y_s), rows(y_m), rows(h), full(w_r), full(w_s), full(w_m),
                  full(g), full(b), full(r_w), full(r_b)],
        out_specs=[pl.BlockSpec((tm, d), lambda i: (i, 0)),
                   pl.BlockSpec((tm * spr, LANES), lambda i: (i, 0)),
                   pl.BlockSpec((tm, LANES), lambda i: (i, 0))],
        out_shape=[jax.ShapeDtypeStruct((m, d), F32),
                   jax.ShapeDtypeStruct((m * spr, LANES), F32),
                   jax.ShapeDtypeStruct((m, LANES), F32)],
        compiler_params=_cparams(("parallel",)),
        name="outproj",
    )(y_r, y_s, y_m, h, w_r, w_s, w_m, g, b, r_w, r_b)


def _route_kernel(lg_ref, id_ref, wt_ref, *, n_groups, per_group):
    lg = lg_ref[...]
    lane_i = lax.broadcasted_iota(jnp.int32, lg.shape, 1)
    lane = lane_i.astype(F32)
    neg = jnp.float32(-jnp.inf)
    big = jnp.float32(2 ** 20)

    def first_max(vals):
        mx = jnp.max(vals, axis=-1, keepdims=True)
        idx = jnp.min(jnp.where(vals == mx, lane, big), axis=-1, keepdims=True)
        return mx, idx

    is_group = lane < n_groups
    gmax, gidx = first_max(jnp.where(is_group, lg, neg))
    gsum = jnp.sum(jnp.where(is_group, jnp.exp(lg - gmax), 0.0), axis=-1, keepdims=True)
    group_w = 1.0 / gsum
    lo = n_groups + gidx * per_group
    in_group = (lane >= lo) & (lane < lo + per_group)
    v1, i1 = first_max(jnp.where(in_group, lg, neg))
    v2, i2 = first_max(jnp.where(in_group & (lane != i1), lg, neg))
    e2 = jnp.exp(v2 - v1)
    w1 = group_w / (1.0 + e2)
    w2 = group_w * e2 / (1.0 + e2)
    e1 = (i1 - n_groups).astype(jnp.int32)
    e2i = (i2 - n_groups).astype(jnp.int32)
    id_ref[...] = jnp.where(lane_i == 0, e1, jnp.where(lane_i == 1, e2i, 0))
    wt_ref[...] = jnp.where(lane_i == 0, w1, jnp.where(lane_i == 1, w2, 0.0))


def _route(logits, n_groups, per_group, tm):
    m = logits.shape[0]
    kern = functools.partial(_route_kernel, n_groups=n_groups, per_group=per_group)
    spec = pl.BlockSpec((tm, LANES), lambda i: (i, 0))
    return pl.pallas_call(
        kern,
        grid=(m // tm,),
        in_specs=[spec],
        out_specs=[spec, spec],
        out_shape=[jax.ShapeDtypeStruct((m, LANES), jnp.int32),
                   jax.ShapeDtypeStruct((m, LANES), F32)],
        compiler_params=_cparams(("parallel",)),
        name="route",
    )(logits)


def _row_copy(src_hbm, dst_vmem, sem, src_row, dst_row, spr):
    return pltpu.make_async_copy(src_hbm.at[pl.ds(src_row * spr, spr)],
                                 dst_vmem.at[pl.ds(dst_row * spr, spr)], sem)


def _ffn_kernel(tok_ref, be_ref, nv_ref, nu_ref, h_hbm, wg_ref, wu_ref, wd_ref, o_ref,
                xf_ref, xb_ref, acc_ref, sem, *, rows, sub, spr, nj):
    i = pl.program_id(0)
    j = pl.program_id(1)
    n_used = nu_ref[0]
    used = i < n_used
    slot = i % 2

    def issue(block, lo, hi, buf):
        base = block * rows

        def body(r, _):
            _row_copy(h_hbm, xf_ref.at[buf], sem.at[buf], tok_ref[base + r], r, spr).start()
            return 0

        lax.fori_loop(lo, hi, body, 0)

    def drain(n, buf):
        def body(r, _):
            _row_copy(h_hbm, xf_ref.at[buf], sem.at[buf], 0, r, spr).wait()
            return 0

        lax.fori_loop(0, n, body, 0)

    @pl.when((i == 0) & (j == 0))
    def _():
        xf_ref[...] = jnp.zeros_like(xf_ref)
        issue(0, 0, nv_ref[0], 0)

    @pl.when(used & (j == 0))
    def _():
        drain(nv_ref[i], slot)
        for s in range(spr):
            xb_ref[:, s * LANES:(s + 1) * LANES] = _load_rowmajor(xf_ref.at[slot], rows, spr, s).astype(BF16)
        acc_ref[...] = jnp.zeros_like(acc_ref)

    @pl.when(i + 1 < n_used)
    def _():
        nxt = nv_ref[i + 1]
        q = rows // nj
        issue(i + 1, jnp.minimum(j * q, nxt), jnp.minimum((j + 1) * q, nxt), 1 - slot)

    n_sub = (nv_ref[i] + sub - 1) // sub
    for n in range(1, rows // sub + 1):
        @pl.when(used & (n_sub == n))
        def _(n=n):
            r = n * sub
            xb = xb_ref[0:r, :]
            gate = _dot(xb, wg_ref[...].astype(BF16))
            up = _dot(xb, wu_ref[...].astype(BF16))
            hid = (gate * jax.nn.sigmoid(gate)) * up
            acc_ref[0:r, :] += _dot(hid.astype(BF16), wd_ref[...].astype(BF16))

    @pl.when(j == nj - 1)
    def _():
        _store_rowmajor(o_ref, acc_ref[...])


def _moe_ffn(slot_tok, block_expert, n_valid, n_used, h1_flat, w_gate, w_up, w_down, n_blocks, rows, tf):
    d = w_gate.shape[1]
    de = w_gate.shape[2]
    spr = d // LANES
    nj = de // tf
    kern = functools.partial(_ffn_kernel, rows=rows, sub=MOE_SUB, spr=spr, nj=nj)

    def jidx(i, j, nu):
        return jnp.where(i < nu[0], j, nj - 1)

    return pl.pallas_call(
        kern,
        grid_spec=pltpu.PrefetchScalarGridSpec(
            num_scalar_prefetch=4,
            grid=(n_blocks, nj),
            in_specs=[
                pl.BlockSpec(memory_space=pl.ANY),
                pl.BlockSpec((None, d, tf), lambda i, j, tok, be, nv, nu: (be[i], 0, jidx(i, j, nu))),
                pl.BlockSpec((None, d, tf), lambda i, j, tok, be, nv, nu: (be[i], 0, jidx(i, j, nu))),
                pl.BlockSpec((None, tf, d), lambda i, j, tok, be, nv, nu: (be[i], jidx(i, j, nu), 0)),
            ],
            out_specs=pl.BlockSpec((rows * spr, LANES), lambda i, j, tok, be, nv, nu: (i, 0)),
            scratch_shapes=[pltpu.VMEM((2, rows * spr, LANES), F32),
                            pltpu.VMEM((rows, d), BF16),
                            pltpu.VMEM((rows, d), F32),
                            pltpu.SemaphoreType.DMA((2,))],
        ),
        out_shape=jax.ShapeDtypeStruct((n_blocks * rows * spr, LANES), F32),
        compiler_params=_cparams(("arbitrary", "arbitrary")),
        name="moe_ffn",
    )(slot_tok, block_expert, n_valid, n_used, h1_flat, w_gate, w_up, w_down)


def _combine_kernel(pos_ref, h_ref, wt_ref, g_ref, b_ref, y_hbm, o_ref, buf_ref, z_ref, sem, *, tm, spr):
    base = pl.program_id(0) * tm

    def issue(t, _):
        _row_copy(y_hbm, buf_ref.at[0], sem, pos_ref[2 * (base + t)], t, spr).start()
        _row_copy(y_hbm, buf_ref.at[1], sem, pos_ref[2 * (base + t) + 1], t, spr).start()
        return 0

    lax.fori_loop(0, tm, issue, 0)

    def drain(t, _):
        _row_copy(y_hbm, buf_ref.at[0], sem, 0, t, spr).wait()
        _row_copy(y_hbm, buf_ref.at[1], sem, 0, t, spr).wait()
        return 0

    lax.fori_loop(0, tm, drain, 0)
    wt = wt_ref[...]
    w0, w1 = wt[:, 0:1], wt[:, 1:2]
    for s in range(spr):
        cs = slice(s * LANES, (s + 1) * LANES)
        ffn = (_load_rowmajor(buf_ref.at[0], tm, spr, s) * w0
               + _load_rowmajor(buf_ref.at[1], tm, spr, s) * w1)
        z_ref[:, cs] = DEEPNORM_ALPHA * h_ref[:, cs] + ffn
    o_ref[...] = _layer_norm(z_ref[...], g_ref[...], b_ref[...])


def _combine(pos, h1, wts, g, b, ys_flat, tm):
    m, d = h1.shape
    spr = d // LANES
    kern = functools.partial(_combine_kernel, tm=tm, spr=spr)
    return pl.pallas_call(
        kern,
        grid_spec=pltpu.PrefetchScalarGridSpec(
            num_scalar_prefetch=1,
            grid=(m // tm,),
            in_specs=[
                pl.BlockSpec((tm, d), lambda i, pos: (i, 0)),
                pl.BlockSpec((tm, LANES), lambda i, pos: (i, 0)),
                pl.BlockSpec((1, d), lambda i, pos: (0, 0)),
                pl.BlockSpec((1, d), lambda i, pos: (0, 0)),
                pl.BlockSpec(memory_space=pl.ANY),
            ],
            out_specs=pl.BlockSpec((tm, d), lambda i, pos: (i, 0)),
            scratch_shapes=[pltpu.VMEM((2, tm * spr, LANES), F32), pltpu.VMEM((tm, d), F32),
                            pltpu.SemaphoreType.DMA],
        ),
        out_shape=jax.ShapeDtypeStruct((m, d), F32),
        compiler_params=_cparams(("arbitrary",)),
        name="moe_combine",
    )(pos, h1, wts, g, b, ys_flat)


def _dispatch_plan(expert_ids, n_experts, rows, sub):
    n_tok = expert_ids.shape[0]
    n_assign = n_tok * 2
    n_blocks = -(-n_assign // rows) + n_experts
    flat_e = expert_ids.reshape(-1)
    counts = jnp.sum(flat_e[:, None] == jnp.arange(n_experts, dtype=jnp.int32)[None, :], axis=0,
                     dtype=jnp.int32)
    padded = (counts + rows - 1) // rows * rows
    pad_end = jnp.cumsum(padded)
    pad_start = pad_end - padded
    start = jnp.cumsum(counts) - counts
    order = jnp.argsort(flat_e, stable=True).astype(jnp.int32)
    sorted_e = flat_e[order]
    dest = pad_start[sorted_e] + jnp.arange(n_assign, dtype=jnp.int32) - start[sorted_e]
    slot_tok = jnp.zeros((n_blocks * rows,), jnp.int32).at[dest].set(order // 2)
    pos = jnp.zeros((n_assign,), jnp.int32).at[order].set(dest)
    n_used = (pad_end[-1] // rows).astype(jnp.int32)
    blk = jnp.arange(n_blocks, dtype=jnp.int32)
    blk_c = jnp.minimum(blk, n_used - 1)
    block_expert = jnp.minimum(jnp.searchsorted(pad_end, blk_c * rows, side="right"),
                               n_experts - 1).astype(jnp.int32)
    valid = jnp.clip(counts[block_expert] - (blk_c * rows - pad_start[block_expert]), 0, rows)
    n_valid = jnp.where(blk < n_used, valid, 0).astype(jnp.int32)
    return slot_tok, pos, n_used.reshape(1), block_expert, n_valid, n_blocks


def _pick(n, pref):
    t = min(pref, n)
    while n % t:
        t //= 2
    return t


def kernel(x, mem, ln_in_g, ln_in_b, w_in, tshift_mu, w0, w_decay_up, a0, w_a_up, w_g_up, k_k, k_a, r_k,
           lnx_g, lnx_b, w_mem_kv, w_out, ln1_g, ln1_b, router_group, router_group_b, router_expert,
           router_expert_b, w_e_gate, w_e_up, w_e_down, ln2_g, ln2_b):
    batch, seq, d = x.shape
    mem_len = mem.shape[1]
    m = batch * seq
    c = w0.shape[1]
    dr, ar, gr = w_decay_up.shape[1], w_a_up.shape[1], w_g_up.shape[1]
    rwkv_cols = 3 * c + dr + ar + gr
    sb_w = SB_HEADS * LANES
    mem_w = MEM_HEADS * LANES
    assert dr + ar == LANES and c % (2 * LANES) == 0 and w_in.shape[0] == DEPTH
    assert w_in.shape[2] == rwkv_cols + 3 * sb_w + mem_w
    n_experts = router_expert.shape[2]
    row = lambda a: a.reshape(1, -1)

    tn = 512
    low_w = -(-(dr + ar + gr) // LANES) * LANES
    rw_pad = -(-(3 * c + low_w) // tn) * tn
    wi = w_in[0]
    zpad = lambda n: jnp.zeros((d, n), F32)
    w_packed = jnp.concatenate(
        [wi[:, :rwkv_cols], zpad(rw_pad - rwkv_cols), wi[:, rwkv_cols:]], axis=1).astype(BF16)
    mu = tshift_mu[0]
    lp = low_w - (dr + ar + gr)
    prm = dict(
        mu_r=row(mu[:c]), mu_k=row(mu[c:2 * c]), mu_v=row(mu[2 * c:3 * c]),
        mu_low=row(jnp.concatenate([mu[3 * c:rwkv_cols], jnp.zeros((lp,), F32)])),
        w0=row(w0[0]), a0=row(a0[0]), k_k=row(k_k[0]), k_a=row(k_a[0]), r_k=row(r_k[0]),
        lnx_g=row(lnx_g[0]), lnx_b=row(lnx_b[0]),
        wd=jnp.concatenate([w_decay_up[0], jnp.zeros((ar, c), F32)], axis=0).astype(BF16),
        wa=jnp.concatenate([jnp.zeros((dr, c), F32), w_a_up[0]], axis=0).astype(BF16),
        wg=jnp.concatenate([w_g_up[0], jnp.zeros((lp, c), F32)], axis=0).astype(BF16),
    )
    assert (3 * c) % low_w == 0

    x2 = x.reshape(m, d)
    tm = _pick(m, 512)
    h, p_rwkv, p_attn = _ln_inproj(x2, row(ln_in_g), row(ln_in_b), w_packed, rw_pad, tm, tn)

    y_rwkv = _wkv(p_rwkv, prm, batch, seq, n_pairs=4, tt=_pick(seq, 512))
    y_sb = _sb_attn(p_attn, batch, seq, tq=LANES)
    kv = _mem_kv(mem.reshape(batch * mem_len, d), w_mem_kv[0].astype(BF16), _pick(batch * mem_len, 256))
    y_mem = _mem_attn(p_attn, kv, batch, seq, mem_len, tq=_pick(seq, 512))

    wo = w_out[0].astype(BF16)
    r_w = jnp.concatenate([router_group[0], router_expert[0],
                           jnp.zeros((d, LANES - N_GROUPS - n_experts), F32)], axis=1)
    r_b = jnp.concatenate([router_group_b[0], router_expert_b[0],
                           jnp.zeros((LANES - N_GROUPS - n_experts,), F32)]).reshape(1, LANES)
    h1, h1_flat, logits = _outproj(y_rwkv, y_sb, y_mem, h, wo[:c], wo[c:c + sb_w], wo[c + sb_w:],
                                   row(ln1_g[0]), row(ln1_b[0]), r_w, r_b, _pick(m, 256))

    ids, wts = _route(logits, N_GROUPS, n_experts // N_GROUPS, _pick(m, 512))
    slot_tok, pos, n_used, block_expert, n_valid, n_blocks = _dispatch_plan(
        ids[:, :2], n_experts, MOE_ROWS, MOE_SUB)
    ys_flat = _moe_ffn(slot_tok, block_expert, n_valid, n_used, h1_flat, w_e_gate[0], w_e_up[0], w_e_down[0],
                       n_blocks, MOE_ROWS, tf=256)
    out = _combine(pos, h1, wts, row(ln2_g[0]), row(ln2_b[0]), ys_flat, _pick(m, 256))
    return out.reshape(batch, seq, d)
```

```python
import functools

import jax
import jax.numpy as jnp
from jax import lax
from jax.experimental import pallas as pl
from jax.experimental.pallas import tpu as pltpu

F32 = jnp.float32
BF16 = jnp.bfloat16

SB_HEADS = 4
MEM_HEADS = 4
N_GROUPS = 8
DEPTH = 1
DEEPNORM_ALPHA = (2.0 * DEPTH) ** 0.25
LN_EPS = 1e-5
GN_EPS = 64e-5

LANES = 128
WKV_CHUNK = 64
MOE_ROWS = 512
MOE_SUB = 128
VMEM_LIMIT = 56 * 1024 * 1024


def _cparams(sem):
    return pltpu.CompilerParams(dimension_semantics=sem, vmem_limit_bytes=VMEM_LIMIT)


def _layer_norm(x, g, b):
    mu = jnp.mean(x, axis=-1, keepdims=True)
    xc = x - mu
    var = jnp.mean(xc * xc, axis=-1, keepdims=True)
    return xc * lax.rsqrt(var + LN_EPS) * g + b


def _split3(x):
    hi = x.astype(BF16)
    r1 = x - hi.astype(F32)
    mid = r1.astype(BF16)
    lo = (r1 - mid.astype(F32)).astype(BF16)
    return hi, mid, lo


def _dot(a, b):
    return jnp.dot(a, b, preferred_element_type=F32)


def _dot_nt(a, b):
    return lax.dot_general(a, b, (((1,), (1,)), ((), ())), preferred_element_type=F32)


def _dot_tn(a, b):
    return lax.dot_general(a, b, (((0,), (0,)), ((), ())), preferred_element_type=F32)


def _dot_f32_by_exact(x, m):
    hi, mid, lo = _split3(x)
    return _dot(hi, m) + _dot(mid, m) + _dot(lo, m)


def _dot_exact_by_f32(m, x):
    hi, mid, lo = _split3(x)
    return _dot(m, hi) + _dot(m, mid) + _dot(m, lo)


def _dot_hp(a, b):
    return jnp.dot(a, b, preferred_element_type=F32, precision=lax.Precision.HIGHEST)


def _ln_kernel(x_ref, g_ref, b_ref, h_ref, hb_ref):
    h = _layer_norm(x_ref[...], g_ref[...], b_ref[...])
    h_ref[...] = h
    hb_ref[...] = h.astype(BF16)


def _ln_in(x2, g, b, tm):
    m, d = x2.shape
    rows = pl.BlockSpec((tm, d), lambda i: (i, 0))
    vec = pl.BlockSpec((1, d), lambda i: (0, 0))
    return pl.pallas_call(
        _ln_kernel,
        grid=(m // tm,),
        in_specs=[rows, vec, vec],
        out_specs=[rows, rows],
        out_shape=[jax.ShapeDtypeStruct((m, d), F32), jax.ShapeDtypeStruct((m, d), BF16)],
        compiler_params=_cparams(("parallel",)),
        name="ln_in",
    )(x2, g, b)


def _inproj_kernel(hb_ref, w_ref, pr_ref, pa_ref, *, n_f32_tiles):
    n = pl.program_id(1)
    p = _dot(hb_ref[...], w_ref[...])

    @pl.when(n < n_f32_tiles)
    def _():
        pr_ref[...] = p

    @pl.when(n >= n_f32_tiles)
    def _():
        pa_ref[...] = p.astype(BF16)


def _inproj(hb, w_packed, n_rwkv_cols, tm, tn):
    m, d = hb.shape
    n_total = w_packed.shape[1]
    n_attn_cols = n_total - n_rwkv_cols
    nf = n_rwkv_cols // tn
    kern = functools.partial(_inproj_kernel, n_f32_tiles=nf)
    return pl.pallas_call(
        kern,
        grid=(m // tm, n_total // tn),
        in_specs=[
            pl.BlockSpec((tm, d), lambda i, n: (i, 0)),
            pl.BlockSpec((d, tn), lambda i, n: (0, n)),
        ],
        out_specs=[
            pl.BlockSpec((tm, tn), lambda i, n: (i, jnp.minimum(n, nf - 1))),
            pl.BlockSpec((tm, tn), lambda i, n: (i, jnp.maximum(n - nf, 0))),
        ],
        out_shape=[
            jax.ShapeDtypeStruct((m, n_rwkv_cols), F32),
            jax.ShapeDtypeStruct((m, n_attn_cols), BF16),
        ],
        compiler_params=_cparams(("parallel", "arbitrary")),
        name="inproj",
    )(hb, w_packed)


def _wkv_kernel(pr_ref, pk_ref, pv_ref, pl_ref,
                mur_ref, muk_ref, muv_ref, mul_ref,
                w0_ref, a0_ref, kk_ref, ka_ref, rk_ref, lg_ref, lb_ref,
                wd_ref, wa_ref, wg_ref,
                y_ref,
                s_ref, cr_ref, ck_ref, cv_ref, cl_ref,
                r_s, lw_s, k_s, v_s, a_s, b_s, y_s, g_s, bo_s,
                *, n_pairs, tt):
    C = WKV_CHUNK
    t_idx = pl.program_id(2)

    @pl.when(t_idx == 0)
    def _():
        s_ref[...] = jnp.zeros_like(s_ref)
        cr_ref[...] = jnp.zeros_like(cr_ref)
        ck_ref[...] = jnp.zeros_like(ck_ref)
        cv_ref[...] = jnp.zeros_like(cv_ref)
        cl_ref[...] = jnp.zeros_like(cl_ref)

    row = lax.broadcasted_iota(jnp.int32, (tt, 1), 0)

    def shifted(p_ref, carry_ref, mu_ref):
        p = p_ref[...]
        prev = pltpu.roll(p, shift=1, axis=0)
        prev = jnp.where(row == 0, carry_ref[...], prev)
        carry_ref[...] = p[tt - 1:tt, :]
        return p + (prev - p) * mu_ref[...]

    lane = lax.broadcasted_iota(jnp.int32, (LANES, LANES), 1)
    sub = lax.broadcasted_iota(jnp.int32, (LANES, LANES), 0)
    head_ones = jnp.where((lane // 64) == (sub // 64), 1.0, 0.0).astype(BF16)

    def head_sum(x):
        hi = x.astype(BF16)
        lo = (x - hi.astype(F32)).astype(BF16)
        return _dot(hi, head_ones) + _dot(lo, head_ones)

    low = shifted(pl_ref, cl_ref, mul_ref)
    da = low[:, 0:LANES]
    th = jnp.tanh(da).astype(BF16)
    sg = jax.nn.sigmoid(low[:, LANES:]).astype(BF16)
    da = da.astype(BF16)
    r = shifted(pr_ref, cr_ref, mur_ref)
    k = shifted(pk_ref, ck_ref, muk_ref)
    v = shifted(pv_ref, cv_ref, muv_ref)
    for g in range(n_pairs):
        cs = slice(g * LANES, (g + 1) * LANES)
        rg, kg, vg = r[:, cs], k[:, cs], v[:, cs]
        pre = w0_ref[:, cs] + _dot(th, wd_ref[:, cs])
        w_log = -jax.nn.softplus(-pre) - 0.5
        lw = -jnp.exp(w_log)
        a = jax.nn.sigmoid(a0_ref[:, cs] + _dot(da, wa_ref[:, cs]))
        gate = _dot(sg, wg_ref[:, cs])
        kk = kg * kk_ref[:, cs]
        kk = kk / jnp.maximum(jnp.sqrt(head_sum(kk * kk)), 1e-12)
        k2 = kg * (1.0 + (a - 1.0) * ka_ref[:, cs])
        bonus = head_sum(rg * k2 * rk_ref[:, cs]) * vg
        r_s[:, cs] = rg
        lw_s[:, cs] = lw
        k_s[:, cs] = k2
        v_s[:, cs] = vg
        a_s[:, cs] = -kk
        b_s[:, cs] = kk * a
        g_s[:, cs] = gate
        bo_s[:, cs] = bonus

    ci = lax.broadcasted_iota(jnp.int32, (C, 2 * C), 0)
    cj = lax.broadcasted_iota(jnp.int32, (C, 2 * C), 1)
    left = cj < C
    strict = (cj % C) < ci
    incl = (cj % C) <= ci
    tri_incl = jnp.where(lax.broadcasted_iota(jnp.int32, (C, C), 1)
                         <= lax.broadcasted_iota(jnp.int32, (C, C), 0), 1.0, 0.0).astype(BF16)
    lane_c = lax.broadcasted_iota(jnp.int32, (C, LANES), 1)
    m0 = lane_c < 64
    eye = jnp.where(lane == sub, 1.0, 0.0).astype(F32)
    blockdiag = (lane // 64) == (sub // 64)

    csl = [slice(g * LANES, (g + 1) * LANES) for g in range(n_pairs)]
    P = range(n_pairs)
    cat0 = lambda *xs: jnp.concatenate(xs, axis=0)
    cat1 = lambda *xs: jnp.concatenate(xs, axis=1)
    bf = lambda x: x.astype(BF16)

    def chunk(c, _):
        rows = pl.ds(pl.multiple_of(c * C, C), C)
        ld = lambda ref: [ref[rows, csl[g]] for g in P]
        rc, lwc, kc, vc, ac, bc = ld(r_s), ld(lw_s), ld(k_s), ld(v_s), ld(a_s), ld(b_s)
        cum = [_dot_exact_by_f32(tri_incl, lwc[g]) for g in P]
        last = [cum[g][C - 1:C, :] for g in P]
        rt = [rc[g] * jnp.exp(cum[g]) for g in P]
        at = [ac[g] * jnp.exp(cum[g] - lwc[g]) for g in P]
        ginv = [jnp.exp(-cum[g]) for g in P]
        btb = [bf(bc[g] * ginv[g]) for g in P]
        ktb = [bf(kc[g] * ginv[g]) for g in P]
        ghat = [jnp.exp(last[g] - cum[g]) for g in P]
        bk = [cat0(bf(bc[g] * ghat[g]), bf(kc[g] * ghat[g])) for g in P]
        vb = [bf(vc[g]) for g in P]
        rtb = [bf(rt[g]) for g in P]
        lhs0 = [bf(cat0(jnp.where(m0, at[g], 0.0), jnp.where(m0, rt[g], 0.0))) for g in P]
        lhs1 = [bf(cat0(jnp.where(m0, 0.0, at[g]), jnp.where(m0, 0.0, rt[g]))) for g in P]
        x0 = [_dot_nt(lhs0[g], cat0(btb[g], ktb[g])) for g in P]
        x1 = [_dot_nt(lhs1[g], cat0(ktb[g], btb[g])) for g in P]
        n_bd = [cat0(jnp.where(left & strict, x0[g][:C], 0.0),
                     jnp.where((~left) & strict, x1[g][:C], 0.0)) for g in P]
        ak = [bf(cat0(jnp.where((~left) & strict, x0[g][:C], 0.0),
                      jnp.where(left & strict, x1[g][:C], 0.0))) for g in P]
        mm0 = [bf(jnp.where(incl, x0[g][C:], 0.0)) for g in P]
        mm1 = [bf(jnp.where(incl, x1[g][C:], 0.0)) for g in P]
        t = [eye + n_bd[g] for g in P]
        pw = [bf(n_bd[g]) for g in P]
        pw = [bf(_dot(pw[g], pw[g])) for g in P]
        for _ in range(4):
            res = [_dot(pw[g], cat1(pw[g], bf(t[g]))) for g in P]
            t = [t[g] + res[g][:, LANES:] for g in P]
            pw = [bf(res[g][:, :LANES]) for g in P]
        t = [t[g] + _dot(pw[g], bf(t[g])) for g in P]
        av = [_dot(ak[g], cat0(vb[g], vb[g])) for g in P]
        av = [cat0(jnp.where(m0, av[g][:C], 0.0), jnp.where(m0, 0.0, av[g][C:])) for g in P]
        wu = [_dot(bf(t[g]), cat1(cat0(lhs0[g][:C], lhs1[g][:C]), bf(av[g]))) for g in P]
        w = [bf(wu[g][:C, :LANES] + wu[g][C:, :LANES]) for g in P]
        u0 = [wu[g][:C, LANES:] + wu[g][C:, LANES:] for g in P]
        s = [s_ref[g] for g in P]
        sb = [bf(s[g]) for g in P]
        ub = [bf(_dot_nt(w[g], sb[g]) + u0[g]) for g in P]
        uv = [cat0(ub[g], vb[g]) for g in P]
        y = [_dot_nt(rtb[g], sb[g])
             + jnp.where(m0, _dot(mm0[g], uv[g]), _dot(mm1[g], cat0(vb[g], ub[g]))) for g in P]
        upd = [_dot_tn(uv[g], bk[g]) for g in P]
        for g in P:
            s_ref[g] = s[g] * jnp.exp(last[g]) + jnp.where(blockdiag, upd[g], 0.0)
            y_s[rows, csl[g]] = y[g]
        return 0

    lax.fori_loop(0, tt // C, chunk, 0)

    for g in range(n_pairs):
        cs = slice(g * LANES, (g + 1) * LANES)
        y = y_s[:, cs]
        mean = head_sum(y) * (1.0 / 64.0)
        yc = y - mean
        var = head_sum(yc * yc) * (1.0 / 64.0)
        yn = yc * lax.rsqrt(var + GN_EPS) * lg_ref[:, cs] + lb_ref[:, cs]
        y_ref[:, cs] = ((yn + bo_s[:, cs]) * g_s[:, cs]).astype(BF16)


def _wkv(p_rwkv, prm, batch, seq, n_pairs, tt):
    m = p_rwkv.shape[0]
    c = prm["w0"].shape[1]
    gw = n_pairs * LANES
    n_col_blocks = c // gw
    nt = seq // tt
    low_w = prm["mu_low"].shape[1]

    def pspec(off):
        return pl.BlockSpec((tt, gw), lambda b, g, t: (b * nt + t, off * n_col_blocks + g))

    def vspec():
        return pl.BlockSpec((1, gw), lambda b, g, t: (0, g))

    def wspec(rows):
        return pl.BlockSpec((rows, gw), lambda b, g, t: (0, g))

    kern = functools.partial(_wkv_kernel, n_pairs=n_pairs, tt=tt)
    tile = pltpu.VMEM((tt, gw), F32)
    return pl.pallas_call(
        kern,
        grid=(batch, n_col_blocks, nt),
        in_specs=[
            pspec(0), pspec(1), pspec(2),
            pl.BlockSpec((tt, low_w), lambda b, g, t: (b * nt + t, (3 * c) // low_w)),
            vspec(), vspec(), vspec(),
            pl.BlockSpec((1, low_w), lambda b, g, t: (0, 0)),
            vspec(), vspec(), vspec(), vspec(), vspec(), vspec(), vspec(),
            wspec(LANES), wspec(LANES), wspec(low_w - LANES),
        ],
        out_specs=pl.BlockSpec((tt, gw), lambda b, g, t: (b * nt + t, g)),
        out_shape=jax.ShapeDtypeStruct((m, c), BF16),
        scratch_shapes=[
            pltpu.VMEM((n_pairs, LANES, LANES), F32),
            pltpu.VMEM((1, gw), F32), pltpu.VMEM((1, gw), F32), pltpu.VMEM((1, gw), F32),
            pltpu.VMEM((1, low_w), F32),
            tile, tile, tile, tile, tile, tile, tile, tile, tile,
        ],
        compiler_params=_cparams(("parallel", "parallel", "arbitrary")),
        name="wkv7",
    )(p_rwkv, p_rwkv, p_rwkv, p_rwkv,
      prm["mu_r"], prm["mu_k"], prm["mu_v"], prm["mu_low"],
      prm["w0"], prm["a0"], prm["k_k"], prm["k_a"], prm["r_k"], prm["lnx_g"], prm["lnx_b"],
      prm["wd"], prm["wa"], prm["wg"])


def _sb_kernel(q_ref, k_ref, v_ref, o_ref, acc_ref, right_ref, *, tq, scale):
    i = pl.program_id(1)
    row = lax.broadcasted_iota(jnp.int32, (tq, tq), 0)
    col = lax.broadcasted_iota(jnp.int32, (tq, tq), 1)
    r2 = lax.broadcasted_iota(jnp.int32, (tq, 2 * tq), 0)
    c2 = lax.broadcasted_iota(jnp.int32, (tq, 2 * tq), 1)
    after_and_total = jnp.where((c2 >= tq) | (r2 > c2), 1.0, 0.0).astype(BF16)
    diag = col < row

    heads = range(SB_HEADS)
    hsl = [slice(h * LANES, (h + 1) * LANES) for h in heads]

    def block(j, first):
        ks = pl.ds(pl.multiple_of(j * tq, tq), tq)
        z = [_dot_nt(q_ref[:, hsl[h]], k_ref[ks, hsl[h]]) * scale for h in heads]
        sp = [jnp.maximum(z[h], 0.0) + jnp.log(1.0 + jnp.exp(-jnp.abs(z[h]))) for h in heads]
        log_keep = [jnp.where(diag, -sp[h], 0.0) if first else -sp[h] for h in heads]
        hi = [log_keep[h].astype(BF16) for h in heads]
        lo = [(log_keep[h] - hi[h].astype(F32)).astype(BF16) for h in heads]
        sums = [_dot(hi[h], after_and_total) + _dot(lo[h], after_and_total) for h in heads]
        after = [sums[h][:, :tq] if first else sums[h][:, :tq] + right_ref[h] for h in heads]
        attn = [jnp.exp(z[h] - sp[h] + after[h]) for h in heads]
        if first:
            attn = [jnp.where(diag, attn[h], 0.0) for h in heads]
        pv = [_dot(attn[h].astype(BF16), v_ref[ks, hsl[h]]) for h in heads]
        for h in heads:
            if first:
                acc_ref[h] = pv[h]
                right_ref[h] = sums[h][:, tq:]
            else:
                acc_ref[h] += pv[h]
                right_ref[h] += sums[h][:, tq:]

    block(i, True)

    def body(jj, _):
        block(i - 1 - jj, False)
        return 0

    lax.fori_loop(0, i, body, 0)
    for h in heads:
        o_ref[:, hsl[h]] = acc_ref[h].astype(BF16)


def _sb_attn(p_attn, batch, seq, tq):
    m = p_attn.shape[0]
    w = SB_HEADS * LANES
    nq = seq // tq
    kern = functools.partial(_sb_kernel, tq=tq, scale=LANES ** -0.5)
    return pl.pallas_call(
        kern,
        grid=(batch, nq),
        in_specs=[
            pl.BlockSpec((tq, w), lambda b, i: (b * nq + i, 0)),
            pl.BlockSpec((seq, w), lambda b, i: (b, 1)),
            pl.BlockSpec((seq, w), lambda b, i: (b, 2)),
        ],
        out_specs=pl.BlockSpec((tq, w), lambda b, i: (b * nq + i, 0)),
        out_shape=jax.ShapeDtypeStruct((m, w), BF16),
        scratch_shapes=[pltpu.VMEM((SB_HEADS, tq, tq), F32), pltpu.VMEM((SB_HEADS, tq, tq), F32)],
        compiler_params=_cparams(("parallel", "arbitrary")),
        name="sb_attn",
    )(p_attn, p_attn, p_attn)


def _mem_kv_kernel(m_ref, w_ref, o_ref):
    o_ref[...] = _dot(m_ref[...].astype(BF16), w_ref[...]).astype(BF16)


def _mem_kv(mem2, w_bf16, tm):
    m, d = mem2.shape
    n = w_bf16.shape[1]
    return pl.pallas_call(
        _mem_kv_kernel,
        grid=(m // tm,),
        in_specs=[pl.BlockSpec((tm, d), lambda i: (i, 0)),
                  pl.BlockSpec((d, n), lambda i: (0, 0))],
        out_specs=pl.BlockSpec((tm, n), lambda i: (i, 0)),
        out_shape=jax.ShapeDtypeStruct((m, n), BF16),
        compiler_params=_cparams(("parallel",)),
        name="mem_kv",
    )(mem2, w_bf16)


def _mem_attn_kernel(q_ref, k_ref, v_ref, o_ref, *, scale):
    s = _dot_nt(q_ref[...], k_ref[...]) * scale
    s = s - jnp.max(s, axis=-1, keepdims=True)
    e = jnp.exp(s)
    p = e / jnp.sum(e, axis=-1, keepdims=True)
    o_ref[...] = _dot(p.astype(BF16), v_ref[...]).astype(BF16)


def _mem_attn(p_attn, kv, batch, seq, mem_len, tq):
    m = p_attn.shape[0]
    hd = LANES
    nq = seq // tq
    q_off = 3 * SB_HEADS
    kern = functools.partial(_mem_attn_kernel, scale=hd ** -0.5)
    return pl.pallas_call(
        kern,
        grid=(batch, MEM_HEADS, nq),
        in_specs=[
            pl.BlockSpec((tq, hd), lambda b, h, i: (b * nq + i, q_off + h)),
            pl.BlockSpec((mem_len, hd), lambda b, h, i: (b, h)),
            pl.BlockSpec((mem_len, hd), lambda b, h, i: (b, MEM_HEADS + h)),
        ],
        out_specs=pl.BlockSpec((tq, hd), lambda b, h, i: (b * nq + i, h)),
        out_shape=jax.ShapeDtypeStruct((m, MEM_HEADS * hd), BF16),
        compiler_params=_cparams(("parallel", "parallel", "parallel")),
        name="mem_attn",
    )(p_attn, kv, kv)


def _outproj_kernel(yr_ref, ys_ref, ym_ref, h_ref, wr_ref, ws_ref, wm_ref, g_ref, b_ref,
                    rw_ref, rb_ref, h1_ref, lg_ref):
    mix = _dot(yr_ref[...], wr_ref[...]) + _dot(ys_ref[...], ws_ref[...]) + _dot(ym_ref[...], wm_ref[...])
    h1 = _layer_norm(DEEPNORM_ALPHA * h_ref[...] + mix, g_ref[...], b_ref[...])
    h1_ref[...] = h1
    hi = h1.astype(BF16)
    lo = (h1 - hi.astype(F32)).astype(BF16)
    both = _dot(hi, rw_ref[...])
    lg_ref[...] = both[:, :LANES] + both[:, LANES:] + _dot(lo, rw_ref[:, :LANES]) + rb_ref[...]


def _outproj(y_r, y_s, y_m, h, w_r, w_s, w_m, g, b, r_w, r_b, tm):
    m, d = h.shape
    full = lambda a: pl.BlockSpec(a.shape, lambda i: (0, 0))
    rows = lambda a: pl.BlockSpec((tm, a.shape[1]), lambda i: (i, 0))
    return pl.pallas_call(
        _outproj_kernel,
        grid=(m // tm,),
        in_specs=[rows(y_r), rows(y_s), rows(y_m), rows(h), full(w_r), full(w_s), full(w_m),
                  full(g), full(b), full(r_w), full(r_b)],
        out_specs=[pl.BlockSpec((tm, d), lambda i: (i, 0)),
                   pl.BlockSpec((tm, LANES), lambda i: (i, 0))],
        out_shape=[jax.ShapeDtypeStruct((m, d), F32),
                   jax.ShapeDtypeStruct((m, LANES), F32)],
        compiler_params=_cparams(("parallel",)),
        name="outproj",
    )(y_r, y_s, y_m, h, w_r, w_s, w_m, g, b, r_w, r_b)


def _route_kernel(lg_ref, id_ref, wt_ref, *, n_groups, per_group):
    lg = lg_ref[...]
    lane_i = lax.broadcasted_iota(jnp.int32, lg.shape, 1)
    lane = lane_i.astype(F32)
    neg = jnp.float32(-jnp.inf)
    big = jnp.float32(2 ** 20)

    def first_max(vals):
        mx = jnp.max(vals, axis=-1, keepdims=True)
        idx = jnp.min(jnp.where(vals == mx, lane, big), axis=-1, keepdims=True)
        return mx, idx

    is_group = lane < n_groups
    gmax, gidx = first_max(jnp.where(is_group, lg, neg))
    gsum = jnp.sum(jnp.where(is_group, jnp.exp(lg - gmax), 0.0), axis=-1, keepdims=True)
    group_w = 1.0 / gsum
    lo = n_groups + gidx * per_group
    in_group = (lane >= lo) & (lane < lo + per_group)
    v1, i1 = first_max(jnp.where(in_group, lg, neg))
    v2, i2 = first_max(jnp.where(in_group & (lane != i1), lg, neg))
    e2 = jnp.exp(v2 - v1)
    w1 = group_w / (1.0 + e2)
    w2 = group_w * e2 / (1.0 + e2)
    e1 = (i1 - n_groups).astype(jnp.int32)
    e2i = (i2 - n_groups).astype(jnp.int32)
    id_ref[...] = jnp.where(lane_i == 0, e1, jnp.where(lane_i == 1, e2i, 0))
    wt_ref[...] = jnp.where(lane_i == 0, w1, jnp.where(lane_i == 1, w2, 0.0))


def _route(logits, n_groups, per_group, tm):
    m = logits.shape[0]
    kern = functools.partial(_route_kernel, n_groups=n_groups, per_group=per_group)
    spec = pl.BlockSpec((tm, LANES), lambda i: (i, 0))
    return pl.pallas_call(
        kern,
        grid=(m // tm,),
        in_specs=[spec],
        out_specs=[spec, spec],
        out_shape=[jax.ShapeDtypeStruct((m, LANES), jnp.int32),
                   jax.ShapeDtypeStruct((m, LANES), F32)],
        compiler_params=_cparams(("parallel",)),
        name="route",
    )(logits)


def _row_copy(src_hbm, dst_vmem, sem, src_row, dst_row):
    return pltpu.make_async_copy(src_hbm.at[pl.ds(src_row, 1)], dst_vmem.at[pl.ds(dst_row, 1)], sem)


def _gather_cparams(sem):
    return pltpu.CompilerParams(dimension_semantics=sem, vmem_limit_bytes=VMEM_LIMIT,
                                disable_bounds_checks=True)


def _ffn_kernel(tok_ref, be_ref, nv_ref, nu_ref, h_hbm, wg_ref, wu_ref, wd_ref, o_ref,
                xf_ref, xb_ref, sem, *, rows, sub, nj):
    i = pl.program_id(0)
    j = pl.program_id(1)
    n_used = nu_ref[0]
    used = i < n_used
    slot = i % 2

    def issue(block, lo, hi, buf):
        base = block * rows

        def body(r, _):
            _row_copy(h_hbm, xf_ref.at[buf], sem.at[buf], tok_ref[base + r], r).start()
            return 0

        lax.fori_loop(lo, hi, body, 0)

    def drain(n, buf):
        def body(r, _):
            _row_copy(h_hbm, xf_ref.at[buf], sem.at[buf], 0, r).wait()
            return 0

        lax.fori_loop(0, n, body, 0)

    @pl.when((i == 0) & (j == 0))
    def _():
        xf_ref[...] = jnp.zeros_like(xf_ref)
        issue(0, 0, nv_ref[0], 0)

    @pl.when(j == 0)
    def _():
        o_ref[...] = jnp.zeros_like(o_ref)

    @pl.when(used & (j == 0))
    def _():
        drain(nv_ref[i], slot)

    @pl.when(i + 1 < n_used)
    def _():
        nxt = nv_ref[i + 1]
        q = rows // nj
        issue(i + 1, jnp.minimum(j * q, nxt), jnp.minimum((j + 1) * q, nxt), 1 - slot)

    n_sub = (nv_ref[i] + sub - 1) // sub
    for n in range(1, rows // sub + 1):
        @pl.when(used & (n_sub == n))
        def _(n=n):
            r = n * sub

            @pl.when(j == 0)
            def _():
                xb_ref[0:r, :] = xf_ref[slot, 0:r, :].astype(BF16)

            xb = xb_ref[0:r, :]
            gate = _dot(xb, wg_ref[...].astype(BF16))
            up = _dot(xb, wu_ref[...].astype(BF16))
            hid = (gate * jax.nn.sigmoid(gate)) * up
            o_ref[0:r, :] += _dot(hid.astype(BF16), wd_ref[...].astype(BF16))


def _moe_ffn(slot_tok, block_expert, n_valid, n_used, h1, w_gate, w_up, w_down, n_blocks, rows, tf):
    d = w_gate.shape[1]
    de = w_gate.shape[2]
    nj = de // tf
    kern = functools.partial(_ffn_kernel, rows=rows, sub=MOE_SUB, nj=nj)

    def jidx(i, j, nu):
        return jnp.where(i < nu[0], j, nj - 1)

    return pl.pallas_call(
        kern,
        grid_spec=pltpu.PrefetchScalarGridSpec(
            num_scalar_prefetch=4,
            grid=(n_blocks, nj),
            in_specs=[
                pl.BlockSpec(memory_space=pl.ANY),
                pl.BlockSpec((None, d, tf), lambda i, j, tok, be, nv, nu: (be[i], 0, jidx(i, j, nu))),
                pl.BlockSpec((None, d, tf), lambda i, j, tok, be, nv, nu: (be[i], 0, jidx(i, j, nu))),
                pl.BlockSpec((None, tf, d), lambda i, j, tok, be, nv, nu: (be[i], jidx(i, j, nu), 0)),
            ],
            out_specs=pl.BlockSpec((rows, d), lambda i, j, tok, be, nv, nu: (i, 0)),
            scratch_shapes=[pltpu.VMEM((2, rows, d), F32),
                            pltpu.VMEM((rows, d), BF16),
                            pltpu.SemaphoreType.DMA((2,))],
        ),
        out_shape=jax.ShapeDtypeStruct((n_blocks * rows, d), F32),
        compiler_params=_gather_cparams(("arbitrary", "arbitrary")),
        name="moe_ffn",
    )(slot_tok, block_expert, n_valid, n_used, h1, w_gate, w_up, w_down)


def _combine_kernel(pos_ref, h_ref, wt_ref, g_ref, b_ref, y_hbm, o_ref, buf_ref, sem, *, tm, n_tiles):
    i = pl.program_id(0)
    slot = i % 2

    def issue(tile, buf):
        base = tile * tm

        def body(t, _):
            _row_copy(y_hbm, buf_ref.at[buf, 0], sem.at[buf], pos_ref[2 * (base + t)], t).start()
            _row_copy(y_hbm, buf_ref.at[buf, 1], sem.at[buf], pos_ref[2 * (base + t) + 1], t).start()
            return 0

        lax.fori_loop(0, tm, body, 0)

    @pl.when(i == 0)
    def _():
        issue(0, 0)

    @pl.when(i + 1 < n_tiles)
    def _():
        issue(i + 1, 1 - slot)

    def drain(t, _):
        _row_copy(y_hbm, buf_ref.at[slot, 0], sem.at[slot], 0, t).wait()
        _row_copy(y_hbm, buf_ref.at[slot, 1], sem.at[slot], 0, t).wait()
        return 0

    lax.fori_loop(0, tm, drain, 0)
    wt = wt_ref[...]
    ffn = buf_ref[slot, 0] * wt[:, 0:1] + buf_ref[slot, 1] * wt[:, 1:2]
    o_ref[...] = _layer_norm(DEEPNORM_ALPHA * h_ref[...] + ffn, g_ref[...], b_ref[...])


def _combine(pos, h1, wts, g, b, ys, tm):
    m, d = h1.shape
    kern = functools.partial(_combine_kernel, tm=tm, n_tiles=m // tm)
    return pl.pallas_call(
        kern,
        grid_spec=pltpu.PrefetchScalarGridSpec(
            num_scalar_prefetch=1,
            grid=(m // tm,),
            in_specs=[
                pl.BlockSpec((tm, d), lambda i, pos: (i, 0)),
                pl.BlockSpec((tm, LANES), lambda i, pos: (i, 0)),
                pl.BlockSpec((1, d), lambda i, pos: (0, 0)),
                pl.BlockSpec((1, d), lambda i, pos: (0, 0)),
                pl.BlockSpec(memory_space=pl.ANY),
            ],
            out_specs=pl.BlockSpec((tm, d), lambda i, pos: (i, 0)),
            scratch_shapes=[pltpu.VMEM((2, 2, tm, d), F32), pltpu.SemaphoreType.DMA((2,))],
        ),
        out_shape=jax.ShapeDtypeStruct((m, d), F32),
        compiler_params=_gather_cparams(("arbitrary",)),
        name="moe_combine",
    )(pos, h1, wts, g, b, ys)


def _dispatch_plan(expert_ids, n_experts, rows, sub):
    n_tok = expert_ids.shape[0]
    n_assign = n_tok * 2
    n_blocks = -(-n_assign // rows) + n_experts
    flat_e = expert_ids.reshape(-1)
    counts = jnp.sum(flat_e[:, None] == jnp.arange(n_experts, dtype=jnp.int32)[None, :], axis=0,
                     dtype=jnp.int32)
    padded = (counts + rows - 1) // rows * rows
    pad_end = jnp.cumsum(padded)
    pad_start = pad_end - padded
    start = jnp.cumsum(counts) - counts
    order = jnp.argsort(flat_e, stable=True).astype(jnp.int32)
    sorted_e = flat_e[order]
    dest = pad_start[sorted_e] + jnp.arange(n_assign, dtype=jnp.int32) - start[sorted_e]
    slot_tok = jnp.zeros((n_blocks * rows,), jnp.int32).at[dest].set(order // 2)
    pos = jnp.zeros((n_assign,), jnp.int32).at[order].set(dest)
    n_used = (pad_end[-1] // rows).astype(jnp.int32)
    blk = jnp.arange(n_blocks, dtype=jnp.int32)
    blk_c = jnp.minimum(blk, n_used - 1)
    block_expert = jnp.minimum(jnp.searchsorted(pad_end, blk_c * rows, side="right"),
                               n_experts - 1).astype(jnp.int32)
    valid = jnp.clip(counts[block_expert] - (blk_c * rows - pad_start[block_expert]), 0, rows)
    n_valid = jnp.where(blk < n_used, valid, 0).astype(jnp.int32)
    return slot_tok, pos, n_used.reshape(1), block_expert, n_valid, n_blocks


def _pick(n, pref):
    t = min(pref, n)
    while n % t:
        t //= 2
    return t


def kernel(x, mem, ln_in_g, ln_in_b, w_in, tshift_mu, w0, w_decay_up, a0, w_a_up, w_g_up, k_k, k_a, r_k,
           lnx_g, lnx_b, w_mem_kv, w_out, ln1_g, ln1_b, router_group, router_group_b, router_expert,
           router_expert_b, w_e_gate, w_e_up, w_e_down, ln2_g, ln2_b):
    batch, seq, d = x.shape
    mem_len = mem.shape[1]
    m = batch * seq
    c = w0.shape[1]
    dr, ar, gr = w_decay_up.shape[1], w_a_up.shape[1], w_g_up.shape[1]
    rwkv_cols = 3 * c + dr + ar + gr
    sb_w = SB_HEADS * LANES
    mem_w = MEM_HEADS * LANES
    assert dr + ar == LANES and c % (2 * LANES) == 0 and w_in.shape[0] == DEPTH
    assert w_in.shape[2] == rwkv_cols + 3 * sb_w + mem_w
    n_experts = router_expert.shape[2]
    row = lambda a: a.reshape(1, -1)

    tn = 512
    low_w = -(-(dr + ar + gr) // LANES) * LANES
    rw_pad = -(-(3 * c + low_w) // tn) * tn
    wi = w_in[0]
    zpad = lambda n: jnp.zeros((d, n), F32)
    w_packed = jnp.concatenate(
        [wi[:, :rwkv_cols], zpad(rw_pad - rwkv_cols), wi[:, rwkv_cols:]], axis=1).astype(BF16)
    mu = tshift_mu[0]
    lp = low_w - (dr + ar + gr)
    prm = dict(
        mu_r=row(mu[:c]), mu_k=row(mu[c:2 * c]), mu_v=row(mu[2 * c:3 * c]),
        mu_low=row(jnp.concatenate([mu[3 * c:rwkv_cols], jnp.zeros((lp,), F32)])),
        w0=row(w0[0]), a0=row(a0[0]), k_k=row(k_k[0]), k_a=row(k_a[0]), r_k=row(r_k[0]),
        lnx_g=row(lnx_g[0]), lnx_b=row(lnx_b[0]),
        wd=jnp.concatenate([w_decay_up[0], jnp.zeros((ar, c), F32)], axis=0).astype(BF16),
        wa=jnp.concatenate([jnp.zeros((dr, c), F32), w_a_up[0]], axis=0).astype(BF16),
        wg=jnp.concatenate([w_g_up[0], jnp.zeros((lp, c), F32)], axis=0).astype(BF16),
    )
    assert (3 * c) % low_w == 0

    x2 = x.reshape(m, d)
    h, hb = _ln_in(x2, row(ln_in_g), row(ln_in_b), _pick(m, 256))
    p_rwkv, p_attn = _inproj(hb, w_packed, rw_pad, _pick(m, 1024), tn)

    y_rwkv = _wkv(p_rwkv, prm, batch, seq, n_pairs=4, tt=_pick(seq, 512))
    y_sb = _sb_attn(p_attn, batch, seq, tq=LANES)
    kv = _mem_kv(mem.reshape(batch * mem_len, d), w_mem_kv[0].astype(BF16), _pick(batch * mem_len, 256))
    y_mem = _mem_attn(p_attn, kv, batch, seq, mem_len, tq=_pick(seq, 512))

    wo = w_out[0].astype(BF16)
    r_w = jnp.concatenate([router_group[0], router_expert[0],
                           jnp.zeros((d, LANES - N_GROUPS - n_experts), F32)], axis=1)
    r_b = jnp.concatenate([router_group_b[0], router_expert_b[0],
                           jnp.zeros((LANES - N_GROUPS - n_experts,), F32)]).reshape(1, LANES)
    r_hi = r_w.astype(BF16)
    r_w2 = jnp.concatenate([r_hi, (r_w - r_hi.astype(F32)).astype(BF16)], axis=1)
    h1, logits = _outproj(y_rwkv, y_sb, y_mem, h, wo[:c], wo[c:c + sb_w], wo[c + sb_w:],
                          row(ln1_g[0]), row(ln1_b[0]), r_w2, r_b, _pick(m, 512))

    ids, wts = _route(logits, N_GROUPS, n_experts // N_GROUPS, _pick(m, 512))
    slot_tok, pos, n_used, block_expert, n_valid, n_blocks = _dispatch_plan(
        ids[:, :2], n_experts, MOE_ROWS, MOE_SUB)
    ys = _moe_ffn(slot_tok, block_expert, n_valid, n_used, h1, w_e_gate[0], w_e_up[0], w_e_down[0],
                  n_blocks, MOE_ROWS, tf=256)
    out = _combine(pos, h1, wts, row(ln2_g[0]), row(ln2_b[0]), ys, _pick(m, 256))
    return out.reshape(batch, seq, d)
```

```python
import functools

import jax
import jax.numpy as jnp
from jax import lax
from jax.experimental import pallas as pl
from jax.experimental.pallas import tpu as pltpu

F32 = jnp.float32
BF16 = jnp.bfloat16

SB_HEADS = 4
MEM_HEADS = 4
N_GROUPS = 8
DEPTH = 1
DEEPNORM_ALPHA = (2.0 * DEPTH) ** 0.25
LN_EPS = 1e-5
GN_EPS = 64e-5

LANES = 128
WKV_CHUNK = 64
MOE_ROWS = 512
MOE_SUB = 128
ROW_GROUP = 8
SB_BLOCKS_PER_ITER = 3
VMEM_LIMIT = 56 * 1024 * 1024


def _cparams(sem):
    return pltpu.CompilerParams(dimension_semantics=sem, vmem_limit_bytes=VMEM_LIMIT)


def _layer_norm(x, g, b):
    mu = jnp.mean(x, axis=-1, keepdims=True)
    xc = x - mu
    var = jnp.mean(xc * xc, axis=-1, keepdims=True)
    return xc * lax.rsqrt(var + LN_EPS) * g + b


def _split3(x):
    hi = x.astype(BF16)
    r1 = x - hi.astype(F32)
    mid = r1.astype(BF16)
    lo = (r1 - mid.astype(F32)).astype(BF16)
    return hi, mid, lo


def _dot(a, b):
    return jnp.dot(a, b, preferred_element_type=F32)


def _dot_nt(a, b):
    return lax.dot_general(a, b, (((1,), (1,)), ((), ())), preferred_element_type=F32)


def _dot_tn(a, b):
    return lax.dot_general(a, b, (((0,), (0,)), ((), ())), preferred_element_type=F32)


def _dot_f32_by_exact(x, m):
    hi, mid, lo = _split3(x)
    return _dot(hi, m) + _dot(mid, m) + _dot(lo, m)


def _dot_exact_by_f32(m, x):
    hi, mid, lo = _split3(x)
    return _dot(m, hi) + _dot(m, mid) + _dot(m, lo)


def _dot_hp(a, b):
    return jnp.dot(a, b, preferred_element_type=F32, precision=lax.Precision.HIGHEST)


def _ln_kernel(x_ref, g_ref, b_ref, h_ref, hb_ref):
    h = _layer_norm(x_ref[...], g_ref[...], b_ref[...])
    h_ref[...] = h
    hb_ref[...] = h.astype(BF16)


def _ln_in(x2, g, b, tm):
    m, d = x2.shape
    rows = pl.BlockSpec((tm, d), lambda i: (i, 0))
    vec = pl.BlockSpec((1, d), lambda i: (0, 0))
    return pl.pallas_call(
        _ln_kernel,
        grid=(m // tm,),
        in_specs=[rows, vec, vec],
        out_specs=[rows, rows],
        out_shape=[jax.ShapeDtypeStruct((m, d), F32), jax.ShapeDtypeStruct((m, d), BF16)],
        compiler_params=_cparams(("parallel",)),
        name="ln_in",
    )(x2, g, b)


def _inproj_kernel(hb_ref, w_ref, pr_ref, pa_ref, *, n_f32_tiles):
    n = pl.program_id(1)
    p = _dot(hb_ref[...], w_ref[...])

    @pl.when(n < n_f32_tiles)
    def _():
        pr_ref[...] = p

    @pl.when(n >= n_f32_tiles)
    def _():
        pa_ref[...] = p.astype(BF16)


def _inproj(hb, w_packed, n_rwkv_cols, tm, tn):
    m, d = hb.shape
    n_total = w_packed.shape[1]
    n_attn_cols = n_total - n_rwkv_cols
    nf = n_rwkv_cols // tn
    kern = functools.partial(_inproj_kernel, n_f32_tiles=nf)
    return pl.pallas_call(
        kern,
        grid=(m // tm, n_total // tn),
        in_specs=[
            pl.BlockSpec((tm, d), lambda i, n: (i, 0)),
            pl.BlockSpec((d, tn), lambda i, n: (0, n)),
        ],
        out_specs=[
            pl.BlockSpec((tm, tn), lambda i, n: (i, jnp.minimum(n, nf - 1))),
            pl.BlockSpec((tm, tn), lambda i, n: (i, jnp.maximum(n - nf, 0))),
        ],
        out_shape=[
            jax.ShapeDtypeStruct((m, n_rwkv_cols), F32),
            jax.ShapeDtypeStruct((m, n_attn_cols), BF16),
        ],
        compiler_params=_cparams(("parallel", "arbitrary")),
        name="inproj",
    )(hb, w_packed)


def _wkv_kernel(pr_ref, pk_ref, pv_ref, pl_ref,
                mur_ref, muk_ref, muv_ref, mul_ref,
                w0_ref, a0_ref, kk_ref, ka_ref, rk_ref, lg_ref, lb_ref,
                wd_ref, wa_ref, wg_ref,
                y_ref,
                s_ref, cr_ref, ck_ref, cv_ref, cl_ref,
                r_s, lw_s, k_s, v_s, a_s, b_s, y_s, g_s, bo_s, zero_ref, ring64, ring128, ringg,
                *, n_pairs, tt):
    C = WKV_CHUNK
    t_idx = pl.program_id(2)

    @pl.when(t_idx == 0)
    def _():
        s_ref[...] = jnp.zeros_like(s_ref)
        cr_ref[...] = jnp.zeros_like(cr_ref)
        ck_ref[...] = jnp.zeros_like(ck_ref)
        cv_ref[...] = jnp.zeros_like(cv_ref)
        cl_ref[...] = jnp.zeros_like(cl_ref)

    row = lax.broadcasted_iota(jnp.int32, (tt, 1), 0)

    def shifted(p_ref, carry_ref, mu_ref):
        p = p_ref[...]
        prev = pltpu.roll(p, shift=1, axis=0)
        prev = jnp.where(row == 0, carry_ref[...], prev)
        carry_ref[...] = p[tt - 1:tt, :]
        return p + (prev - p) * mu_ref[...]

    lane = lax.broadcasted_iota(jnp.int32, (LANES, LANES), 1)
    sub = lax.broadcasted_iota(jnp.int32, (LANES, LANES), 0)
    head_ones = jnp.where((lane // 64) == (sub // 64), 1.0, 0.0).astype(BF16)

    def head_sum(x):
        hi = x.astype(BF16)
        lo = (x - hi.astype(F32)).astype(BF16)
        return _dot(hi, head_ones) + _dot(lo, head_ones)

    low = shifted(pl_ref, cl_ref, mul_ref)
    da = low[:, 0:LANES]
    th = jnp.tanh(da).astype(BF16)
    sg = jax.nn.sigmoid(low[:, LANES:]).astype(BF16)
    da = da.astype(BF16)
    r = shifted(pr_ref, cr_ref, mur_ref)
    k = shifted(pk_ref, ck_ref, muk_ref)
    v = shifted(pv_ref, cv_ref, muv_ref)
    for g in range(n_pairs):
        cs = slice(g * LANES, (g + 1) * LANES)
        rg, kg, vg = r[:, cs], k[:, cs], v[:, cs]
        pre = w0_ref[:, cs] + _dot(th, wd_ref[:, cs])
        softplus_neg = jnp.maximum(-pre, 0.0) + jnp.log(1.0 + jnp.exp(-jnp.abs(pre)))
        w_log = -softplus_neg - 0.5
        lw = -jnp.exp(w_log)
        a = jax.nn.sigmoid(a0_ref[:, cs] + _dot(da, wa_ref[:, cs]))
        gate = _dot(sg, wg_ref[:, cs])
        kk = kg * kk_ref[:, cs]
        kk = kk * lax.rsqrt(jnp.maximum(head_sum(kk * kk), 1e-24))
        k2 = kg * (1.0 + (a - 1.0) * ka_ref[:, cs])
        bonus = head_sum(rg * k2 * rk_ref[:, cs]) * vg
        r_s[:, cs] = rg
        lw_s[:, cs] = lw
        k_s[:, cs] = k2
        v_s[:, cs] = vg
        a_s[:, cs] = -kk
        b_s[:, cs] = kk * a
        g_s[:, cs] = gate
        bo_s[:, cs] = bonus

    ci = lax.broadcasted_iota(jnp.int32, (C, 2 * C), 0)
    cj = lax.broadcasted_iota(jnp.int32, (C, 2 * C), 1)
    left = cj < C
    strict = (cj % C) < ci
    incl = (cj % C) <= ci
    tri_incl = jnp.where(lax.broadcasted_iota(jnp.int32, (C, C), 1)
                         <= lax.broadcasted_iota(jnp.int32, (C, C), 0), 1.0, 0.0).astype(BF16)
    lane_c = lax.broadcasted_iota(jnp.int32, (C, LANES), 1)
    m0 = lane_c < 64
    eye = jnp.where(lane == sub, 1.0, 0.0).astype(F32)
    blockdiag = (lane // 64) == (sub // 64)

    csl = [slice(g * LANES, (g + 1) * LANES) for g in range(n_pairs)]
    P = range(n_pairs)
    cat0 = lambda *xs: jnp.concatenate(xs, axis=0)
    cat1 = lambda *xs: jnp.concatenate(xs, axis=1)
    bf = lambda x: x.astype(BF16)

    n_chunks = tt // C
    MM0, MM1, RTB, VB = range(4)
    BK, AK, ATB = range(3)
    ring64[...] = jnp.zeros_like(ring64)
    ring128[...] = jnp.zeros_like(ring128)
    ringg[...] = jnp.zeros_like(ringg)

    def stage1(c, out):
        rows = pl.ds(pl.multiple_of(c * C, C), C)
        slot = c % 4
        ld = lambda ref: [ref[rows, csl[g]] for g in P]
        rc, lwc, kc, vc, ac, bc = ld(r_s), ld(lw_s), ld(k_s), ld(v_s), ld(a_s), ld(b_s)
        cum = [_dot_exact_by_f32(tri_incl, lwc[g]) for g in P]
        yield
        last = [cum[g][C - 1:C, :] for g in P]
        rt = [rc[g] * jnp.exp(cum[g]) for g in P]
        at = [ac[g] * jnp.exp(cum[g] - lwc[g]) for g in P]
        ginv = [jnp.exp(-cum[g]) for g in P]
        btb = [bf(bc[g] * ginv[g]) for g in P]
        ktb = [bf(kc[g] * ginv[g]) for g in P]
        ghat = [jnp.exp(last[g] - cum[g]) for g in P]
        lhs0 = [bf(cat0(jnp.where(m0, at[g], 0.0), jnp.where(m0, rt[g], 0.0))) for g in P]
        lhs1 = [bf(cat0(jnp.where(m0, 0.0, at[g]), jnp.where(m0, 0.0, rt[g]))) for g in P]
        for g in P:
            ring128[slot, g, BK] = cat0(bf(bc[g] * ghat[g]), bf(kc[g] * ghat[g]))
            ring128[slot, g, ATB] = cat0(lhs0[g][:C], lhs1[g][:C])
            ring64[slot, g, VB] = bf(vc[g])
            ring64[slot, g, RTB] = bf(rt[g])
            ringg[slot, g, 0:1, :] = jnp.exp(last[g])
        x0 = [_dot_nt(lhs0[g], cat0(btb[g], ktb[g])) for g in P]
        x1 = [_dot_nt(lhs1[g], cat0(ktb[g], btb[g])) for g in P]
        yield
        n_bd = [cat0(jnp.where(left & strict, x0[g][:C], 0.0),
                     jnp.where((~left) & strict, x1[g][:C], 0.0)) for g in P]
        for g in P:
            ring128[slot, g, AK] = bf(cat0(jnp.where((~left) & strict, x0[g][:C], 0.0),
                                           jnp.where(left & strict, x1[g][:C], 0.0)))
            ring64[slot, g, MM0] = bf(jnp.where(incl, x0[g][C:], 0.0))
            ring64[slot, g, MM1] = bf(jnp.where(incl, x1[g][C:], 0.0))
        out["t"] = [eye + n_bd[g] for g in P]
        nb = [bf(n_bd[g]) for g in P]
        out["pw"] = [bf(_dot(nb[g], nb[g])) for g in P]

    def inverse_level(t, pw):
        res = [_dot(pw[g], cat1(pw[g], bf(t[g]))) for g in P]
        return [t[g] + res[g][:, LANES:] for g in P], [bf(res[g][:, :LANES]) for g in P]

    def stage2(inp, out):
        t, pw = inp["t"], inp["pw"]
        for level in range(3):
            t, pw = inverse_level(t, pw)
            if level < 2:
                yield
        out["t"], out["pw"] = t, pw

    def stage3(inp, c, out):
        slot = c % 4
        t, pw = inverse_level(inp["t"], inp["pw"])
        vb = [ring64[slot, g, VB] for g in P]
        av = [_dot(ring128[slot, g, AK], cat0(vb[g], vb[g])) for g in P]
        yield
        t = [t[g] + _dot(pw[g], bf(t[g])) for g in P]
        av = [cat0(jnp.where(m0, av[g][:C], 0.0), jnp.where(m0, 0.0, av[g][C:])) for g in P]
        yield
        wu = [_dot(bf(t[g]), cat1(ring128[slot, g, ATB], bf(av[g]))) for g in P]
        out["w"] = [bf(wu[g][:C, :LANES] + wu[g][C:, :LANES]) for g in P]
        out["u0"] = [wu[g][:C, LANES:] + wu[g][C:, LANES:] for g in P]

    def stage4(inp, c, valid):
        rows = pl.ds(pl.multiple_of(c * C, C), C)
        slot = c % 4
        vb = [ring64[slot, g, VB] for g in P]
        s = [s_ref[g] for g in P]
        sb = [bf(s[g]) for g in P]
        ub = [bf(_dot_nt(inp["w"][g], sb[g]) + inp["u0"][g]) for g in P]
        ys = [_dot_nt(ring64[slot, g, RTB], sb[g]) for g in P]
        yield
        uv = [cat0(ub[g], vb[g]) for g in P]
        y = [ys[g] + jnp.where(m0, _dot(ring64[slot, g, MM0], uv[g]),
                               _dot(ring64[slot, g, MM1], cat0(vb[g], ub[g]))) for g in P]
        upd = [_dot_tn(uv[g], ring128[slot, g, BK]) for g in P]
        yield
        for g in P:
            s_new = s[g] * ringg[slot, g, 0:1, :] + jnp.where(blockdiag, upd[g], 0.0)
            s_ref[g] = jnp.where(valid, s_new, s[g])
            y_s[rows, csl[g]] = y[g]

    def interleave(gens):
        live = list(gens)
        while live:
            still = []
            for gen in live:
                try:
                    next(gen)
                    still.append(gen)
                except StopIteration:
                    pass
            live = still

    def body(it, carry):
        o1, o2, o3 = carry
        n1, n2, n3 = {}, {}, {}
        interleave([stage4(o3, jnp.maximum(it - 3, 0), it >= 3),
                    stage3(o2, jnp.maximum(it - 2, 0), n3),
                    stage2(o1, n2),
                    stage1(jnp.minimum(it, n_chunks - 1), n1)])
        return n1, n2, n3

    zero_ref[...] = jnp.zeros_like(zero_ref)
    zb = lambda r: [zero_ref[0:r, :].astype(BF16) for _ in P]
    zf = lambda r: [zero_ref[0:r, :] for _ in P]
    front = lambda: dict(t=zf(2 * C), pw=zb(2 * C))
    lax.fori_loop(0, n_chunks + 3, body, (front(), front(), dict(w=zb(C), u0=zf(C))))

    for g in range(n_pairs):
        cs = slice(g * LANES, (g + 1) * LANES)
        y = y_s[:, cs]
        mean = head_sum(y) * (1.0 / 64.0)
        yc = y - mean
        var = head_sum(yc * yc) * (1.0 / 64.0)
        yn = yc * lax.rsqrt(var + GN_EPS) * lg_ref[:, cs] + lb_ref[:, cs]
        y_ref[:, cs] = ((yn + bo_s[:, cs]) * g_s[:, cs]).astype(BF16)


def _wkv(p_rwkv, prm, batch, seq, n_pairs, tt):
    m = p_rwkv.shape[0]
    c = prm["w0"].shape[1]
    gw = n_pairs * LANES
    n_col_blocks = c // gw
    nt = seq // tt
    low_w = prm["mu_low"].shape[1]

    def pspec(off):
        return pl.BlockSpec((tt, gw), lambda b, g, t: (b * nt + t, off * n_col_blocks + g))

    def vspec():
        return pl.BlockSpec((1, gw), lambda b, g, t: (0, g))

    def wspec(rows):
        return pl.BlockSpec((rows, gw), lambda b, g, t: (0, g))

    kern = functools.partial(_wkv_kernel, n_pairs=n_pairs, tt=tt)
    tile = pltpu.VMEM((tt, gw), F32)
    return pl.pallas_call(
        kern,
        grid=(batch, n_col_blocks, nt),
        in_specs=[
            pspec(0), pspec(1), pspec(2),
            pl.BlockSpec((tt, low_w), lambda b, g, t: (b * nt + t, (3 * c) // low_w)),
            vspec(), vspec(), vspec(),
            pl.BlockSpec((1, low_w), lambda b, g, t: (0, 0)),
            vspec(), vspec(), vspec(), vspec(), vspec(), vspec(), vspec(),
            wspec(LANES), wspec(LANES), wspec(low_w - LANES),
        ],
        out_specs=pl.BlockSpec((tt, gw), lambda b, g, t: (b * nt + t, g)),
        out_shape=jax.ShapeDtypeStruct((m, c), BF16),
        scratch_shapes=[
            pltpu.VMEM((n_pairs, LANES, LANES), F32),
            pltpu.VMEM((1, gw), F32), pltpu.VMEM((1, gw), F32), pltpu.VMEM((1, gw), F32),
            pltpu.VMEM((1, low_w), F32),
            tile, tile, tile, tile, tile, tile, tile, tile, tile,
            pltpu.VMEM((LANES, LANES), F32),
            pltpu.VMEM((4, n_pairs, 4, WKV_CHUNK, LANES), BF16),
            pltpu.VMEM((4, n_pairs, 3, 2 * WKV_CHUNK, LANES), BF16),
            pltpu.VMEM((4, n_pairs, 8, LANES), F32),
        ],
        compiler_params=_cparams(("parallel", "parallel", "arbitrary")),
        name="wkv7",
    )(p_rwkv, p_rwkv, p_rwkv, p_rwkv,
      prm["mu_r"], prm["mu_k"], prm["mu_v"], prm["mu_low"],
      prm["w0"], prm["a0"], prm["k_k"], prm["k_a"], prm["r_k"], prm["lnx_g"], prm["lnx_b"],
      prm["wd"], prm["wa"], prm["wg"])


def _sb_kernel(q_ref, k_ref, v_ref, o_ref, acc_ref, right_ref, *, tq, scale):
    i = pl.program_id(1)
    row = lax.broadcasted_iota(jnp.int32, (tq, tq), 0)
    col = lax.broadcasted_iota(jnp.int32, (tq, tq), 1)
    r2 = lax.broadcasted_iota(jnp.int32, (tq, 2 * tq), 0)
    c2 = lax.broadcasted_iota(jnp.int32, (tq, 2 * tq), 1)
    after_and_total = jnp.where((c2 >= tq) | (r2 > c2), 1.0, 0.0).astype(BF16)
    diag = col < row

    heads = range(SB_HEADS)
    hsl = [slice(h * LANES, (h + 1) * LANES) for h in heads]

    def blocks(js, first):
        nb = len(js)
        it = [(b, h) for b in range(nb) for h in heads]
        ks = [pl.ds(pl.multiple_of(j * tq, tq), tq) for j in js]
        z = {p: _dot_nt(q_ref[:, hsl[p[1]]], k_ref[ks[p[0]], hsl[p[1]]]) * scale for p in it}
        sp = {p: jnp.maximum(z[p], 0.0) + jnp.log(1.0 + jnp.exp(-jnp.abs(z[p]))) for p in it}
        log_keep = {p: jnp.where(diag, -sp[p], 0.0) if first else -sp[p] for p in it}
        hi = {p: log_keep[p].astype(BF16) for p in it}
        lo = {p: (log_keep[p] - hi[p].astype(F32)).astype(BF16) for p in it}
        sums = {p: _dot(hi[p], after_and_total) + _dot(lo[p], after_and_total) for p in it}
        right = {}
        for h in heads:
            run = None if first else right_ref[h]
            for b in range(nb):
                right[(b, h)] = run
                tot = sums[(b, h)][:, tq:]
                run = tot if run is None else run + tot
            right[("end", h)] = run
        after = {p: sums[p][:, :tq] if right[p] is None else sums[p][:, :tq] + right[p] for p in it}
        attn = {p: jnp.exp(z[p] - sp[p] + after[p]) for p in it}
        if first:
            attn = {p: jnp.where(diag, attn[p], 0.0) for p in it}
        pv = {p: _dot(attn[p].astype(BF16), v_ref[ks[p[0]], hsl[p[1]]]) for p in it}
        for h in heads:
            tot = pv[(0, h)]
            for b in range(1, nb):
                tot = tot + pv[(b, h)]
            if first:
                acc_ref[h] = tot
            else:
                acc_ref[h] += tot
            right_ref[h] = right[("end", h)]

    blocks([i], True)

    def body(jj, _):
        j = i - 1 - SB_BLOCKS_PER_ITER * jj
        blocks([j - b for b in range(SB_BLOCKS_PER_ITER)], False)
        return 0

    lax.fori_loop(0, i // SB_BLOCKS_PER_ITER, body, 0)

    for rem in range(1, SB_BLOCKS_PER_ITER):
        @pl.when(i % SB_BLOCKS_PER_ITER == rem)
        def _(rem=rem):
            blocks(list(range(rem - 1, -1, -1)), False)

    for h in heads:
        o_ref[:, hsl[h]] = acc_ref[h].astype(BF16)


def _sb_attn(p_attn, batch, seq, tq):
    m = p_attn.shape[0]
    w = SB_HEADS * LANES
    nq = seq // tq
    kern = functools.partial(_sb_kernel, tq=tq, scale=LANES ** -0.5)
    return pl.pallas_call(
        kern,
        grid=(batch, nq),
        in_specs=[
            pl.BlockSpec((tq, w), lambda b, i: (b * nq + i, 0)),
            pl.BlockSpec((seq, w), lambda b, i: (b, 1)),
            pl.BlockSpec((seq, w), lambda b, i: (b, 2)),
        ],
        out_specs=pl.BlockSpec((tq, w), lambda b, i: (b * nq + i, 0)),
        out_shape=jax.ShapeDtypeStruct((m, w), BF16),
        scratch_shapes=[pltpu.VMEM((SB_HEADS, tq, tq), F32), pltpu.VMEM((SB_HEADS, tq, tq), F32)],
        compiler_params=_cparams(("parallel", "arbitrary")),
        name="sb_attn",
    )(p_attn, p_attn, p_attn)


def _mem_kv_kernel(m_ref, w_ref, o_ref):
    o_ref[...] = _dot(m_ref[...].astype(BF16), w_ref[...]).astype(BF16)


def _mem_kv(mem2, w_bf16, tm):
    m, d = mem2.shape
    n = w_bf16.shape[1]
    return pl.pallas_call(
        _mem_kv_kernel,
        grid=(m // tm,),
        in_specs=[pl.BlockSpec((tm, d), lambda i: (i, 0)),
                  pl.BlockSpec((d, n), lambda i: (0, 0))],
        out_specs=pl.BlockSpec((tm, n), lambda i: (i, 0)),
        out_shape=jax.ShapeDtypeStruct((m, n), BF16),
        compiler_params=_cparams(("parallel",)),
        name="mem_kv",
    )(mem2, w_bf16)


def _mem_attn_kernel(q_ref, k_ref, v_ref, o_ref, *, scale):
    s = _dot_nt(q_ref[...], k_ref[...]) * scale
    s = s - jnp.max(s, axis=-1, keepdims=True)
    e = jnp.exp(s)
    p = e / jnp.sum(e, axis=-1, keepdims=True)
    o_ref[...] = _dot(p.astype(BF16), v_ref[...]).astype(BF16)


def _mem_attn(p_attn, kv, batch, seq, mem_len, tq):
    m = p_attn.shape[0]
    hd = LANES
    nq = seq // tq
    q_off = 3 * SB_HEADS
    kern = functools.partial(_mem_attn_kernel, scale=hd ** -0.5)
    return pl.pallas_call(
        kern,
        grid=(batch, MEM_HEADS, nq),
        in_specs=[
            pl.BlockSpec((tq, hd), lambda b, h, i: (b * nq + i, q_off + h)),
            pl.BlockSpec((mem_len, hd), lambda b, h, i: (b, h)),
            pl.BlockSpec((mem_len, hd), lambda b, h, i: (b, MEM_HEADS + h)),
        ],
        out_specs=pl.BlockSpec((tq, hd), lambda b, h, i: (b * nq + i, h)),
        out_shape=jax.ShapeDtypeStruct((m, MEM_HEADS * hd), BF16),
        compiler_params=_cparams(("parallel", "parallel", "parallel")),
        name="mem_attn",
    )(p_attn, kv, kv)


def _outproj_kernel(yr_ref, ys_ref, ym_ref, h_ref, wr_ref, ws_ref, wm_ref, g_ref, b_ref,
                    rw_ref, rb_ref, h1_ref, lg_ref):
    mix = _dot(yr_ref[...], wr_ref[...]) + _dot(ys_ref[...], ws_ref[...]) + _dot(ym_ref[...], wm_ref[...])
    h1 = _layer_norm(DEEPNORM_ALPHA * h_ref[...] + mix, g_ref[...], b_ref[...])
    h1_ref[...] = h1
    hi = h1.astype(BF16)
    lo = (h1 - hi.astype(F32)).astype(BF16)
    both = _dot(hi, rw_ref[...])
    lg_ref[...] = both[:, :LANES] + both[:, LANES:] + _dot(lo, rw_ref[:, :LANES]) + rb_ref[...]


def _outproj(y_r, y_s, y_m, h, w_r, w_s, w_m, g, b, r_w, r_b, tm):
    m, d = h.shape
    full = lambda a: pl.BlockSpec(a.shape, lambda i: (0, 0))
    rows = lambda a: pl.BlockSpec((tm, a.shape[1]), lambda i: (i, 0))
    return pl.pallas_call(
        _outproj_kernel,
        grid=(m // tm,),
        in_specs=[rows(y_r), rows(y_s), rows(y_m), rows(h), full(w_r), full(w_s), full(w_m),
                  full(g), full(b), full(r_w), full(r_b)],
        out_specs=[pl.BlockSpec((tm, d), lambda i: (i, 0)),
                   pl.BlockSpec((tm, LANES), lambda i: (i, 0))],
        out_shape=[jax.ShapeDtypeStruct((m, d), F32),
                   jax.ShapeDtypeStruct((m, LANES), F32)],
        compiler_params=_cparams(("parallel",)),
        name="outproj",
    )(y_r, y_s, y_m, h, w_r, w_s, w_m, g, b, r_w, r_b)


def _route_kernel(lg_ref, id_ref, wt_ref, *, n_groups, per_group):
    lg = lg_ref[...]
    lane_i = lax.broadcasted_iota(jnp.int32, lg.shape, 1)
    lane = lane_i.astype(F32)
    neg = jnp.float32(-jnp.inf)
    big = jnp.float32(2 ** 20)

    def first_max(vals):
        mx = jnp.max(vals, axis=-1, keepdims=True)
        idx = jnp.min(jnp.where(vals == mx, lane, big), axis=-1, keepdims=True)
        return mx, idx

    is_group = lane < n_groups
    gmax, gidx = first_max(jnp.where(is_group, lg, neg))
    gsum = jnp.sum(jnp.where(is_group, jnp.exp(lg - gmax), 0.0), axis=-1, keepdims=True)
    group_w = 1.0 / gsum
    lo = n_groups + gidx * per_group
    in_group = (lane >= lo) & (lane < lo + per_group)
    v1, i1 = first_max(jnp.where(in_group, lg, neg))
    v2, i2 = first_max(jnp.where(in_group & (lane != i1), lg, neg))
    e2 = jnp.exp(v2 - v1)
    w1 = group_w / (1.0 + e2)
    w2 = group_w * e2 / (1.0 + e2)
    ids = jnp.where(lane_i == 0, i1 - n_groups, jnp.where(lane_i == 1, i2 - n_groups, 0.0))
    id_ref[...] = ids.T[0:8, :].astype(jnp.int32)
    wt_ref[...] = jnp.where(lane_i == 0, w1, jnp.where(lane_i == 1, w2, 0.0))


def _route(logits, n_groups, per_group, tm):
    m = logits.shape[0]
    kern = functools.partial(_route_kernel, n_groups=n_groups, per_group=per_group)
    spec = pl.BlockSpec((tm, LANES), lambda i: (i, 0))
    return pl.pallas_call(
        kern,
        grid=(m // tm,),
        in_specs=[spec],
        out_specs=[pl.BlockSpec((8, tm), lambda i: (0, i)), spec],
        out_shape=[jax.ShapeDtypeStruct((8, m), jnp.int32),
                   jax.ShapeDtypeStruct((m, LANES), F32)],
        compiler_params=_cparams(("parallel",)),
        name="route",
    )(logits)


def _plan_kernel(id_ref, pos_ref, meta_ref, cnt_ref, base_ref, start_ref, *, tb, rows, n_experts, nb):
    phase = pl.program_id(0)
    j = pl.program_id(1)
    sub = lax.broadcasted_iota(jnp.int32, (LANES, tb), 0)
    e1 = id_ref[0:1, :]
    e2 = id_ref[1:2, :]
    hit1 = jnp.where(sub == e1, 1.0, 0.0)
    hit2 = jnp.where(sub == e2, 1.0, 0.0)
    hits = hit1 + hit2
    per_expert = jnp.sum(hits, axis=1, keepdims=True)

    @pl.when((phase == 0) & (j == 0))
    def _():
        cnt_ref[...] = jnp.zeros_like(cnt_ref)

    @pl.when(phase == 0)
    def _():
        cnt_ref[...] += per_expert

    sq_r = lax.broadcasted_iota(jnp.int32, (LANES, LANES), 0)
    sq_c = lax.broadcasted_iota(jnp.int32, (LANES, LANES), 1)

    @pl.when((phase == 1) & (j == 0))
    def _():
        n_blk = jnp.floor((cnt_ref[...] + (rows - 1)) * (1.0 / rows))
        before = jnp.where(sq_c < sq_r, 1.0, 0.0).astype(BF16)
        blk_start = _dot(before, jnp.broadcast_to(n_blk, (LANES, LANES)).astype(BF16))
        start_ref[...] = blk_start[:, 0:1]
        base_ref[...] = jnp.zeros_like(base_ref)
        blk_end = blk_start + n_blk
        n_used = jnp.sum(jnp.where(sq_r[:, 0:1] < n_experts, n_blk, 0.0), axis=0, keepdims=True)
        owner = jnp.sum(jnp.where((sq_r < n_experts) & (blk_end <= sq_c.astype(F32)), 1.0, 0.0),
                        axis=0, keepdims=True)
        last_owner = jnp.max(jnp.where(n_blk > 0.0, sq_r[:, 0:1].astype(F32), 0.0), axis=0, keepdims=True)
        blk = sq_c[0:1, :].astype(F32)
        in_use = blk < n_used
        owner = jnp.where(in_use, owner, last_owner)
        mine = sq_r.astype(F32) == owner
        cnt_o = jnp.sum(jnp.where(mine, cnt_ref[...], 0.0), axis=0, keepdims=True)
        start_o = jnp.sum(jnp.where(mine, blk_start, 0.0), axis=0, keepdims=True)
        valid = jnp.clip(cnt_o - (blk - start_o) * rows, 0.0, float(rows))
        valid = jnp.where(in_use, valid, 0.0)
        row8 = lax.broadcasted_iota(jnp.int32, (8, LANES), 0)
        meta = jnp.where(row8 == 0, owner, jnp.where(row8 == 1, valid, jnp.where(row8 == 2, n_used, 0.0)))
        meta_ref[...] = meta.astype(jnp.int32)

    @pl.when(phase == 1)
    def _():
        tr = lax.broadcasted_iota(jnp.int32, (tb, tb), 0)
        tc = lax.broadcasted_iota(jnp.int32, (tb, tb), 1)
        earlier = jnp.where(tr < tc, 1.0, 0.0).astype(BF16)
        seen = _dot(hits.astype(BF16), earlier)
        slot = start_ref[...] * rows + base_ref[...] + seen
        p1 = jnp.sum(hit1 * slot, axis=0, keepdims=True)
        p2 = jnp.sum(hit2 * slot, axis=0, keepdims=True)
        row8 = lax.broadcasted_iota(jnp.int32, (8, tb), 0)
        pos_ref[...] = jnp.where(row8 == 0, p1, jnp.where(row8 == 1, p2, 0.0)).astype(jnp.int32)
        base_ref[...] += per_expert


def _plan(ids_t, n_experts, rows, tb):
    m = ids_t.shape[1]
    nb = m // tb
    kern = functools.partial(_plan_kernel, tb=tb, rows=rows, n_experts=n_experts, nb=nb)
    col = pltpu.VMEM((LANES, 1), F32)
    return pl.pallas_call(
        kern,
        grid=(2, nb),
        in_specs=[pl.BlockSpec((8, tb), lambda p, j: (0, j))],
        out_specs=[pl.BlockSpec((8, tb), lambda p, j: (0, j * p)),
                   pl.BlockSpec((8, LANES), lambda p, j: (0, 0))],
        out_shape=[jax.ShapeDtypeStruct((8, m), jnp.int32),
                   jax.ShapeDtypeStruct((8, LANES), jnp.int32)],
        scratch_shapes=[col, col, col],
        compiler_params=_cparams(("arbitrary", "arbitrary")),
        name="moe_plan",
    )(ids_t)


def _for_range(lo, hi, body):
    full = (hi - lo) // ROW_GROUP

    def group(g, _):
        for u in range(ROW_GROUP):
            body(lo + g * ROW_GROUP + u)
        return 0

    lax.fori_loop(0, full, group, 0)

    def one(r, _):
        body(r)
        return 0

    lax.fori_loop(lo + full * ROW_GROUP, hi, one, 0)


def _row_in(src_hbm, dst_vmem, sem, src_row, dst_row):
    return pltpu.make_async_copy(src_hbm.at[pl.ds(src_row, 1)], dst_vmem.at[pl.ds(dst_row, 1)], sem)


def _row_out(src_vmem, dst_hbm, sem, src_row, dst_row):
    return pltpu.make_async_copy(src_vmem.at[pl.ds(src_row, 1)], dst_hbm.at[pl.ds(dst_row, 1)], sem)


def _gather_cparams(sem):
    return pltpu.CompilerParams(dimension_semantics=sem, vmem_limit_bytes=VMEM_LIMIT,
                                disable_bounds_checks=True)


def _ffn_kernel(asg_ref, be_ref, nv_ref, nu_ref, h_hbm, wg_ref, wu_ref, wd_ref, y_hbm,
                xf_ref, xb_ref, acc_ref, sem_in, sem_out, *, rows, sub, nj, n_tok, n_blocks):
    i = pl.program_id(0)
    j = pl.program_id(1)
    n_used = nu_ref[0]
    used = i < n_used
    slot = i % 2

    def gather(block, lo, hi, buf):
        def body(r):
            a = asg_ref[block * rows + r]
            tok = jnp.where(a >= n_tok, a - n_tok, a)
            _row_in(h_hbm, xf_ref.at[buf], sem_in.at[buf], tok, r).start()

        _for_range(lo, hi, body)

    def gather_wait(block, buf):
        _for_range(0, nv_ref[block], lambda r: _row_in(h_hbm, xf_ref.at[buf], sem_in.at[buf], 0, r).wait())

    def emit(block, buf):
        def body(r):
            _row_out(acc_ref.at[buf], y_hbm, sem_out.at[buf], r, asg_ref[block * rows + r]).start()

        _for_range(0, nv_ref[block], body)

    def emit_wait(block, buf):
        _for_range(0, nv_ref[block], lambda r: _row_out(acc_ref.at[buf], y_hbm, sem_out.at[buf], r, 0).wait())

    @pl.when((i == 0) & (j == 0))
    def _():
        xf_ref[...] = jnp.zeros_like(xf_ref)
        gather(0, 0, nv_ref[0], 0)

    @pl.when(used & (j == 0))
    def _():
        gather_wait(i, slot)

    @pl.when(i + 1 < n_used)
    def _():
        nxt = nv_ref[i + 1]
        q = rows // nj
        gather(i + 1, jnp.minimum(j * q, nxt), jnp.minimum((j + 1) * q, nxt), 1 - slot)

    n_sub = (nv_ref[i] + sub - 1) // sub
    for n in range(1, rows // sub + 1):
        @pl.when(used & (n_sub == n))
        def _(n=n):
            r = n * sub

            @pl.when(j == 0)
            def _():
                xb_ref[0:r, :] = xf_ref[slot, 0:r, :].astype(BF16)
                acc_ref[slot, 0:r, :] = jnp.zeros((r, acc_ref.shape[2]), F32)

            xb = xb_ref[0:r, :]
            gate = _dot(xb, wg_ref[...].astype(BF16))
            up = _dot(xb, wu_ref[...].astype(BF16))
            hid = (gate * jax.nn.sigmoid(gate)) * up
            acc_ref[slot, 0:r, :] += _dot(hid.astype(BF16), wd_ref[...].astype(BF16))

    @pl.when((j == nj - 1) & (i >= 1) & (i <= n_used))
    def _():
        emit_wait(i - 1, 1 - slot)

    @pl.when((j == nj - 1) & used)
    def _():
        emit(i, slot)

    @pl.when((j == nj - 1) & used & (i == n_blocks - 1))
    def _():
        emit_wait(i, slot)


def _moe_ffn(slot_asg, block_expert, n_valid, n_used, h1, w_gate, w_up, w_down, n_blocks, rows, tf):
    n_tok, d = h1.shape
    de = w_gate.shape[2]
    nj = de // tf
    kern = functools.partial(_ffn_kernel, rows=rows, sub=MOE_SUB, nj=nj, n_tok=n_tok, n_blocks=n_blocks)

    def jidx(i, j, nu):
        return jnp.where(i < nu[0], j, nj - 1)

    return pl.pallas_call(
        kern,
        grid_spec=pltpu.PrefetchScalarGridSpec(
            num_scalar_prefetch=4,
            grid=(n_blocks, nj),
            in_specs=[
                pl.BlockSpec(memory_space=pl.ANY),
                pl.BlockSpec((None, d, tf), lambda i, j, asg, be, nv, nu: (be[i], 0, jidx(i, j, nu))),
                pl.BlockSpec((None, d, tf), lambda i, j, asg, be, nv, nu: (be[i], 0, jidx(i, j, nu))),
                pl.BlockSpec((None, tf, d), lambda i, j, asg, be, nv, nu: (be[i], jidx(i, j, nu), 0)),
            ],
            out_specs=pl.BlockSpec(memory_space=pl.ANY),
            scratch_shapes=[pltpu.VMEM((2, rows, d), F32),
                            pltpu.VMEM((rows, d), BF16),
                            pltpu.VMEM((2, rows, d), F32),
                            pltpu.SemaphoreType.DMA((2,)),
                            pltpu.SemaphoreType.DMA((2,))],
        ),
        out_shape=jax.ShapeDtypeStruct((2 * n_tok, d), F32),
        compiler_params=_gather_cparams(("arbitrary", "arbitrary")),
        name="moe_ffn",
    )(slot_asg, block_expert, n_valid, n_used, h1, w_gate, w_up, w_down)


def _combine_kernel(h_ref, y0_ref, y1_ref, wt_ref, g_ref, b_ref, o_ref):
    wt = wt_ref[...]
    ffn = y0_ref[...] * wt[:, 0:1] + y1_ref[...] * wt[:, 1:2]
    o_ref[...] = _layer_norm(DEEPNORM_ALPHA * h_ref[...] + ffn, g_ref[...], b_ref[...])


def _combine(h1, ys, wts, g, b, tm):
    m, d = h1.shape
    nt = m // tm
    rows = pl.BlockSpec((tm, d), lambda i: (i, 0))
    vec = pl.BlockSpec((1, d), lambda i: (0, 0))
    return pl.pallas_call(
        _combine_kernel,
        grid=(nt,),
        in_specs=[rows, rows, pl.BlockSpec((tm, d), lambda i: (nt + i, 0)),
                  pl.BlockSpec((tm, LANES), lambda i: (i, 0)), vec, vec],
        out_specs=rows,
        out_shape=jax.ShapeDtypeStruct((m, d), F32),
        compiler_params=_cparams(("parallel",)),
        name="moe_combine",
    )(h1, ys, ys, wts, g, b)


def _dispatch_plan(ids_t, n_experts, rows):
    m = ids_t.shape[1]
    n_blocks = -(-(2 * m) // rows) + n_experts
    assert n_blocks <= LANES
    pos_t, meta = _plan(ids_t, n_experts, rows, _pick(m, 512))
    pos = pos_t[:2].reshape(-1)
    slot_asg = jnp.zeros((n_blocks * rows,), jnp.int32).at[pos].set(jnp.arange(2 * m, dtype=jnp.int32))
    return slot_asg, meta[2, :1], meta[0, :n_blocks], meta[1, :n_blocks], n_blocks


def _pick(n, pref):
    t = min(pref, n)
    while n % t:
        t //= 2
    return t


def kernel(x, mem, ln_in_g, ln_in_b, w_in, tshift_mu, w0, w_decay_up, a0, w_a_up, w_g_up, k_k, k_a, r_k,
           lnx_g, lnx_b, w_mem_kv, w_out, ln1_g, ln1_b, router_group, router_group_b, router_expert,
           router_expert_b, w_e_gate, w_e_up, w_e_down, ln2_g, ln2_b):
    batch, seq, d = x.shape
    mem_len = mem.shape[1]
    m = batch * seq
    c = w0.shape[1]
    dr, ar, gr = w_decay_up.shape[1], w_a_up.shape[1], w_g_up.shape[1]
    rwkv_cols = 3 * c + dr + ar + gr
    sb_w = SB_HEADS * LANES
    mem_w = MEM_HEADS * LANES
    assert dr + ar == LANES and c % (2 * LANES) == 0 and w_in.shape[0] == DEPTH
    assert w_in.shape[2] == rwkv_cols + 3 * sb_w + mem_w
    n_experts = router_expert.shape[2]
    row = lambda a: a.reshape(1, -1)

    tn = 512
    low_w = -(-(dr + ar + gr) // LANES) * LANES
    rw_pad = -(-(3 * c + low_w) // tn) * tn
    wi = w_in[0]
    zpad = lambda n: jnp.zeros((d, n), F32)
    w_packed = jnp.concatenate(
        [wi[:, :rwkv_cols], zpad(rw_pad - rwkv_cols), wi[:, rwkv_cols:]], axis=1).astype(BF16)
    mu = tshift_mu[0]
    lp = low_w - (dr + ar + gr)
    prm = dict(
        mu_r=row(mu[:c]), mu_k=row(mu[c:2 * c]), mu_v=row(mu[2 * c:3 * c]),
        mu_low=row(jnp.concatenate([mu[3 * c:rwkv_cols], jnp.zeros((lp,), F32)])),
        w0=row(w0[0]), a0=row(a0[0]), k_k=row(k_k[0]), k_a=row(k_a[0]), r_k=row(r_k[0]),
        lnx_g=row(lnx_g[0]), lnx_b=row(lnx_b[0]),
        wd=jnp.concatenate([w_decay_up[0], jnp.zeros((ar, c), F32)], axis=0).astype(BF16),
        wa=jnp.concatenate([jnp.zeros((dr, c), F32), w_a_up[0]], axis=0).astype(BF16),
        wg=jnp.concatenate([w_g_up[0], jnp.zeros((lp, c), F32)], axis=0).astype(BF16),
    )
    assert (3 * c) % low_w == 0

    x2 = x.reshape(m, d)
    h, hb = _ln_in(x2, row(ln_in_g), row(ln_in_b), _pick(m, 256))
    p_rwkv, p_attn = _inproj(hb, w_packed, rw_pad, _pick(m, 1024), tn)

    y_rwkv = _wkv(p_rwkv, prm, batch, seq, n_pairs=4, tt=_pick(seq, 1024))
    y_sb = _sb_attn(p_attn, batch, seq, tq=LANES)
    kv = _mem_kv(mem.reshape(batch * mem_len, d), w_mem_kv[0].astype(BF16), _pick(batch * mem_len, 256))
    y_mem = _mem_attn(p_attn, kv, batch, seq, mem_len, tq=_pick(seq, 512))

    wo = w_out[0].astype(BF16)
    r_w = jnp.concatenate([router_group[0], router_expert[0],
                           jnp.zeros((d, LANES - N_GROUPS - n_experts), F32)], axis=1)
    r_b = jnp.concatenate([router_group_b[0], router_expert_b[0],
                           jnp.zeros((LANES - N_GROUPS - n_experts,), F32)]).reshape(1, LANES)
    r_hi = r_w.astype(BF16)
    r_w2 = jnp.concatenate([r_hi, (r_w - r_hi.astype(F32)).astype(BF16)], axis=1)
    h1, logits = _outproj(y_rwkv, y_sb, y_mem, h, wo[:c], wo[c:c + sb_w], wo[c + sb_w:],
                          row(ln1_g[0]), row(ln1_b[0]), r_w2, r_b, _pick(m, 512))

    ids_t, wts = _route(logits, N_GROUPS, n_experts // N_GROUPS, _pick(m, 512))
    slot_asg, n_used, block_expert, n_valid, n_blocks = _dispatch_plan(ids_t, n_experts, MOE_ROWS)
    ys = _moe_ffn(slot_asg, block_expert, n_valid, n_used, h1, w_e_gate[0], w_e_up[0], w_e_down[0],
                  n_blocks, MOE_ROWS, tf=512)
    out = _combine(h1, ys, wts, row(ln2_g[0]), row(ln2_b[0]), _pick(m, 256))
    return out.reshape(batch, seq, d)
```

```python
import functools

import jax
import jax.numpy as jnp
from jax import lax
from jax.experimental import pallas as pl
from jax.experimental.pallas import tpu as pltpu

F32 = jnp.float32
BF16 = jnp.bfloat16

SB_HEADS = 4
MEM_HEADS = 4
N_GROUPS = 8
DEPTH = 1
DEEPNORM_ALPHA = (2.0 * DEPTH) ** 0.25
LN_EPS = 1e-5
GN_EPS = 64e-5

LANES = 128
WKV_CHUNK = 64
MOE_ROWS = 512
MOE_SUB = 128
ROW_GROUP = 8
SB_BLOCKS_PER_ITER = 3
VMEM_LIMIT = 56 * 1024 * 1024


def _cparams(sem):
    return pltpu.CompilerParams(dimension_semantics=sem, vmem_limit_bytes=VMEM_LIMIT)


def _layer_norm(x, g, b):
    mu = jnp.mean(x, axis=-1, keepdims=True)
    xc = x - mu
    var = jnp.mean(xc * xc, axis=-1, keepdims=True)
    return xc * lax.rsqrt(var + LN_EPS) * g + b


def _split3(x):
    hi = x.astype(BF16)
    r1 = x - hi.astype(F32)
    mid = r1.astype(BF16)
    lo = (r1 - mid.astype(F32)).astype(BF16)
    return hi, mid, lo


def _dot(a, b):
    return jnp.dot(a, b, preferred_element_type=F32)


def _dot_nt(a, b):
    return lax.dot_general(a, b, (((1,), (1,)), ((), ())), preferred_element_type=F32)


def _dot_tn(a, b):
    return lax.dot_general(a, b, (((0,), (0,)), ((), ())), preferred_element_type=F32)


def _dot_f32_by_exact(x, m):
    hi, mid, lo = _split3(x)
    return _dot(hi, m) + _dot(mid, m) + _dot(lo, m)


def _dot_exact_by_f32(m, x):
    hi, mid, lo = _split3(x)
    return _dot(m, hi) + _dot(m, mid) + _dot(m, lo)


def _dot_hp(a, b):
    return jnp.dot(a, b, preferred_element_type=F32, precision=lax.Precision.HIGHEST)


def _ln_kernel(x_ref, g_ref, b_ref, hb_ref):
    hb_ref[...] = _layer_norm(x_ref[...], g_ref[...], b_ref[...]).astype(BF16)


def _ln_in(x2, g, b, tm):
    m, d = x2.shape
    rows = pl.BlockSpec((tm, d), lambda i: (i, 0))
    vec = pl.BlockSpec((1, d), lambda i: (0, 0))
    return pl.pallas_call(
        _ln_kernel,
        grid=(m // tm,),
        in_specs=[rows, vec, vec],
        out_specs=rows,
        out_shape=jax.ShapeDtypeStruct((m, d), BF16),
        compiler_params=_cparams(("parallel",)),
        name="ln_in",
    )(x2, g, b)


def _inproj_kernel(hb_ref, wr_ref, wa_ref, pr_ref, pa_ref, *, n_f32_tiles):
    n = pl.program_id(1)

    @pl.when(n < n_f32_tiles)
    def _():
        pr_ref[...] = _dot(hb_ref[...], wr_ref[...])

    @pl.when(n >= n_f32_tiles)
    def _():
        pa_ref[...] = _dot(hb_ref[...], wa_ref[...]).astype(BF16)


def _inproj(hb, w_rwkv, w_attn, tm, tn):
    m, d = hb.shape
    n_rwkv_cols, n_attn_cols = w_rwkv.shape[1], w_attn.shape[1]
    nf = n_rwkv_cols // tn
    kern = functools.partial(_inproj_kernel, n_f32_tiles=nf)
    first = lambda i, n: (0, jnp.minimum(n, nf - 1))
    second = lambda i, n: (0, jnp.maximum(n - nf, 0))
    return pl.pallas_call(
        kern,
        grid=(m // tm, (n_rwkv_cols + n_attn_cols) // tn),
        in_specs=[
            pl.BlockSpec((tm, d), lambda i, n: (i, 0)),
            pl.BlockSpec((d, tn), first),
            pl.BlockSpec((d, tn), second),
        ],
        out_specs=[
            pl.BlockSpec((tm, tn), lambda i, n: (i, jnp.minimum(n, nf - 1))),
            pl.BlockSpec((tm, tn), lambda i, n: (i, jnp.maximum(n - nf, 0))),
        ],
        out_shape=[
            jax.ShapeDtypeStruct((m, n_rwkv_cols), F32),
            jax.ShapeDtypeStruct((m, n_attn_cols), BF16),
        ],
        compiler_params=_cparams(("parallel", "arbitrary")),
        name="inproj",
    )(hb, w_rwkv, w_attn)


def _wkv_kernel(pr_ref, pk_ref, pv_ref, pl_ref,
                mur_ref, muk_ref, muv_ref, mul_ref,
                w0_ref, a0_ref, kk_ref, ka_ref, rk_ref, lg_ref, lb_ref,
                wd_ref, wa_ref, wg_ref,
                y_ref,
                s_ref, cr_ref, ck_ref, cv_ref, cl_ref,
                r_s, lw_s, k_s, v_s, a_s, b_s, y_s, g_s, bo_s, zero_ref, ring64, ring128, ringg,
                *, n_pairs, tt):
    C = WKV_CHUNK
    t_idx = pl.program_id(2)

    @pl.when(t_idx == 0)
    def _():
        s_ref[...] = jnp.zeros_like(s_ref)
        cr_ref[...] = jnp.zeros_like(cr_ref)
        ck_ref[...] = jnp.zeros_like(ck_ref)
        cv_ref[...] = jnp.zeros_like(cv_ref)
        cl_ref[...] = jnp.zeros_like(cl_ref)

    row = lax.broadcasted_iota(jnp.int32, (tt, 1), 0)

    def shifted(p_ref, carry_ref, mu_ref):
        p = p_ref[...]
        prev = pltpu.roll(p, shift=1, axis=0)
        prev = jnp.where(row == 0, carry_ref[...], prev)
        carry_ref[...] = p[tt - 1:tt, :]
        return p + (prev - p) * mu_ref[...]

    lane = lax.broadcasted_iota(jnp.int32, (LANES, LANES), 1)
    sub = lax.broadcasted_iota(jnp.int32, (LANES, LANES), 0)
    head_ones = jnp.where((lane // 64) == (sub // 64), 1.0, 0.0).astype(BF16)

    def head_sum(x):
        hi = x.astype(BF16)
        lo = (x - hi.astype(F32)).astype(BF16)
        return _dot(hi, head_ones) + _dot(lo, head_ones)

    low = shifted(pl_ref, cl_ref, mul_ref)
    da = low[:, 0:LANES]
    th = jnp.tanh(da).astype(BF16)
    sg = jax.nn.sigmoid(low[:, LANES:]).astype(BF16)
    da = da.astype(BF16)
    r = shifted(pr_ref, cr_ref, mur_ref)
    k = shifted(pk_ref, ck_ref, muk_ref)
    v = shifted(pv_ref, cv_ref, muv_ref)
    for g in range(n_pairs):
        cs = slice(g * LANES, (g + 1) * LANES)
        rg, kg, vg = r[:, cs], k[:, cs], v[:, cs]
        pre = w0_ref[:, cs] + _dot(th, wd_ref[:, cs])
        softplus_neg = jnp.maximum(-pre, 0.0) + jnp.log(1.0 + jnp.exp(-jnp.abs(pre)))
        w_log = -softplus_neg - 0.5
        lw = -jnp.exp(w_log)
        a = jax.nn.sigmoid(a0_ref[:, cs] + _dot(da, wa_ref[:, cs]))
        gate = _dot(sg, wg_ref[:, cs])
        kk = kg * kk_ref[:, cs]
        kk = kk * lax.rsqrt(jnp.maximum(head_sum(kk * kk), 1e-24))
        k2 = kg * (1.0 + (a - 1.0) * ka_ref[:, cs])
        bonus = head_sum(rg * k2 * rk_ref[:, cs]) * vg
        r_s[:, cs] = rg
        lw_s[:, cs] = lw
        k_s[:, cs] = k2
        v_s[:, cs] = vg
        a_s[:, cs] = -kk
        b_s[:, cs] = kk * a
        g_s[:, cs] = gate
        bo_s[:, cs] = bonus

    ci = lax.broadcasted_iota(jnp.int32, (C, 2 * C), 0)
    cj = lax.broadcasted_iota(jnp.int32, (C, 2 * C), 1)
    left = cj < C
    strict = (cj % C) < ci
    incl = (cj % C) <= ci
    tri_incl = jnp.where(lax.broadcasted_iota(jnp.int32, (C, C), 1)
                         <= lax.broadcasted_iota(jnp.int32, (C, C), 0), 1.0, 0.0).astype(BF16)
    lane_c = lax.broadcasted_iota(jnp.int32, (C, LANES), 1)
    m0 = lane_c < 64
    eye = jnp.where(lane == sub, 1.0, 0.0).astype(F32)
    blockdiag = (lane // 64) == (sub // 64)

    csl = [slice(g * LANES, (g + 1) * LANES) for g in range(n_pairs)]
    P = range(n_pairs)
    cat0 = lambda *xs: jnp.concatenate(xs, axis=0)
    cat1 = lambda *xs: jnp.concatenate(xs, axis=1)
    bf = lambda x: x.astype(BF16)

    n_chunks = tt // C
    MM0, MM1, RTB, VB = range(4)
    BK, AK, ATB = range(3)
    ring64[...] = jnp.zeros_like(ring64)
    ring128[...] = jnp.zeros_like(ring128)
    ringg[...] = jnp.zeros_like(ringg)

    def stage1(c, out):
        rows = pl.ds(pl.multiple_of(c * C, C), C)
        slot = c % 4
        ld = lambda ref: [ref[rows, csl[g]] for g in P]
        rc, lwc, kc, vc, ac, bc = ld(r_s), ld(lw_s), ld(k_s), ld(v_s), ld(a_s), ld(b_s)
        cum = [_dot_exact_by_f32(tri_incl, lwc[g]) for g in P]
        yield
        last = [cum[g][C - 1:C, :] for g in P]
        rt = [rc[g] * jnp.exp(cum[g]) for g in P]
        at = [ac[g] * jnp.exp(cum[g] - lwc[g]) for g in P]
        ginv = [jnp.exp(-cum[g]) for g in P]
        btb = [bf(bc[g] * ginv[g]) for g in P]
        ktb = [bf(kc[g] * ginv[g]) for g in P]
        ghat = [jnp.exp(last[g] - cum[g]) for g in P]
        lhs0 = [bf(cat0(jnp.where(m0, at[g], 0.0), jnp.where(m0, rt[g], 0.0))) for g in P]
        lhs1 = [bf(cat0(jnp.where(m0, 0.0, at[g]), jnp.where(m0, 0.0, rt[g]))) for g in P]
        for g in P:
            ring128[slot, g, BK] = cat0(bf(bc[g] * ghat[g]), bf(kc[g] * ghat[g]))
            ring128[slot, g, ATB] = cat0(lhs0[g][:C], lhs1[g][:C])
            ring64[slot, g, VB] = bf(vc[g])
            ring64[slot, g, RTB] = bf(rt[g])
            ringg[slot, g, 0:1, :] = jnp.exp(last[g])
        x0 = [_dot_nt(lhs0[g], cat0(btb[g], ktb[g])) for g in P]
        x1 = [_dot_nt(lhs1[g], cat0(ktb[g], btb[g])) for g in P]
        yield
        n_bd = [cat0(jnp.where(left & strict, x0[g][:C], 0.0),
                     jnp.where((~left) & strict, x1[g][:C], 0.0)) for g in P]
        for g in P:
            ring128[slot, g, AK] = bf(cat0(jnp.where((~left) & strict, x0[g][:C], 0.0),
                                           jnp.where(left & strict, x1[g][:C], 0.0)))
            ring64[slot, g, MM0] = bf(jnp.where(incl, x0[g][C:], 0.0))
            ring64[slot, g, MM1] = bf(jnp.where(incl, x1[g][C:], 0.0))
        out["t"] = [eye + n_bd[g] for g in P]
        nb = [bf(n_bd[g]) for g in P]
        out["pw"] = [bf(_dot(nb[g], nb[g])) for g in P]

    def inverse_level(t, pw):
        res = [_dot(pw[g], cat1(pw[g], bf(t[g]))) for g in P]
        return [t[g] + res[g][:, LANES:] for g in P], [bf(res[g][:, :LANES]) for g in P]

    def stage2(inp, out):
        t, pw = inp["t"], inp["pw"]
        for level in range(3):
            t, pw = inverse_level(t, pw)
            if level < 2:
                yield
        out["t"], out["pw"] = t, pw

    def stage3(inp, c, out):
        slot = c % 4
        t, pw = inverse_level(inp["t"], inp["pw"])
        vb = [ring64[slot, g, VB] for g in P]
        av = [_dot(ring128[slot, g, AK], cat0(vb[g], vb[g])) for g in P]
        yield
        t = [t[g] + _dot(pw[g], bf(t[g])) for g in P]
        av = [cat0(jnp.where(m0, av[g][:C], 0.0), jnp.where(m0, 0.0, av[g][C:])) for g in P]
        yield
        wu = [_dot(bf(t[g]), cat1(ring128[slot, g, ATB], bf(av[g]))) for g in P]
        out["w"] = [bf(wu[g][:C, :LANES] + wu[g][C:, :LANES]) for g in P]
        out["u0"] = [wu[g][:C, LANES:] + wu[g][C:, LANES:] for g in P]

    def stage4(inp, c, valid):
        rows = pl.ds(pl.multiple_of(c * C, C), C)
        slot = c % 4
        vb = [ring64[slot, g, VB] for g in P]
        s = [s_ref[g] for g in P]
        sb = [bf(s[g]) for g in P]
        ub = [bf(_dot_nt(inp["w"][g], sb[g]) + inp["u0"][g]) for g in P]
        ys = [_dot_nt(ring64[slot, g, RTB], sb[g]) for g in P]
        yield
        uv = [cat0(ub[g], vb[g]) for g in P]
        y = [ys[g] + jnp.where(m0, _dot(ring64[slot, g, MM0], uv[g]),
                               _dot(ring64[slot, g, MM1], cat0(vb[g], ub[g]))) for g in P]
        upd = [_dot_tn(uv[g], ring128[slot, g, BK]) for g in P]
        yield
        for g in P:
            s_new = s[g] * ringg[slot, g, 0:1, :] + jnp.where(blockdiag, upd[g], 0.0)
            s_ref[g] = jnp.where(valid, s_new, s[g])
            y_s[rows, csl[g]] = y[g]

    def interleave(gens):
        live = list(gens)
        while live:
            still = []
            for gen in live:
                try:
                    next(gen)
                    still.append(gen)
                except StopIteration:
                    pass
            live = still

    def body(it, carry):
        o1, o2, o3 = carry
        n1, n2, n3 = {}, {}, {}
        interleave([stage4(o3, jnp.maximum(it - 3, 0), it >= 3),
                    stage3(o2, jnp.maximum(it - 2, 0), n3),
                    stage2(o1, n2),
                    stage1(jnp.minimum(it, n_chunks - 1), n1)])
        return n1, n2, n3

    zero_ref[...] = jnp.zeros_like(zero_ref)
    zb = lambda r: [zero_ref[0:r, :].astype(BF16) for _ in P]
    zf = lambda r: [zero_ref[0:r, :] for _ in P]
    front = lambda: dict(t=zf(2 * C), pw=zb(2 * C))
    lax.fori_loop(0, n_chunks + 3, body, (front(), front(), dict(w=zb(C), u0=zf(C))))

    for g in range(n_pairs):
        cs = slice(g * LANES, (g + 1) * LANES)
        y = y_s[:, cs]
        mean = head_sum(y) * (1.0 / 64.0)
        yc = y - mean
        var = head_sum(yc * yc) * (1.0 / 64.0)
        yn = yc * lax.rsqrt(var + GN_EPS) * lg_ref[:, cs] + lb_ref[:, cs]
        y_ref[:, cs] = ((yn + bo_s[:, cs]) * g_s[:, cs]).astype(BF16)


def _wkv(p_rwkv, prm, batch, seq, n_pairs, tt):
    m = p_rwkv.shape[0]
    c = prm["w0"].shape[1]
    gw = n_pairs * LANES
    n_col_blocks = c // gw
    nt = seq // tt
    low_w = prm["mu_low"].shape[1]

    def pspec(off):
        return pl.BlockSpec((tt, gw), lambda b, g, t: (b * nt + t, off * n_col_blocks + g))

    def vspec():
        return pl.BlockSpec((1, gw), lambda b, g, t: (0, g))

    def wspec(rows):
        return pl.BlockSpec((rows, gw), lambda b, g, t: (0, g))

    kern = functools.partial(_wkv_kernel, n_pairs=n_pairs, tt=tt)
    tile = pltpu.VMEM((tt, gw), F32)
    return pl.pallas_call(
        kern,
        grid=(batch, n_col_blocks, nt),
        in_specs=[
            pspec(0), pspec(1), pspec(2),
            pl.BlockSpec((tt, low_w), lambda b, g, t: (b * nt + t, (3 * c) // low_w)),
            vspec(), vspec(), vspec(),
            pl.BlockSpec((1, low_w), lambda b, g, t: (0, 0)),
            vspec(), vspec(), vspec(), vspec(), vspec(), vspec(), vspec(),
            wspec(LANES), wspec(LANES), wspec(low_w - LANES),
        ],
        out_specs=pl.BlockSpec((tt, gw), lambda b, g, t: (b * nt + t, g)),
        out_shape=jax.ShapeDtypeStruct((m, c), BF16),
        scratch_shapes=[
            pltpu.VMEM((n_pairs, LANES, LANES), F32),
            pltpu.VMEM((1, gw), F32), pltpu.VMEM((1, gw), F32), pltpu.VMEM((1, gw), F32),
            pltpu.VMEM((1, low_w), F32),
            tile, tile, tile, tile, tile, tile, tile, tile, tile,
            pltpu.VMEM((LANES, LANES), F32),
            pltpu.VMEM((4, n_pairs, 4, WKV_CHUNK, LANES), BF16),
            pltpu.VMEM((4, n_pairs, 3, 2 * WKV_CHUNK, LANES), BF16),
            pltpu.VMEM((4, n_pairs, 8, LANES), F32),
        ],
        compiler_params=_cparams(("parallel", "parallel", "arbitrary")),
        name="wkv7",
    )(p_rwkv, p_rwkv, p_rwkv, p_rwkv,
      prm["mu_r"], prm["mu_k"], prm["mu_v"], prm["mu_low"],
      prm["w0"], prm["a0"], prm["k_k"], prm["k_a"], prm["r_k"], prm["lnx_g"], prm["lnx_b"],
      prm["wd"], prm["wa"], prm["wg"])


def _sb_kernel(q_ref, k_ref, v_ref, o_ref, acc_ref, right_ref, *, tq, scale):
    i = pl.program_id(1)
    row = lax.broadcasted_iota(jnp.int32, (tq, tq), 0)
    col = lax.broadcasted_iota(jnp.int32, (tq, tq), 1)
    r2 = lax.broadcasted_iota(jnp.int32, (tq, 2 * tq), 0)
    c2 = lax.broadcasted_iota(jnp.int32, (tq, 2 * tq), 1)
    after_and_total = jnp.where((c2 >= tq) | (r2 > c2), 1.0, 0.0).astype(BF16)
    diag = col < row

    heads = range(SB_HEADS)
    hsl = [slice(h * LANES, (h + 1) * LANES) for h in heads]

    def blocks(js, first):
        nb = len(js)
        it = [(b, h) for b in range(nb) for h in heads]
        ks = [pl.ds(pl.multiple_of(j * tq, tq), tq) for j in js]
        z = {p: _dot_nt(q_ref[:, hsl[p[1]]], k_ref[ks[p[0]], hsl[p[1]]]) * scale for p in it}
        sp = {p: jnp.maximum(z[p], 0.0) + jnp.log(1.0 + jnp.exp(-jnp.abs(z[p]))) for p in it}
        log_keep = {p: jnp.where(diag, -sp[p], 0.0) if first else -sp[p] for p in it}
        hi = {p: log_keep[p].astype(BF16) for p in it}
        lo = {p: (log_keep[p] - hi[p].astype(F32)).astype(BF16) for p in it}
        sums = {p: _dot(hi[p], after_and_total) + _dot(lo[p], after_and_total) for p in it}
        right = {}
        for h in heads:
            run = None if first else right_ref[h]
            for b in range(nb):
                right[(b, h)] = run
                tot = sums[(b, h)][:, tq:]
                run = tot if run is None else run + tot
            right[("end", h)] = run
        after = {p: sums[p][:, :tq] if right[p] is None else sums[p][:, :tq] + right[p] for p in it}
        attn = {p: jnp.exp(z[p] - sp[p] + after[p]) for p in it}
        if first:
            attn = {p: jnp.where(diag, attn[p], 0.0) for p in it}
        pv = {p: _dot(attn[p].astype(BF16), v_ref[ks[p[0]], hsl[p[1]]]) for p in it}
        for h in heads:
            tot = pv[(0, h)]
            for b in range(1, nb):
                tot = tot + pv[(b, h)]
            if first:
                acc_ref[h] = tot
            else:
                acc_ref[h] += tot
            right_ref[h] = right[("end", h)]

    blocks([i], True)

    def body(jj, _):
        j = i - 1 - SB_BLOCKS_PER_ITER * jj
        blocks([j - b for b in range(SB_BLOCKS_PER_ITER)], False)
        return 0

    lax.fori_loop(0, i // SB_BLOCKS_PER_ITER, body, 0)

    for rem in range(1, SB_BLOCKS_PER_ITER):
        @pl.when(i % SB_BLOCKS_PER_ITER == rem)
        def _(rem=rem):
            blocks(list(range(rem - 1, -1, -1)), False)

    for h in heads:
        o_ref[:, hsl[h]] = acc_ref[h].astype(BF16)


def _sb_attn(p_attn, batch, seq, tq):
    m = p_attn.shape[0]
    w = SB_HEADS * LANES
    nq = seq // tq
    kern = functools.partial(_sb_kernel, tq=tq, scale=LANES ** -0.5)
    return pl.pallas_call(
        kern,
        grid=(batch, nq),
        in_specs=[
            pl.BlockSpec((tq, w), lambda b, i: (b * nq + i, 0)),
            pl.BlockSpec((seq, w), lambda b, i: (b, 1)),
            pl.BlockSpec((seq, w), lambda b, i: (b, 2)),
        ],
        out_specs=pl.BlockSpec((tq, w), lambda b, i: (b * nq + i, 0)),
        out_shape=jax.ShapeDtypeStruct((m, w), BF16),
        scratch_shapes=[pltpu.VMEM((SB_HEADS, tq, tq), F32), pltpu.VMEM((SB_HEADS, tq, tq), F32)],
        compiler_params=_cparams(("parallel", "arbitrary")),
        name="sb_attn",
    )(p_attn, p_attn, p_attn)


def _mem_kv_kernel(m_ref, w_ref, o_ref):
    o_ref[...] = _dot(m_ref[...].astype(BF16), w_ref[...]).astype(BF16)


def _mem_kv(mem2, w_bf16, tm):
    m, d = mem2.shape
    n = w_bf16.shape[1]
    return pl.pallas_call(
        _mem_kv_kernel,
        grid=(m // tm,),
        in_specs=[pl.BlockSpec((tm, d), lambda i: (i, 0)),
                  pl.BlockSpec((d, n), lambda i: (0, 0))],
        out_specs=pl.BlockSpec((tm, n), lambda i: (i, 0)),
        out_shape=jax.ShapeDtypeStruct((m, n), BF16),
        compiler_params=_cparams(("parallel",)),
        name="mem_kv",
    )(mem2, w_bf16)


def _mem_attn_kernel(q_ref, k_ref, v_ref, o_ref, *, scale):
    for h in range(MEM_HEADS):
        hs = slice(h * LANES, (h + 1) * LANES)
        s = _dot_nt(q_ref[:, hs], k_ref[:, hs]) * scale
        s = s - jnp.max(s, axis=-1, keepdims=True)
        e = jnp.exp(s)
        p = e / jnp.sum(e, axis=-1, keepdims=True)
        o_ref[:, hs] = _dot(p.astype(BF16), v_ref[:, hs]).astype(BF16)


def _mem_attn(p_attn, kv, batch, seq, mem_len, tq):
    m = p_attn.shape[0]
    w = MEM_HEADS * LANES
    nq = seq // tq
    kern = functools.partial(_mem_attn_kernel, scale=LANES ** -0.5)
    return pl.pallas_call(
        kern,
        grid=(batch, nq),
        in_specs=[
            pl.BlockSpec((tq, w), lambda b, i: (b * nq + i, 3)),
            pl.BlockSpec((mem_len, w), lambda b, i: (b, 0)),
            pl.BlockSpec((mem_len, w), lambda b, i: (b, 1)),
        ],
        out_specs=pl.BlockSpec((tq, w), lambda b, i: (b * nq + i, 0)),
        out_shape=jax.ShapeDtypeStruct((m, w), BF16),
        compiler_params=_cparams(("parallel", "parallel")),
        name="mem_attn",
    )(p_attn, kv, kv)


def _outproj_kernel(yr_ref, ys_ref, ym_ref, x_ref, g0_ref, b0_ref, w_ref, g_ref, b_ref, rw_ref, rb_ref,
                    h1_ref, lg_ref):
    c0 = yr_ref.shape[1]
    c1 = c0 + ys_ref.shape[1]
    mix = (_dot(yr_ref[...], w_ref[0:c0, :]) + _dot(ys_ref[...], w_ref[c0:c1, :])
           + _dot(ym_ref[...], w_ref[c1:, :]))
    h = _layer_norm(x_ref[...], g0_ref[...], b0_ref[...])
    h1 = _layer_norm(DEEPNORM_ALPHA * h + mix, g_ref[...], b_ref[...])
    h1_ref[...] = h1
    hi = h1.astype(BF16)
    lo = (h1 - hi.astype(F32)).astype(BF16)
    both = _dot(hi, rw_ref[...])
    lg_ref[...] = both[:, :LANES] + both[:, LANES:] + _dot(lo, rw_ref[:, :LANES]) + rb_ref[...]


def _outproj(y_r, y_s, y_m, x2, g0, b0, w, g, b, r_w, r_b, tm):
    m, d = x2.shape
    full = lambda a: pl.BlockSpec(a.shape, lambda i: (0, 0))
    rows = lambda a: pl.BlockSpec((tm, a.shape[1]), lambda i: (i, 0))
    return pl.pallas_call(
        _outproj_kernel,
        grid=(m // tm,),
        in_specs=[rows(y_r), rows(y_s), rows(y_m), rows(x2), full(g0), full(b0), full(w), full(g), full(b),
                  full(r_w), full(r_b)],
        out_specs=[pl.BlockSpec((tm, d), lambda i: (i, 0)),
                   pl.BlockSpec((tm, LANES), lambda i: (i, 0))],
        out_shape=[jax.ShapeDtypeStruct((m, d), F32),
                   jax.ShapeDtypeStruct((m, LANES), F32)],
        compiler_params=_cparams(("parallel",)),
        name="outproj",
    )(y_r, y_s, y_m, x2, g0, b0, w, g, b, r_w, r_b)


def _route_kernel(lg_ref, id_ref, wt_ref, *, n_groups, per_group):
    lg = lg_ref[...]
    lane_i = lax.broadcasted_iota(jnp.int32, lg.shape, 1)
    lane = lane_i.astype(F32)
    neg = jnp.float32(-jnp.inf)
    big = jnp.float32(2 ** 20)

    def first_max(vals):
        mx = jnp.max(vals, axis=-1, keepdims=True)
        idx = jnp.min(jnp.where(vals == mx, lane, big), axis=-1, keepdims=True)
        return mx, idx

    is_group = lane < n_groups
    gmax, gidx = first_max(jnp.where(is_group, lg, neg))
    gsum = jnp.sum(jnp.where(is_group, jnp.exp(lg - gmax), 0.0), axis=-1, keepdims=True)
    group_w = 1.0 / gsum
    lo = n_groups + gidx * per_group
    in_group = (lane >= lo) & (lane < lo + per_group)
    v1, i1 = first_max(jnp.where(in_group, lg, neg))
    v2, i2 = first_max(jnp.where(in_group & (lane != i1), lg, neg))
    e2 = jnp.exp(v2 - v1)
    w1 = group_w / (1.0 + e2)
    w2 = group_w * e2 / (1.0 + e2)
    ids = jnp.where(lane_i == 0, i1 - n_groups, jnp.where(lane_i == 1, i2 - n_groups, 0.0))
    id_ref[...] = ids.T[0:8, :].astype(jnp.int32)
    wt_ref[...] = jnp.where(lane_i == 0, w1, jnp.where(lane_i == 1, w2, 0.0))


def _route(logits, n_groups, per_group, tm):
    m = logits.shape[0]
    kern = functools.partial(_route_kernel, n_groups=n_groups, per_group=per_group)
    spec = pl.BlockSpec((tm, LANES), lambda i: (i, 0))
    return pl.pallas_call(
        kern,
        grid=(m // tm,),
        in_specs=[spec],
        out_specs=[pl.BlockSpec((8, tm), lambda i: (0, i)), spec],
        out_shape=[jax.ShapeDtypeStruct((8, m), jnp.int32),
                   jax.ShapeDtypeStruct((m, LANES), F32)],
        compiler_params=_cparams(("parallel",)),
        name="route",
    )(logits)


def _plan_kernel(id_ref, pos_ref, meta_ref, cnt_ref, base_ref, start_ref, *, tb, rows, n_experts, nb):
    phase = pl.program_id(0)
    j = pl.program_id(1)
    sub = lax.broadcasted_iota(jnp.int32, (LANES, tb), 0)
    e1 = id_ref[0:1, :]
    e2 = id_ref[1:2, :]
    hit1 = jnp.where(sub == e1, 1.0, 0.0)
    hit2 = jnp.where(sub == e2, 1.0, 0.0)
    hits = hit1 + hit2
    per_expert = jnp.sum(hits, axis=1, keepdims=True)

    @pl.when((phase == 0) & (j == 0))
    def _():
        cnt_ref[...] = jnp.zeros_like(cnt_ref)

    @pl.when(phase == 0)
    def _():
        cnt_ref[...] += per_expert

    sq_r = lax.broadcasted_iota(jnp.int32, (LANES, LANES), 0)
    sq_c = lax.broadcasted_iota(jnp.int32, (LANES, LANES), 1)

    @pl.when((phase == 1) & (j == 0))
    def _():
        n_blk = jnp.floor((cnt_ref[...] + (rows - 1)) * (1.0 / rows))
        before = jnp.where(sq_c < sq_r, 1.0, 0.0).astype(BF16)
        blk_start = _dot(before, jnp.broadcast_to(n_blk, (LANES, LANES)).astype(BF16))
        start_ref[...] = blk_start[:, 0:1]
        base_ref[...] = jnp.zeros_like(base_ref)
        blk_end = blk_start + n_blk
        n_used = jnp.sum(jnp.where(sq_r[:, 0:1] < n_experts, n_blk, 0.0), axis=0, keepdims=True)
        owner = jnp.sum(jnp.where((sq_r < n_experts) & (blk_end <= sq_c.astype(F32)), 1.0, 0.0),
                        axis=0, keepdims=True)
        last_owner = jnp.max(jnp.where(n_blk > 0.0, sq_r[:, 0:1].astype(F32), 0.0), axis=0, keepdims=True)
        blk = sq_c[0:1, :].astype(F32)
        in_use = blk < n_used
        owner = jnp.where(in_use, owner, last_owner)
        mine = sq_r.astype(F32) == owner
        cnt_o = jnp.sum(jnp.where(mine, cnt_ref[...], 0.0), axis=0, keepdims=True)
        start_o = jnp.sum(jnp.where(mine, blk_start, 0.0), axis=0, keepdims=True)
        valid = jnp.clip(cnt_o - (blk - start_o) * rows, 0.0, float(rows))
        valid = jnp.where(in_use, valid, 0.0)
        row8 = lax.broadcasted_iota(jnp.int32, (8, LANES), 0)
        meta = jnp.where(row8 == 0, owner, jnp.where(row8 == 1, valid, jnp.where(row8 == 2, n_used, 0.0)))
        meta_ref[...] = meta.astype(jnp.int32)

    @pl.when(phase == 1)
    def _():
        tr = lax.broadcasted_iota(jnp.int32, (tb, tb), 0)
        tc = lax.broadcasted_iota(jnp.int32, (tb, tb), 1)
        earlier = jnp.where(tr < tc, 1.0, 0.0).astype(BF16)
        seen = _dot(hits.astype(BF16), earlier)
        slot = start_ref[...] * rows + base_ref[...] + seen
        p1 = jnp.sum(hit1 * slot, axis=0, keepdims=True)
        p2 = jnp.sum(hit2 * slot, axis=0, keepdims=True)
        row8 = lax.broadcasted_iota(jnp.int32, (8, tb), 0)
        pos_ref[...] = jnp.where(row8 == 0, p1, jnp.where(row8 == 1, p2, 0.0)).astype(jnp.int32)
        base_ref[...] += per_expert


def _plan(ids_t, n_experts, rows, tb):
    m = ids_t.shape[1]
    nb = m // tb
    kern = functools.partial(_plan_kernel, tb=tb, rows=rows, n_experts=n_experts, nb=nb)
    col = pltpu.VMEM((LANES, 1), F32)
    return pl.pallas_call(
        kern,
        grid=(2, nb),
        in_specs=[pl.BlockSpec((8, tb), lambda p, j: (0, j))],
        out_specs=[pl.BlockSpec((8, tb), lambda p, j: (0, j * p)),
                   pl.BlockSpec((8, LANES), lambda p, j: (0, 0))],
        out_shape=[jax.ShapeDtypeStruct((8, m), jnp.int32),
                   jax.ShapeDtypeStruct((8, LANES), jnp.int32)],
        scratch_shapes=[col, col, col],
        compiler_params=_cparams(("arbitrary", "arbitrary")),
        name="moe_plan",
    )(ids_t)


def _for_range(lo, hi, body):
    full = (hi - lo) // ROW_GROUP

    def group(g, _):
        for u in range(ROW_GROUP):
            body(lo + g * ROW_GROUP + u)
        return 0

    lax.fori_loop(0, full, group, 0)

    def one(r, _):
        body(r)
        return 0

    lax.fori_loop(lo + full * ROW_GROUP, hi, one, 0)


def _row_in(src_hbm, dst_vmem, sem, src_row, dst_row):
    return pltpu.make_async_copy(src_hbm.at[pl.ds(src_row, 1)], dst_vmem.at[pl.ds(dst_row, 1)], sem)


def _row_out(src_vmem, dst_hbm, sem, src_row, dst_row):
    return pltpu.make_async_copy(src_vmem.at[pl.ds(src_row, 1)], dst_hbm.at[pl.ds(dst_row, 1)], sem)


def _gather_cparams(sem):
    return pltpu.CompilerParams(dimension_semantics=sem, vmem_limit_bytes=VMEM_LIMIT,
                                disable_bounds_checks=True)


def _ffn_kernel(asg_ref, be_ref, nv_ref, nu_ref, h_hbm, wg_ref, wu_ref, wd_ref, y_hbm,
                xf_ref, xb_ref, acc_ref, sem_in, sem_out, *, rows, sub, nj, n_tok, n_blocks):
    i = pl.program_id(0)
    j = pl.program_id(1)
    n_used = nu_ref[0]
    used = i < n_used
    slot = i % 2

    def gather(block, lo, hi, buf):
        def body(r):
            a = asg_ref[block * rows + r]
            tok = jnp.where(a >= n_tok, a - n_tok, a)
            _row_in(h_hbm, xf_ref.at[buf], sem_in.at[buf], tok, r).start()

        _for_range(lo, hi, body)

    def gather_wait(block, buf):
        _for_range(0, nv_ref[block], lambda r: _row_in(h_hbm, xf_ref.at[buf], sem_in.at[buf], 0, r).wait())

    def emit(block, buf):
        def body(r):
            _row_out(acc_ref.at[buf], y_hbm, sem_out.at[buf], r, asg_ref[block * rows + r]).start()

        _for_range(0, nv_ref[block], body)

    def emit_wait(block, buf):
        _for_range(0, nv_ref[block], lambda r: _row_out(acc_ref.at[buf], y_hbm, sem_out.at[buf], r, 0).wait())

    @pl.when((i == 0) & (j == 0))
    def _():
        xf_ref[...] = jnp.zeros_like(xf_ref)
        gather(0, 0, nv_ref[0], 0)

    @pl.when(used & (j == 0))
    def _():
        gather_wait(i, slot)

    @pl.when(i + 1 < n_used)
    def _():
        nxt = nv_ref[i + 1]
        q = rows // nj
        gather(i + 1, jnp.minimum(j * q, nxt), jnp.minimum((j + 1) * q, nxt), 1 - slot)

    n_sub = (nv_ref[i] + sub - 1) // sub
    for n in range(1, rows // sub + 1):
        @pl.when(used & (n_sub == n))
        def _(n=n):
            r = n * sub

            @pl.when(j == 0)
            def _():
                xb_ref[0:r, :] = xf_ref[slot, 0:r, :].astype(BF16)
                acc_ref[slot, 0:r, :] = jnp.zeros((r, acc_ref.shape[2]), F32)

            xb = xb_ref[0:r, :]
            gate = _dot(xb, wg_ref[...].astype(BF16))
            up = _dot(xb, wu_ref[...].astype(BF16))
            hid = (gate * jax.nn.sigmoid(gate)) * up
            acc_ref[slot, 0:r, :] += _dot(hid.astype(BF16), wd_ref[...].astype(BF16))

    @pl.when((j == nj - 1) & (i >= 1) & (i <= n_used))
    def _():
        emit_wait(i - 1, 1 - slot)

    @pl.when((j == nj - 1) & used)
    def _():
        emit(i, slot)

    @pl.when((j == nj - 1) & used & (i == n_blocks - 1))
    def _():
        emit_wait(i, slot)


def _moe_ffn(slot_asg, block_expert, n_valid, n_used, h1, w_gate, w_up, w_down, n_blocks, rows, tf):
    n_tok, d = h1.shape
    de = w_gate.shape[2]
    nj = de // tf
    kern = functools.partial(_ffn_kernel, rows=rows, sub=MOE_SUB, nj=nj, n_tok=n_tok, n_blocks=n_blocks)

    def jidx(i, j, nu):
        return jnp.where(i < nu[0], j, nj - 1)

    return pl.pallas_call(
        kern,
        grid_spec=pltpu.PrefetchScalarGridSpec(
            num_scalar_prefetch=4,
            grid=(n_blocks, nj),
            in_specs=[
                pl.BlockSpec(memory_space=pl.ANY),
                pl.BlockSpec((None, d, tf), lambda i, j, asg, be, nv, nu: (be[i], 0, jidx(i, j, nu))),
                pl.BlockSpec((None, d, tf), lambda i, j, asg, be, nv, nu: (be[i], 0, jidx(i, j, nu))),
                pl.BlockSpec((None, tf, d), lambda i, j, asg, be, nv, nu: (be[i], jidx(i, j, nu), 0)),
            ],
            out_specs=pl.BlockSpec(memory_space=pl.ANY),
            scratch_shapes=[pltpu.VMEM((2, rows, d), F32),
                            pltpu.VMEM((rows, d), BF16),
                            pltpu.VMEM((2, rows, d), F32),
                            pltpu.SemaphoreType.DMA((2,)),
                            pltpu.SemaphoreType.DMA((2,))],
        ),
        out_shape=jax.ShapeDtypeStruct((2 * n_tok, d), F32),
        compiler_params=_gather_cparams(("arbitrary", "arbitrary")),
        name="moe_ffn",
    )(slot_asg, block_expert, n_valid, n_used, h1, w_gate, w_up, w_down)


def _combine_kernel(h_ref, y0_ref, y1_ref, wt_ref, g_ref, b_ref, o_ref):
    wt = wt_ref[...]
    ffn = y0_ref[...] * wt[:, 0:1] + y1_ref[...] * wt[:, 1:2]
    o_ref[...] = _layer_norm(DEEPNORM_ALPHA * h_ref[...] + ffn, g_ref[...], b_ref[...])


def _combine(h1, ys, wts, g, b, tm):
    m, d = h1.shape
    nt = m // tm
    rows = pl.BlockSpec((tm, d), lambda i: (i, 0))
    vec = pl.BlockSpec((1, d), lambda i: (0, 0))
    return pl.pallas_call(
        _combine_kernel,
        grid=(nt,),
        in_specs=[rows, rows, pl.BlockSpec((tm, d), lambda i: (nt + i, 0)),
                  pl.BlockSpec((tm, LANES), lambda i: (i, 0)), vec, vec],
        out_specs=rows,
        out_shape=jax.ShapeDtypeStruct((m, d), F32),
        compiler_params=_cparams(("parallel",)),
        name="moe_combine",
    )(h1, ys, ys, wts, g, b)


def _dispatch_plan(ids_t, n_experts, rows):
    m = ids_t.shape[1]
    n_blocks = -(-(2 * m) // rows) + n_experts
    assert n_blocks <= LANES
    pos_t, meta = _plan(ids_t, n_experts, rows, _pick(m, 512))
    pos = pos_t[:2].reshape(-1)
    slot_asg = jnp.zeros((n_blocks * rows,), jnp.int32).at[pos].set(jnp.arange(2 * m, dtype=jnp.int32))
    return slot_asg, meta[2, :1], meta[0, :n_blocks], meta[1, :n_blocks], n_blocks


def _pick(n, pref):
    t = min(pref, n)
    while n % t:
        t //= 2
    return t


def kernel(x, mem, ln_in_g, ln_in_b, w_in, tshift_mu, w0, w_decay_up, a0, w_a_up, w_g_up, k_k, k_a, r_k,
           lnx_g, lnx_b, w_mem_kv, w_out, ln1_g, ln1_b, router_group, router_group_b, router_expert,
           router_expert_b, w_e_gate, w_e_up, w_e_down, ln2_g, ln2_b):
    batch, seq, d = x.shape
    mem_len = mem.shape[1]
    m = batch * seq
    c = w0.shape[1]
    dr, ar, gr = w_decay_up.shape[1], w_a_up.shape[1], w_g_up.shape[1]
    rwkv_cols = 3 * c + dr + ar + gr
    sb_w = SB_HEADS * LANES
    mem_w = MEM_HEADS * LANES
    assert dr + ar == LANES and c % (2 * LANES) == 0 and w_in.shape[0] == DEPTH
    assert w_in.shape[2] == rwkv_cols + 3 * sb_w + mem_w
    n_experts = router_expert.shape[2]
    row = lambda a: a.reshape(1, -1)

    tn = 512
    low_w = -(-(dr + ar + gr) // LANES) * LANES
    rw_pad = -(-(3 * c + low_w) // tn) * tn
    wi = w_in[0]
    w_rwkv = jnp.pad(wi[:, :rwkv_cols].astype(BF16), ((0, 0), (0, rw_pad - rwkv_cols)))
    w_attn = wi[:, rwkv_cols:].astype(BF16)
    mu = tshift_mu[0]
    lp = low_w - (dr + ar + gr)
    prm = dict(
        mu_r=row(mu[:c]), mu_k=row(mu[c:2 * c]), mu_v=row(mu[2 * c:3 * c]),
        mu_low=row(jnp.concatenate([mu[3 * c:rwkv_cols], jnp.zeros((lp,), F32)])),
        w0=row(w0[0]), a0=row(a0[0]), k_k=row(k_k[0]), k_a=row(k_a[0]), r_k=row(r_k[0]),
        lnx_g=row(lnx_g[0]), lnx_b=row(lnx_b[0]),
        wd=jnp.concatenate([w_decay_up[0], jnp.zeros((ar, c), F32)], axis=0).astype(BF16),
        wa=jnp.concatenate([jnp.zeros((dr, c), F32), w_a_up[0]], axis=0).astype(BF16),
        wg=jnp.concatenate([w_g_up[0], jnp.zeros((lp, c), F32)], axis=0).astype(BF16),
    )
    assert (3 * c) % low_w == 0

    x2 = x.reshape(m, d)
    hb = _ln_in(x2, row(ln_in_g), row(ln_in_b), _pick(m, 256))
    p_rwkv, p_attn = _inproj(hb, w_rwkv, w_attn, _pick(m, 1024), tn)

    y_rwkv = _wkv(p_rwkv, prm, batch, seq, n_pairs=4, tt=_pick(seq, 1024))
    y_sb = _sb_attn(p_attn, batch, seq, tq=LANES)
    kv = _mem_kv(mem.reshape(batch * mem_len, d), w_mem_kv[0].astype(BF16), _pick(batch * mem_len, 256))
    y_mem = _mem_attn(p_attn, kv, batch, seq, mem_len, tq=_pick(seq, 512))

    wo = w_out[0].astype(BF16)
    r_w = jnp.concatenate([router_group[0], router_expert[0],
                           jnp.zeros((d, LANES - N_GROUPS - n_experts), F32)], axis=1)
    r_b = jnp.concatenate([router_group_b[0], router_expert_b[0],
                           jnp.zeros((LANES - N_GROUPS - n_experts,), F32)]).reshape(1, LANES)
    r_hi = r_w.astype(BF16)
    r_w2 = jnp.concatenate([r_hi, (r_w - r_hi.astype(F32)).astype(BF16)], axis=1)
    h1, logits = _outproj(y_rwkv, y_sb, y_mem, x2, row(ln_in_g), row(ln_in_b), wo,
                          row(ln1_g[0]), row(ln1_b[0]), r_w2, r_b, _pick(m, 512))

    ids_t, wts = _route(logits, N_GROUPS, n_experts // N_GROUPS, _pick(m, 512))
    slot_asg, n_used, block_expert, n_valid, n_blocks = _dispatch_plan(ids_t, n_experts, MOE_ROWS)
    ys = _moe_ffn(slot_asg, block_expert, n_valid, n_used, h1, w_e_gate[0], w_e_up[0], w_e_down[0],
                  n_blocks, MOE_ROWS, tf=512)
    out = _combine(h1, ys, wts, row(ln2_g[0]), row(ln2_b[0]), _pick(m, 256))
    return out.reshape(batch, seq, d)
```

```python
import functools

import jax
import jax.numpy as jnp
from jax import lax
from jax.experimental import pallas as pl
from jax.experimental.pallas import tpu as pltpu

F32 = jnp.float32
BF16 = jnp.bfloat16

SB_HEADS = 4
MEM_HEADS = 4
N_GROUPS = 8
DEPTH = 1
DEEPNORM_ALPHA = (2.0 * DEPTH) ** 0.25
LN_EPS = 1e-5
GN_EPS = 64e-5

LANES = 128
WKV_CHUNK = 64
MOE_ROWS = 512
MOE_SUB = 64
ROW_GROUP = 8
SB_BLOCKS_PER_ITER = 3
VMEM_LIMIT = 56 * 1024 * 1024


def _cparams(sem):
    return pltpu.CompilerParams(dimension_semantics=sem, vmem_limit_bytes=VMEM_LIMIT)


def _layer_norm(x, g, b):
    mu = jnp.mean(x, axis=-1, keepdims=True)
    xc = x - mu
    var = jnp.mean(xc * xc, axis=-1, keepdims=True)
    return xc * lax.rsqrt(var + LN_EPS) * g + b


def _split3(x):
    hi = x.astype(BF16)
    r1 = x - hi.astype(F32)
    mid = r1.astype(BF16)
    lo = (r1 - mid.astype(F32)).astype(BF16)
    return hi, mid, lo


def _dot(a, b):
    return jnp.dot(a, b, preferred_element_type=F32)


def _dot_nt(a, b):
    return lax.dot_general(a, b, (((1,), (1,)), ((), ())), preferred_element_type=F32)


def _dot_tn(a, b):
    return lax.dot_general(a, b, (((0,), (0,)), ((), ())), preferred_element_type=F32)


def _dot_f32_by_exact(x, m):
    hi, mid, lo = _split3(x)
    return _dot(hi, m) + _dot(mid, m) + _dot(lo, m)


def _dot_exact_by_f32(m, x):
    hi, mid, lo = _split3(x)
    return _dot(m, hi) + _dot(m, mid) + _dot(m, lo)


def _dot_hp(a, b):
    return jnp.dot(a, b, preferred_element_type=F32, precision=lax.Precision.HIGHEST)


def _ln_kernel(x_ref, g_ref, b_ref, hb_ref):
    hb_ref[...] = _layer_norm(x_ref[...], g_ref[...], b_ref[...]).astype(BF16)


def _ln_in(x2, g, b, tm):
    m, d = x2.shape
    rows = pl.BlockSpec((tm, d), lambda i: (i, 0))
    vec = pl.BlockSpec((1, d), lambda i: (0, 0))
    return pl.pallas_call(
        _ln_kernel,
        grid=(m // tm,),
        in_specs=[rows, vec, vec],
        out_specs=rows,
        out_shape=jax.ShapeDtypeStruct((m, d), BF16),
        compiler_params=_cparams(("parallel",)),
        name="ln_in",
    )(x2, g, b)


def _inproj_kernel(hb_ref, wr_ref, wa_ref, mu_ref, pr_ref, pa_ref, last_ref, *, n_f32_tiles, tiles_per_seq):
    i = pl.program_id(0)
    n = pl.program_id(1)

    @pl.when((i == 0) & (n == 0))
    def _():
        last_ref[...] = jnp.zeros_like(last_ref)

    @pl.when(n < n_f32_tiles)
    def _():
        tm = pr_ref.shape[0]
        p = _dot(hb_ref[...], wr_ref[...])
        slot = jnp.minimum(n, n_f32_tiles - 1)
        carried = jnp.where(i % tiles_per_seq == 0, 0.0, last_ref[slot])
        prev = pltpu.roll(p, shift=1, axis=0)
        prev = jnp.where(lax.broadcasted_iota(jnp.int32, (tm, 1), 0) == 0, carried, prev)
        last_ref[slot] = p[tm - 1:tm, :]
        pr_ref[...] = p + (prev - p) * mu_ref[...]

    @pl.when(n >= n_f32_tiles)
    def _():
        pa_ref[...] = _dot(hb_ref[...], wa_ref[...]).astype(BF16)


def _inproj(hb, w_rwkv, w_attn, mu, seq, tm, tn):
    m, d = hb.shape
    n_rwkv_cols, n_attn_cols = w_rwkv.shape[1], w_attn.shape[1]
    nf = n_rwkv_cols // tn
    kern = functools.partial(_inproj_kernel, n_f32_tiles=nf, tiles_per_seq=seq // tm)
    first = lambda i, n: (0, jnp.minimum(n, nf - 1))
    second = lambda i, n: (0, jnp.maximum(n - nf, 0))
    return pl.pallas_call(
        kern,
        grid=(m // tm, (n_rwkv_cols + n_attn_cols) // tn),
        in_specs=[
            pl.BlockSpec((tm, d), lambda i, n: (i, 0)),
            pl.BlockSpec((d, tn), first),
            pl.BlockSpec((d, tn), second),
            pl.BlockSpec((1, tn), first),
        ],
        out_specs=[
            pl.BlockSpec((tm, tn), lambda i, n: (i, jnp.minimum(n, nf - 1))),
            pl.BlockSpec((tm, tn), lambda i, n: (i, jnp.maximum(n - nf, 0))),
        ],
        out_shape=[
            jax.ShapeDtypeStruct((m, n_rwkv_cols), F32),
            jax.ShapeDtypeStruct((m, n_attn_cols), BF16),
        ],
        scratch_shapes=[pltpu.VMEM((nf, 1, tn), F32)],
        compiler_params=_cparams(("arbitrary", "arbitrary")),
        name="inproj",
    )(hb, w_rwkv, w_attn, mu)


def _wkv_kernel(pr_ref, pk_ref, pv_ref, pl_ref,
                w0_ref, a0_ref, kk_ref, ka_ref, rk_ref, lg_ref, lb_ref,
                wd_ref, wa_ref, wg_ref,
                y_ref,
                s_ref,
                r_s, lw_s, k_s, v_s, a_s, b_s, y_s, g_s, bo_s, zero_ref, ring64, ring128, ringg,
                *, n_pairs, tt):
    C = WKV_CHUNK
    t_idx = pl.program_id(2)

    @pl.when(t_idx == 0)
    def _():
        s_ref[...] = jnp.zeros_like(s_ref)

    lane = lax.broadcasted_iota(jnp.int32, (LANES, LANES), 1)
    sub = lax.broadcasted_iota(jnp.int32, (LANES, LANES), 0)
    head_ones = jnp.where((lane // 64) == (sub // 64), 1.0, 0.0).astype(BF16)

    def head_sum(x):
        hi = x.astype(BF16)
        lo = (x - hi.astype(F32)).astype(BF16)
        return _dot(hi, head_ones) + _dot(lo, head_ones)

    da = pl_ref[:, 0:LANES]
    th = jnp.tanh(da).astype(BF16)
    sg = jax.nn.sigmoid(pl_ref[:, LANES:]).astype(BF16)
    da = da.astype(BF16)
    for g in range(n_pairs):
        cs = slice(g * LANES, (g + 1) * LANES)
        rg, kg, vg = pr_ref[:, cs], pk_ref[:, cs], pv_ref[:, cs]
        pre = w0_ref[:, cs] + _dot(th, wd_ref[:, cs])
        softplus_neg = jnp.maximum(-pre, 0.0) + jnp.log(1.0 + jnp.exp(-jnp.abs(pre)))
        w_log = -softplus_neg - 0.5
        lw = -jnp.exp(w_log)
        a = jax.nn.sigmoid(a0_ref[:, cs] + _dot(da, wa_ref[:, cs]))
        gate = _dot(sg, wg_ref[:, cs])
        kk = kg * kk_ref[:, cs]
        kk = kk * lax.rsqrt(jnp.maximum(head_sum(kk * kk), 1e-24))
        k2 = kg * (1.0 + (a - 1.0) * ka_ref[:, cs])
        bonus = head_sum(rg * k2 * rk_ref[:, cs]) * vg
        r_s[:, cs] = rg
        lw_s[:, cs] = lw
        k_s[:, cs] = k2
        v_s[:, cs] = vg
        a_s[:, cs] = -kk
        b_s[:, cs] = kk * a
        g_s[:, cs] = gate
        bo_s[:, cs] = bonus

    ci = lax.broadcasted_iota(jnp.int32, (C, 2 * C), 0)
    cj = lax.broadcasted_iota(jnp.int32, (C, 2 * C), 1)
    left = cj < C
    strict = (cj % C) < ci
    incl = (cj % C) <= ci
    tri_incl = jnp.where(lax.broadcasted_iota(jnp.int32, (C, C), 1)
                         <= lax.broadcasted_iota(jnp.int32, (C, C), 0), 1.0, 0.0).astype(BF16)
    lane_c = lax.broadcasted_iota(jnp.int32, (C, LANES), 1)
    m0 = lane_c < 64
    eye = jnp.where(lane == sub, 1.0, 0.0).astype(F32)
    blockdiag = (lane // 64) == (sub // 64)

    csl = [slice(g * LANES, (g + 1) * LANES) for g in range(n_pairs)]
    P = range(n_pairs)
    cat0 = lambda *xs: jnp.concatenate(xs, axis=0)
    cat1 = lambda *xs: jnp.concatenate(xs, axis=1)
    bf = lambda x: x.astype(BF16)

    n_chunks = tt // C
    MM0, MM1, RTB, VB = range(4)
    BK, AK, ATB = range(3)
    ring64[...] = jnp.zeros_like(ring64)
    ring128[...] = jnp.zeros_like(ring128)
    ringg[...] = jnp.zeros_like(ringg)

    def stage1(c, out):
        rows = pl.ds(pl.multiple_of(c * C, C), C)
        slot = c % 4
        ld = lambda ref: [ref[rows, csl[g]] for g in P]
        rc, lwc, kc, vc, ac, bc = ld(r_s), ld(lw_s), ld(k_s), ld(v_s), ld(a_s), ld(b_s)
        cum = [_dot_exact_by_f32(tri_incl, lwc[g]) for g in P]
        yield
        last = [cum[g][C - 1:C, :] for g in P]
        rt = [rc[g] * jnp.exp(cum[g]) for g in P]
        at = [ac[g] * jnp.exp(cum[g] - lwc[g]) for g in P]
        ginv = [jnp.exp(-cum[g]) for g in P]
        btb = [bf(bc[g] * ginv[g]) for g in P]
        ktb = [bf(kc[g] * ginv[g]) for g in P]
        ghat = [jnp.exp(last[g] - cum[g]) for g in P]
        lhs0 = [bf(cat0(jnp.where(m0, at[g], 0.0), jnp.where(m0, rt[g], 0.0))) for g in P]
        lhs1 = [bf(cat0(jnp.where(m0, 0.0, at[g]), jnp.where(m0, 0.0, rt[g]))) for g in P]
        for g in P:
            ring128[slot, g, BK] = cat0(bf(bc[g] * ghat[g]), bf(kc[g] * ghat[g]))
            ring128[slot, g, ATB] = cat0(lhs0[g][:C], lhs1[g][:C])
            ring64[slot, g, VB] = bf(vc[g])
            ring64[slot, g, RTB] = bf(rt[g])
            ringg[slot, g, 0:1, :] = jnp.exp(last[g])
        x0 = [_dot_nt(lhs0[g], cat0(btb[g], ktb[g])) for g in P]
        x1 = [_dot_nt(lhs1[g], cat0(ktb[g], btb[g])) for g in P]
        yield
        n_bd = [cat0(jnp.where(left & strict, x0[g][:C], 0.0),
                     jnp.where((~left) & strict, x1[g][:C], 0.0)) for g in P]
        for g in P:
            ring128[slot, g, AK] = bf(cat0(jnp.where((~left) & strict, x0[g][:C], 0.0),
                                           jnp.where(left & strict, x1[g][:C], 0.0)))
            ring64[slot, g, MM0] = bf(jnp.where(incl, x0[g][C:], 0.0))
            ring64[slot, g, MM1] = bf(jnp.where(incl, x1[g][C:], 0.0))
        out["t"] = [eye + n_bd[g] for g in P]
        nb = [bf(n_bd[g]) for g in P]
        out["pw"] = [bf(_dot(nb[g], nb[g])) for g in P]

    def inverse_level(t, pw):
        res = [_dot(pw[g], cat1(pw[g], bf(t[g]))) for g in P]
        return [t[g] + res[g][:, LANES:] for g in P], [bf(res[g][:, :LANES]) for g in P]

    def stage2(inp, out):
        t, pw = inp["t"], inp["pw"]
        for level in range(3):
            t, pw = inverse_level(t, pw)
            if level < 2:
                yield
        out["t"], out["pw"] = t, pw

    def stage3(inp, c, out):
        slot = c % 4
        t, pw = inverse_level(inp["t"], inp["pw"])
        vb = [ring64[slot, g, VB] for g in P]
        av = [_dot(ring128[slot, g, AK], cat0(vb[g], vb[g])) for g in P]
        yield
        t = [t[g] + _dot(pw[g], bf(t[g])) for g in P]
        av = [cat0(jnp.where(m0, av[g][:C], 0.0), jnp.where(m0, 0.0, av[g][C:])) for g in P]
        yield
        wu = [_dot(bf(t[g]), cat1(ring128[slot, g, ATB], bf(av[g]))) for g in P]
        out["w"] = [bf(wu[g][:C, :LANES] + wu[g][C:, :LANES]) for g in P]
        out["u0"] = [wu[g][:C, LANES:] + wu[g][C:, LANES:] for g in P]

    def stage4(inp, c, valid):
        rows = pl.ds(pl.multiple_of(c * C, C), C)
        slot = c % 4
        vb = [ring64[slot, g, VB] for g in P]
        s = [s_ref[g] for g in P]
        sb = [bf(s[g]) for g in P]
        ub = [bf(_dot_nt(inp["w"][g], sb[g]) + inp["u0"][g]) for g in P]
        ys = [_dot_nt(ring64[slot, g, RTB], sb[g]) for g in P]
        yield
        uv = [cat0(ub[g], vb[g]) for g in P]
        y = [ys[g] + jnp.where(m0, _dot(ring64[slot, g, MM0], uv[g]),
                               _dot(ring64[slot, g, MM1], cat0(vb[g], ub[g]))) for g in P]
        upd = [_dot_tn(uv[g], ring128[slot, g, BK]) for g in P]
        yield
        for g in P:
            s_new = s[g] * ringg[slot, g, 0:1, :] + jnp.where(blockdiag, upd[g], 0.0)
            s_ref[g] = jnp.where(valid, s_new, s[g])
            y_s[rows, csl[g]] = y[g]

    def interleave(gens):
        live = list(gens)
        while live:
            still = []
            for gen in live:
                try:
                    next(gen)
                    still.append(gen)
                except StopIteration:
                    pass
            live = still

    def body(it, carry):
        o1, o2, o3 = carry
        n1, n2, n3 = {}, {}, {}
        interleave([stage4(o3, jnp.maximum(it - 3, 0), it >= 3),
                    stage3(o2, jnp.maximum(it - 2, 0), n3),
                    stage2(o1, n2),
                    stage1(jnp.minimum(it, n_chunks - 1), n1)])
        return n1, n2, n3

    zero_ref[...] = jnp.zeros_like(zero_ref)
    zb = lambda r: [zero_ref[0:r, :].astype(BF16) for _ in P]
    zf = lambda r: [zero_ref[0:r, :] for _ in P]
    front = lambda: dict(t=zf(2 * C), pw=zb(2 * C))
    lax.fori_loop(0, n_chunks + 3, body, (front(), front(), dict(w=zb(C), u0=zf(C))))

    for g in range(n_pairs):
        cs = slice(g * LANES, (g + 1) * LANES)
        y = y_s[:, cs]
        mean = head_sum(y) * (1.0 / 64.0)
        yc = y - mean
        var = head_sum(yc * yc) * (1.0 / 64.0)
        yn = yc * lax.rsqrt(var + GN_EPS) * lg_ref[:, cs] + lb_ref[:, cs]
        y_ref[:, cs] = ((yn + bo_s[:, cs]) * g_s[:, cs]).astype(BF16)


def _wkv(p_rwkv, prm, batch, seq, n_pairs, tt):
    m = p_rwkv.shape[0]
    c = prm["w0"].shape[1]
    gw = n_pairs * LANES
    n_col_blocks = c // gw
    nt = seq // tt
    low_w = LANES + prm["wg"].shape[0]

    def pspec(off):
        return pl.BlockSpec((tt, gw), lambda b, g, t: (b * nt + t, off * n_col_blocks + g))

    def vspec():
        return pl.BlockSpec((1, gw), lambda b, g, t: (0, g))

    def wspec(rows):
        return pl.BlockSpec((rows, gw), lambda b, g, t: (0, g))

    kern = functools.partial(_wkv_kernel, n_pairs=n_pairs, tt=tt)
    tile = pltpu.VMEM((tt, gw), F32)
    return pl.pallas_call(
        kern,
        grid=(batch, n_col_blocks, nt),
        in_specs=[
            pspec(0), pspec(1), pspec(2),
            pl.BlockSpec((tt, low_w), lambda b, g, t: (b * nt + t, (3 * c) // low_w)),
            vspec(), vspec(), vspec(), vspec(), vspec(), vspec(), vspec(),
            wspec(LANES), wspec(LANES), wspec(low_w - LANES),
        ],
        out_specs=pl.BlockSpec((tt, gw), lambda b, g, t: (b * nt + t, g)),
        out_shape=jax.ShapeDtypeStruct((m, c), BF16),
        scratch_shapes=[
            pltpu.VMEM((n_pairs, LANES, LANES), F32),
            tile, tile, tile, tile, tile, tile, tile, tile, tile,
            pltpu.VMEM((LANES, LANES), F32),
            pltpu.VMEM((4, n_pairs, 4, WKV_CHUNK, LANES), BF16),
            pltpu.VMEM((4, n_pairs, 3, 2 * WKV_CHUNK, LANES), BF16),
            pltpu.VMEM((4, n_pairs, 8, LANES), F32),
        ],
        compiler_params=_cparams(("parallel", "parallel", "arbitrary")),
        name="wkv7",
    )(p_rwkv, p_rwkv, p_rwkv, p_rwkv,
      prm["w0"], prm["a0"], prm["k_k"], prm["k_a"], prm["r_k"], prm["lnx_g"], prm["lnx_b"],
      prm["wd"], prm["wa"], prm["wg"])


def _sb_kernel(q_ref, k_ref, v_ref, o_ref, acc_ref, right_ref, *, tq, scale):
    i = pl.program_id(1)
    row = lax.broadcasted_iota(jnp.int32, (tq, tq), 0)
    col = lax.broadcasted_iota(jnp.int32, (tq, tq), 1)
    r2 = lax.broadcasted_iota(jnp.int32, (tq, 2 * tq), 0)
    c2 = lax.broadcasted_iota(jnp.int32, (tq, 2 * tq), 1)
    after_and_total = jnp.where((c2 >= tq) | (r2 > c2), 1.0, 0.0).astype(BF16)
    diag = col < row

    heads = range(SB_HEADS)
    hsl = [slice(h * LANES, (h + 1) * LANES) for h in heads]

    def blocks(js, first):
        nb = len(js)
        it = [(b, h) for b in range(nb) for h in heads]
        ks = [pl.ds(pl.multiple_of(j * tq, tq), tq) for j in js]
        z = {p: _dot_nt(q_ref[:, hsl[p[1]]], k_ref[ks[p[0]], hsl[p[1]]]) * scale for p in it}
        sp = {p: jnp.maximum(z[p], 0.0) + jnp.log(1.0 + jnp.exp(-jnp.abs(z[p]))) for p in it}
        log_keep = {p: jnp.where(diag, -sp[p], 0.0) if first else -sp[p] for p in it}
        hi = {p: log_keep[p].astype(BF16) for p in it}
        lo = {p: (log_keep[p] - hi[p].astype(F32)).astype(BF16) for p in it}
        sums = {p: _dot(hi[p], after_and_total) + _dot(lo[p], after_and_total) for p in it}
        right = {}
        for h in heads:
            run = None if first else right_ref[h]
            for b in range(nb):
                right[(b, h)] = run
                tot = sums[(b, h)][:, tq:]
                run = tot if run is None else run + tot
            right[("end", h)] = run
        after = {p: sums[p][:, :tq] if right[p] is None else sums[p][:, :tq] + right[p] for p in it}
        attn = {p: jnp.exp(z[p] - sp[p] + after[p]) for p in it}
        if first:
            attn = {p: jnp.where(diag, attn[p], 0.0) for p in it}
        pv = {p: _dot(attn[p].astype(BF16), v_ref[ks[p[0]], hsl[p[1]]]) for p in it}
        for h in heads:
            tot = pv[(0, h)]
            for b in range(1, nb):
                tot = tot + pv[(b, h)]
            if first:
                acc_ref[h] = tot
            else:
                acc_ref[h] += tot
            right_ref[h] = right[("end", h)]

    blocks([i], True)

    def body(jj, _):
        j = i - 1 - SB_BLOCKS_PER_ITER * jj
        blocks([j - b for b in range(SB_BLOCKS_PER_ITER)], False)
        return 0

    lax.fori_loop(0, i // SB_BLOCKS_PER_ITER, body, 0)

    for rem in range(1, SB_BLOCKS_PER_ITER):
        @pl.when(i % SB_BLOCKS_PER_ITER == rem)
        def _(rem=rem):
            blocks(list(range(rem - 1, -1, -1)), False)

    for h in heads:
        o_ref[:, hsl[h]] = acc_ref[h].astype(BF16)


def _sb_attn(p_attn, batch, seq, tq):
    m = p_attn.shape[0]
    w = SB_HEADS * LANES
    nq = seq // tq
    kern = functools.partial(_sb_kernel, tq=tq, scale=LANES ** -0.5)
    return pl.pallas_call(
        kern,
        grid=(batch, nq),
        in_specs=[
            pl.BlockSpec((tq, w), lambda b, i: (b * nq + i, 0)),
            pl.BlockSpec((seq, w), lambda b, i: (b, 1)),
            pl.BlockSpec((seq, w), lambda b, i: (b, 2)),
        ],
        out_specs=pl.BlockSpec((tq, w), lambda b, i: (b * nq + i, 0)),
        out_shape=jax.ShapeDtypeStruct((m, w), BF16),
        scratch_shapes=[pltpu.VMEM((SB_HEADS, tq, tq), F32), pltpu.VMEM((SB_HEADS, tq, tq), F32)],
        compiler_params=_cparams(("parallel", "arbitrary")),
        name="sb_attn",
    )(p_attn, p_attn, p_attn)


def _mem_kv_kernel(m_ref, w_ref, o_ref):
    o_ref[...] = _dot(m_ref[...].astype(BF16), w_ref[...]).astype(BF16)


def _mem_kv(mem2, w_bf16, tm):
    m, d = mem2.shape
    n = w_bf16.shape[1]
    return pl.pallas_call(
        _mem_kv_kernel,
        grid=(m // tm,),
        in_specs=[pl.BlockSpec((tm, d), lambda i: (i, 0)),
                  pl.BlockSpec((d, n), lambda i: (0, 0))],
        out_specs=pl.BlockSpec((tm, n), lambda i: (i, 0)),
        out_shape=jax.ShapeDtypeStruct((m, n), BF16),
        compiler_params=_cparams(("parallel",)),
        name="mem_kv",
    )(mem2, w_bf16)


def _mem_attn_kernel(q_ref, k_ref, v_ref, o_ref, *, scale):
    for h in range(MEM_HEADS):
        hs = slice(h * LANES, (h + 1) * LANES)
        s = _dot_nt(q_ref[:, hs], k_ref[:, hs]) * scale
        s = s - jnp.max(s, axis=-1, keepdims=True)
        e = jnp.exp(s)
        p = e / jnp.sum(e, axis=-1, keepdims=True)
        o_ref[:, hs] = _dot(p.astype(BF16), v_ref[:, hs]).astype(BF16)


def _mem_attn(p_attn, kv, batch, seq, mem_len, tq):
    m = p_attn.shape[0]
    w = MEM_HEADS * LANES
    nq = seq // tq
    kern = functools.partial(_mem_attn_kernel, scale=LANES ** -0.5)
    return pl.pallas_call(
        kern,
        grid=(batch, nq),
        in_specs=[
            pl.BlockSpec((tq, w), lambda b, i: (b * nq + i, 3)),
            pl.BlockSpec((mem_len, w), lambda b, i: (b, 0)),
            pl.BlockSpec((mem_len, w), lambda b, i: (b, 1)),
        ],
        out_specs=pl.BlockSpec((tq, w), lambda b, i: (b * nq + i, 0)),
        out_shape=jax.ShapeDtypeStruct((m, w), BF16),
        compiler_params=_cparams(("parallel", "parallel")),
        name="mem_attn",
    )(p_attn, kv, kv)


def _outproj_kernel(yr_ref, ys_ref, ym_ref, x_ref, g0_ref, b0_ref, w_ref, g_ref, b_ref, rw_ref, rb_ref,
                    h1_ref, lg_ref):
    c0 = yr_ref.shape[1]
    c1 = c0 + ys_ref.shape[1]
    mix = (_dot(yr_ref[...], w_ref[0:c0, :]) + _dot(ys_ref[...], w_ref[c0:c1, :])
           + _dot(ym_ref[...], w_ref[c1:, :]))
    h = _layer_norm(x_ref[...], g0_ref[...], b0_ref[...])
    h1 = _layer_norm(DEEPNORM_ALPHA * h + mix, g_ref[...], b_ref[...])
    h1_ref[...] = h1
    hi = h1.astype(BF16)
    lo = (h1 - hi.astype(F32)).astype(BF16)
    both = _dot(hi, rw_ref[...])
    lg_ref[...] = both[:, :LANES] + both[:, LANES:] + _dot(lo, rw_ref[:, :LANES]) + rb_ref[...]


def _outproj(y_r, y_s, y_m, x2, g0, b0, w, g, b, r_w, r_b, tm):
    m, d = x2.shape
    full = lambda a: pl.BlockSpec(a.shape, lambda i: (0, 0))
    rows = lambda a: pl.BlockSpec((tm, a.shape[1]), lambda i: (i, 0))
    return pl.pallas_call(
        _outproj_kernel,
        grid=(m // tm,),
        in_specs=[rows(y_r), rows(y_s), rows(y_m), rows(x2), full(g0), full(b0), full(w), full(g), full(b),
                  full(r_w), full(r_b)],
        out_specs=[pl.BlockSpec((tm, d), lambda i: (i, 0)),
                   pl.BlockSpec((tm, LANES), lambda i: (i, 0))],
        out_shape=[jax.ShapeDtypeStruct((m, d), F32),
                   jax.ShapeDtypeStruct((m, LANES), F32)],
        compiler_params=_cparams(("parallel",)),
        name="outproj",
    )(y_r, y_s, y_m, x2, g0, b0, w, g, b, r_w, r_b)


def _route_kernel(lg_ref, id_ref, wt_ref, *, n_groups, per_group):
    lg = lg_ref[...]
    lane_i = lax.broadcasted_iota(jnp.int32, lg.shape, 1)
    lane = lane_i.astype(F32)
    neg = jnp.float32(-jnp.inf)
    big = jnp.float32(2 ** 20)

    def first_max(vals):
        mx = jnp.max(vals, axis=-1, keepdims=True)
        idx = jnp.min(jnp.where(vals == mx, lane, big), axis=-1, keepdims=True)
        return mx, idx

    is_group = lane < n_groups
    gmax, gidx = first_max(jnp.where(is_group, lg, neg))
    gsum = jnp.sum(jnp.where(is_group, jnp.exp(lg - gmax), 0.0), axis=-1, keepdims=True)
    group_w = 1.0 / gsum
    lo = n_groups + gidx * per_group
    in_group = (lane >= lo) & (lane < lo + per_group)
    v1, i1 = first_max(jnp.where(in_group, lg, neg))
    v2, i2 = first_max(jnp.where(in_group & (lane != i1), lg, neg))
    e2 = jnp.exp(v2 - v1)
    w1 = group_w / (1.0 + e2)
    w2 = group_w * e2 / (1.0 + e2)
    ids = jnp.where(lane_i == 0, i1 - n_groups, jnp.where(lane_i == 1, i2 - n_groups, 0.0))
    id_ref[...] = ids.T[0:8, :].astype(jnp.int32)
    wt_ref[...] = jnp.where(lane_i == 0, w1, jnp.where(lane_i == 1, w2, 0.0))


def _route(logits, n_groups, per_group, tm):
    m = logits.shape[0]
    kern = functools.partial(_route_kernel, n_groups=n_groups, per_group=per_group)
    spec = pl.BlockSpec((tm, LANES), lambda i: (i, 0))
    return pl.pallas_call(
        kern,
        grid=(m // tm,),
        in_specs=[spec],
        out_specs=[pl.BlockSpec((8, tm), lambda i: (0, i)), spec],
        out_shape=[jax.ShapeDtypeStruct((8, m), jnp.int32),
                   jax.ShapeDtypeStruct((m, LANES), F32)],
        compiler_params=_cparams(("parallel",)),
        name="route",
    )(logits)


def _plan_kernel(id_ref, pos_ref, meta_ref, cnt_ref, base_ref, start_ref, *, tb, rows, n_experts, nb):
    phase = pl.program_id(0)
    j = pl.program_id(1)
    sub = lax.broadcasted_iota(jnp.int32, (LANES, tb), 0)
    e1 = id_ref[0:1, :]
    e2 = id_ref[1:2, :]
    hit1 = jnp.where(sub == e1, 1.0, 0.0)
    hit2 = jnp.where(sub == e2, 1.0, 0.0)
    hits = hit1 + hit2
    per_expert = jnp.sum(hits, axis=1, keepdims=True)

    @pl.when((phase == 0) & (j == 0))
    def _():
        cnt_ref[...] = jnp.zeros_like(cnt_ref)

    @pl.when(phase == 0)
    def _():
        cnt_ref[...] += per_expert

    sq_r = lax.broadcasted_iota(jnp.int32, (LANES, LANES), 0)
    sq_c = lax.broadcasted_iota(jnp.int32, (LANES, LANES), 1)

    @pl.when((phase == 1) & (j == 0))
    def _():
        n_blk = jnp.floor((cnt_ref[...] + (rows - 1)) * (1.0 / rows))
        before = jnp.where(sq_c < sq_r, 1.0, 0.0).astype(BF16)
        blk_start = _dot(before, jnp.broadcast_to(n_blk, (LANES, LANES)).astype(BF16))
        start_ref[...] = blk_start[:, 0:1]
        base_ref[...] = jnp.zeros_like(base_ref)
        blk_end = blk_start + n_blk
        n_used = jnp.sum(jnp.where(sq_r[:, 0:1] < n_experts, n_blk, 0.0), axis=0, keepdims=True)
        owner = jnp.sum(jnp.where((sq_r < n_experts) & (blk_end <= sq_c.astype(F32)), 1.0, 0.0),
                        axis=0, keepdims=True)
        last_owner = jnp.max(jnp.where(n_blk > 0.0, sq_r[:, 0:1].astype(F32), 0.0), axis=0, keepdims=True)
        blk = sq_c[0:1, :].astype(F32)
        in_use = blk < n_used
        owner = jnp.where(in_use, owner, last_owner)
        mine = sq_r.astype(F32) == owner
        cnt_o = jnp.sum(jnp.where(mine, cnt_ref[...], 0.0), axis=0, keepdims=True)
        start_o = jnp.sum(jnp.where(mine, blk_start, 0.0), axis=0, keepdims=True)
        valid = jnp.clip(cnt_o - (blk - start_o) * rows, 0.0, float(rows))
        valid = jnp.where(in_use, valid, 0.0)
        row8 = lax.broadcasted_iota(jnp.int32, (8, LANES), 0)
        meta = jnp.where(row8 == 0, owner, jnp.where(row8 == 1, valid, jnp.where(row8 == 2, n_used, 0.0)))
        meta_ref[...] = meta.astype(jnp.int32)

    @pl.when(phase == 1)
    def _():
        tr = lax.broadcasted_iota(jnp.int32, (tb, tb), 0)
        tc = lax.broadcasted_iota(jnp.int32, (tb, tb), 1)
        earlier = jnp.where(tr < tc, 1.0, 0.0).astype(BF16)
        seen = _dot(hits.astype(BF16), earlier)
        slot = start_ref[...] * rows + base_ref[...] + seen
        p1 = jnp.sum(hit1 * slot, axis=0, keepdims=True)
        p2 = jnp.sum(hit2 * slot, axis=0, keepdims=True)
        row8 = lax.broadcasted_iota(jnp.int32, (8, tb), 0)
        pos_ref[...] = jnp.where(row8 == 0, p1, jnp.where(row8 == 1, p2, 0.0)).astype(jnp.int32)
        base_ref[...] += per_expert


def _plan(ids_t, n_experts, rows, tb):
    m = ids_t.shape[1]
    nb = m // tb
    kern = functools.partial(_plan_kernel, tb=tb, rows=rows, n_experts=n_experts, nb=nb)
    col = pltpu.VMEM((LANES, 1), F32)
    return pl.pallas_call(
        kern,
        grid=(2, nb),
        in_specs=[pl.BlockSpec((8, tb), lambda p, j: (0, j))],
        out_specs=[pl.BlockSpec((8, tb), lambda p, j: (0, j * p)),
                   pl.BlockSpec((8, LANES), lambda p, j: (0, 0))],
        out_shape=[jax.ShapeDtypeStruct((8, m), jnp.int32),
                   jax.ShapeDtypeStruct((8, LANES), jnp.int32)],
        scratch_shapes=[col, col, col],
        compiler_params=_cparams(("arbitrary", "arbitrary")),
        name="moe_plan",
    )(ids_t)


def _for_range(lo, hi, body):
    full = (hi - lo) // ROW_GROUP

    def group(g, _):
        for u in range(ROW_GROUP):
            body(lo + g * ROW_GROUP + u)
        return 0

    lax.fori_loop(0, full, group, 0)

    def one(r, _):
        body(r)
        return 0

    lax.fori_loop(lo + full * ROW_GROUP, hi, one, 0)


def _row_in(src_hbm, dst_vmem, sem, src_row, dst_row):
    return pltpu.make_async_copy(src_hbm.at[pl.ds(src_row, 1)], dst_vmem.at[pl.ds(dst_row, 1)], sem)


def _row_out(src_vmem, dst_hbm, sem, src_row, dst_row):
    return pltpu.make_async_copy(src_vmem.at[pl.ds(src_row, 1)], dst_hbm.at[pl.ds(dst_row, 1)], sem)


def _gather_cparams(sem):
    return pltpu.CompilerParams(dimension_semantics=sem, vmem_limit_bytes=VMEM_LIMIT,
                                disable_bounds_checks=True)


def _ffn_kernel(asg_ref, be_ref, nv_ref, nu_ref, h_hbm, wg_ref, wu_ref, wd_ref, y_hbm,
                xf_ref, xb_ref, acc_ref, sem_in, sem_out, *, rows, sub, nj, n_tok, n_blocks):
    i = pl.program_id(0)
    j = pl.program_id(1)
    n_used = nu_ref[0]
    used = i < n_used
    slot = i % 2

    def gather(block, lo, hi, buf):
        def body(r):
            a = asg_ref[block * rows + r]
            tok = jnp.where(a >= n_tok, a - n_tok, a)
            _row_in(h_hbm, xf_ref.at[buf], sem_in.at[buf], tok, r).start()

        _for_range(lo, hi, body)

    def gather_wait(block, buf):
        _for_range(0, nv_ref[block], lambda r: _row_in(h_hbm, xf_ref.at[buf], sem_in.at[buf], 0, r).wait())

    def emit(block, buf):
        def body(r):
            _row_out(acc_ref.at[buf], y_hbm, sem_out.at[buf], r, asg_ref[block * rows + r]).start()

        _for_range(0, nv_ref[block], body)

    def emit_wait(block, buf):
        _for_range(0, nv_ref[block], lambda r: _row_out(acc_ref.at[buf], y_hbm, sem_out.at[buf], r, 0).wait())

    @pl.when((i == 0) & (j == 0))
    def _():
        xf_ref[...] = jnp.zeros_like(xf_ref)
        gather(0, 0, nv_ref[0], 0)

    @pl.when(used & (j == 0))
    def _():
        gather_wait(i, slot)

    @pl.when(i + 1 < n_used)
    def _():
        nxt = nv_ref[i + 1]
        q = rows // nj
        gather(i + 1, jnp.minimum(j * q, nxt), jnp.minimum((j + 1) * q, nxt), 1 - slot)

    n_sub = (nv_ref[i] + sub - 1) // sub
    for n in range(1, rows // sub + 1):
        @pl.when(used & (n_sub == n))
        def _(n=n):
            r = n * sub

            @pl.when(j == 0)
            def _():
                xb_ref[0:r, :] = xf_ref[slot, 0:r, :].astype(BF16)
                acc_ref[slot, 0:r, :] = jnp.zeros((r, acc_ref.shape[2]), F32)

            xb = xb_ref[0:r, :]
            gate = _dot(xb, wg_ref[...].astype(BF16))
            up = _dot(xb, wu_ref[...].astype(BF16))
            hid = (gate * jax.nn.sigmoid(gate)) * up
            acc_ref[slot, 0:r, :] += _dot(hid.astype(BF16), wd_ref[...].astype(BF16))

    @pl.when((j == nj - 1) & (i >= 1) & (i <= n_used))
    def _():
        emit_wait(i - 1, 1 - slot)

    @pl.when((j == nj - 1) & used)
    def _():
        emit(i, slot)

    @pl.when((j == nj - 1) & used & (i == n_blocks - 1))
    def _():
        emit_wait(i, slot)


def _moe_ffn(slot_asg, block_expert, n_valid, n_used, h1, w_gate, w_up, w_down, n_blocks, rows, tf):
    n_tok, d = h1.shape
    de = w_gate.shape[2]
    nj = de // tf
    kern = functools.partial(_ffn_kernel, rows=rows, sub=MOE_SUB, nj=nj, n_tok=n_tok, n_blocks=n_blocks)

    def jidx(i, j, nu):
        return jnp.where(i < nu[0], j, nj - 1)

    return pl.pallas_call(
        kern,
        grid_spec=pltpu.PrefetchScalarGridSpec(
            num_scalar_prefetch=4,
            grid=(n_blocks, nj),
            in_specs=[
                pl.BlockSpec(memory_space=pl.ANY),
                pl.BlockSpec((None, d, tf), lambda i, j, asg, be, nv, nu: (be[i], 0, jidx(i, j, nu))),
                pl.BlockSpec((None, d, tf), lambda i, j, asg, be, nv, nu: (be[i], 0, jidx(i, j, nu))),
                pl.BlockSpec((None, tf, d), lambda i, j, asg, be, nv, nu: (be[i], jidx(i, j, nu), 0)),
            ],
            out_specs=pl.BlockSpec(memory_space=pl.ANY),
            scratch_shapes=[pltpu.VMEM((2, rows, d), F32),
                            pltpu.VMEM((rows, d), BF16),
                            pltpu.VMEM((2, rows, d), F32),
                            pltpu.SemaphoreType.DMA((2,)),
                            pltpu.SemaphoreType.DMA((2,))],
        ),
        out_shape=jax.ShapeDtypeStruct((2 * n_tok, d), F32),
        compiler_params=_gather_cparams(("arbitrary", "arbitrary")),
        name="moe_ffn",
    )(slot_asg, block_expert, n_valid, n_used, h1, w_gate, w_up, w_down)


def _combine_kernel(h_ref, y0_ref, y1_ref, wt_ref, g_ref, b_ref, o_ref):
    wt = wt_ref[...]
    ffn = y0_ref[...] * wt[:, 0:1] + y1_ref[...] * wt[:, 1:2]
    o_ref[...] = _layer_norm(DEEPNORM_ALPHA * h_ref[...] + ffn, g_ref[...], b_ref[...])


def _combine(h1, ys, wts, g, b, tm):
    m, d = h1.shape
    nt = m // tm
    rows = pl.BlockSpec((tm, d), lambda i: (i, 0))
    vec = pl.BlockSpec((1, d), lambda i: (0, 0))
    return pl.pallas_call(
        _combine_kernel,
        grid=(nt,),
        in_specs=[rows, rows, pl.BlockSpec((tm, d), lambda i: (nt + i, 0)),
                  pl.BlockSpec((tm, LANES), lambda i: (i, 0)), vec, vec],
        out_specs=rows,
        out_shape=jax.ShapeDtypeStruct((m, d), F32),
        compiler_params=_cparams(("parallel",)),
        name="moe_combine",
    )(h1, ys, ys, wts, g, b)


def _dispatch_plan(ids_t, n_experts, rows):
    m = ids_t.shape[1]
    n_blocks = -(-(2 * m) // rows) + n_experts
    assert n_blocks <= LANES
    pos_t, meta = _plan(ids_t, n_experts, rows, _pick(m, 512))
    pos = pos_t[:2].reshape(-1)
    slot_asg = jnp.zeros((n_blocks * rows,), jnp.int32).at[pos].set(jnp.arange(2 * m, dtype=jnp.int32))
    return slot_asg, meta[2, :1], meta[0, :n_blocks], meta[1, :n_blocks], n_blocks


def _pick(n, pref):
    t = min(pref, n)
    while n % t:
        t //= 2
    return t


def kernel(x, mem, ln_in_g, ln_in_b, w_in, tshift_mu, w0, w_decay_up, a0, w_a_up, w_g_up, k_k, k_a, r_k,
           lnx_g, lnx_b, w_mem_kv, w_out, ln1_g, ln1_b, router_group, router_group_b, router_expert,
           router_expert_b, w_e_gate, w_e_up, w_e_down, ln2_g, ln2_b):
    batch, seq, d = x.shape
    mem_len = mem.shape[1]
    m = batch * seq
    c = w0.shape[1]
    dr, ar, gr = w_decay_up.shape[1], w_a_up.shape[1], w_g_up.shape[1]
    rwkv_cols = 3 * c + dr + ar + gr
    sb_w = SB_HEADS * LANES
    mem_w = MEM_HEADS * LANES
    assert dr + ar == LANES and c % (2 * LANES) == 0 and w_in.shape[0] == DEPTH
    assert w_in.shape[2] == rwkv_cols + 3 * sb_w + mem_w
    n_experts = router_expert.shape[2]
    row = lambda a: a.reshape(1, -1)

    tn = 512
    low_w = -(-(dr + ar + gr) // LANES) * LANES
    rw_pad = -(-(3 * c + low_w) // tn) * tn
    wi = w_in[0]
    w_rwkv = jnp.pad(wi[:, :rwkv_cols].astype(BF16), ((0, 0), (0, rw_pad - rwkv_cols)))
    w_attn = wi[:, rwkv_cols:].astype(BF16)
    mu = row(jnp.pad(tshift_mu[0], (0, rw_pad - rwkv_cols)))
    lp = low_w - (dr + ar + gr)
    prm = dict(
        w0=row(w0[0]), a0=row(a0[0]), k_k=row(k_k[0]), k_a=row(k_a[0]), r_k=row(r_k[0]),
        lnx_g=row(lnx_g[0]), lnx_b=row(lnx_b[0]),
        wd=jnp.concatenate([w_decay_up[0], jnp.zeros((ar, c), F32)], axis=0).astype(BF16),
        wa=jnp.concatenate([jnp.zeros((dr, c), F32), w_a_up[0]], axis=0).astype(BF16),
        wg=jnp.concatenate([w_g_up[0], jnp.zeros((lp, c), F32)], axis=0).astype(BF16),
    )
    assert (3 * c) % low_w == 0

    x2 = x.reshape(m, d)
    hb = _ln_in(x2, row(ln_in_g), row(ln_in_b), _pick(m, 256))
    p_rwkv, p_attn = _inproj(hb, w_rwkv, w_attn, mu, seq, _pick(seq, 1024), tn)

    y_rwkv = _wkv(p_rwkv, prm, batch, seq, n_pairs=4, tt=_pick(seq, 1024))
    y_sb = _sb_attn(p_attn, batch, seq, tq=LANES)
    kv = _mem_kv(mem.reshape(batch * mem_len, d), w_mem_kv[0].astype(BF16), _pick(batch * mem_len, 256))
    y_mem = _mem_attn(p_attn, kv, batch, seq, mem_len, tq=_pick(seq, 512))

    wo = w_out[0].astype(BF16)
    r_w = jnp.concatenate([router_group[0], router_expert[0],
                           jnp.zeros((d, LANES - N_GROUPS - n_experts), F32)], axis=1)
    r_b = jnp.concatenate([router_group_b[0], router_expert_b[0],
                           jnp.zeros((LANES - N_GROUPS - n_experts,), F32)]).reshape(1, LANES)
    r_hi = r_w.astype(BF16)
    r_w2 = jnp.concatenate([r_hi, (r_w - r_hi.astype(F32)).astype(BF16)], axis=1)
    h1, logits = _outproj(y_rwkv, y_sb, y_mem, x2, row(ln_in_g), row(ln_in_b), wo,
                          row(ln1_g[0]), row(ln1_b[0]), r_w2, r_b, _pick(m, 512))

    ids_t, wts = _route(logits, N_GROUPS, n_experts // N_GROUPS, _pick(m, 512))
    slot_asg, n_used, block_expert, n_valid, n_blocks = _dispatch_plan(ids_t, n_experts, MOE_ROWS)
    ys = _moe_ffn(slot_asg, block_expert, n_valid, n_used, h1, w_e_gate[0], w_e_up[0], w_e_down[0],
                  n_blocks, MOE_ROWS, tf=512)
    out = _combine(h1, ys, wts, row(ln2_g[0]), row(ln2_b[0]), _pick(m, 256))
    return out.reshape(batch, seq, d)
```

```python
import functools

import jax
import jax.numpy as jnp
from jax import lax
from jax.experimental import pallas as pl
from jax.experimental.pallas import tpu as pltpu

F32 = jnp.float32
BF16 = jnp.bfloat16

SB_HEADS = 4
MEM_HEADS = 4
N_GROUPS = 8
DEPTH = 1
DEEPNORM_ALPHA = (2.0 * DEPTH) ** 0.25
LN_EPS = 1e-5
GN_EPS = 64e-5

LANES = 128
WKV_CHUNK = 64
MOE_ROWS = 512
MOE_SUB = 64
ROW_GROUP = 8
SB_BLOCKS_PER_ITER = 3
VMEM_LIMIT = 56 * 1024 * 1024


def _cparams(sem):
    return pltpu.CompilerParams(dimension_semantics=sem, vmem_limit_bytes=VMEM_LIMIT)


def _layer_norm(x, g, b):
    mu = jnp.mean(x, axis=-1, keepdims=True)
    xc = x - mu
    var = jnp.mean(xc * xc, axis=-1, keepdims=True)
    return xc * lax.rsqrt(var + LN_EPS) * g + b


def _split2(x):
    hi = x.astype(BF16)
    return hi, (x - hi.astype(F32)).astype(BF16)


def _dot(a, b):
    return jnp.dot(a, b, preferred_element_type=F32)


def _dot_nt(a, b):
    return lax.dot_general(a, b, (((1,), (1,)), ((), ())), preferred_element_type=F32)


def _dot_tn(a, b):
    return lax.dot_general(a, b, (((0,), (0,)), ((), ())), preferred_element_type=F32)


def _dot_exact_by_f32(m, x):
    hi, lo = _split2(x)
    return _dot(m, hi) + _dot(m, lo)


def _ln_kernel(x_ref, g_ref, b_ref, hb_ref):
    hb_ref[...] = _layer_norm(x_ref[...], g_ref[...], b_ref[...]).astype(BF16)


def _ln_in(x2, g, b, tm):
    m, d = x2.shape
    rows = pl.BlockSpec((tm, d), lambda i: (i, 0))
    vec = pl.BlockSpec((1, d), lambda i: (0, 0))
    return pl.pallas_call(
        _ln_kernel,
        grid=(m // tm,),
        in_specs=[rows, vec, vec],
        out_specs=rows,
        out_shape=jax.ShapeDtypeStruct((m, d), BF16),
        compiler_params=_cparams(("parallel",)),
        name="ln_in",
    )(x2, g, b)


def _inproj_kernel(hb_ref, wr_ref, wa_ref, mu_ref, pr_ref, pa_ref, last_ref, *, n_f32_tiles, tiles_per_seq):
    i = pl.program_id(0)
    n = pl.program_id(1)

    @pl.when((i == 0) & (n == 0))
    def _():
        last_ref[...] = jnp.zeros_like(last_ref)

    @pl.when(n < n_f32_tiles)
    def _():
        tm = pr_ref.shape[0]
        p = _dot(hb_ref[...], wr_ref[...])
        slot = jnp.minimum(n, n_f32_tiles - 1)
        carried = jnp.where(i % tiles_per_seq == 0, 0.0, last_ref[slot])
        prev = pltpu.roll(p, shift=1, axis=0)
        prev = jnp.where(lax.broadcasted_iota(jnp.int32, (tm, 1), 0) == 0, carried, prev)
        last_ref[slot] = p[tm - 1:tm, :]
        pr_ref[...] = p + (prev - p) * mu_ref[...]

    @pl.when(n >= n_f32_tiles)
    def _():
        pa_ref[...] = _dot(hb_ref[...], wa_ref[...]).astype(BF16)


def _inproj(hb, w_rwkv, w_attn, mu, seq, tm, tn):
    m, d = hb.shape
    n_rwkv_cols, n_attn_cols = w_rwkv.shape[1], w_attn.shape[1]
    nf = n_rwkv_cols // tn
    kern = functools.partial(_inproj_kernel, n_f32_tiles=nf, tiles_per_seq=seq // tm)
    first = lambda i, n: (0, jnp.minimum(n, nf - 1))
    second = lambda i, n: (0, jnp.maximum(n - nf, 0))
    return pl.pallas_call(
        kern,
        grid=(m // tm, (n_rwkv_cols + n_attn_cols) // tn),
        in_specs=[
            pl.BlockSpec((tm, d), lambda i, n: (i, 0)),
            pl.BlockSpec((d, tn), first),
            pl.BlockSpec((d, tn), second),
            pl.BlockSpec((1, tn), first),
        ],
        out_specs=[
            pl.BlockSpec((tm, tn), lambda i, n: (i, jnp.minimum(n, nf - 1))),
            pl.BlockSpec((tm, tn), lambda i, n: (i, jnp.maximum(n - nf, 0))),
        ],
        out_shape=[
            jax.ShapeDtypeStruct((m, n_rwkv_cols), F32),
            jax.ShapeDtypeStruct((m, n_attn_cols), BF16),
        ],
        scratch_shapes=[pltpu.VMEM((nf, 1, tn), F32)],
        compiler_params=_cparams(("arbitrary", "arbitrary")),
        name="inproj",
    )(hb, w_rwkv, w_attn, mu)


def _wkv_kernel(pr_ref, pk_ref, pv_ref, pl_ref,
                w0_ref, a0_ref, kk_ref, ka_ref, rk_ref, lg_ref, lb_ref,
                wd_ref, wa_ref, wg_ref,
                y_ref,
                s_ref,
                r_s, lw_s, k_s, v_s, a_s, b_s, y_s, g_s, bo_s, zero_ref, ring64, ring128, ringg,
                *, n_pairs, tt):
    C = WKV_CHUNK
    t_idx = pl.program_id(2)

    @pl.when(t_idx == 0)
    def _():
        s_ref[...] = jnp.zeros_like(s_ref)

    lane = lax.broadcasted_iota(jnp.int32, (LANES, LANES), 1)
    sub = lax.broadcasted_iota(jnp.int32, (LANES, LANES), 0)
    head_ones = jnp.where((lane // 64) == (sub // 64), 1.0, 0.0).astype(BF16)

    def head_sum(x):
        hi = x.astype(BF16)
        lo = (x - hi.astype(F32)).astype(BF16)
        return _dot(hi, head_ones) + _dot(lo, head_ones)

    da = pl_ref[:, 0:LANES]
    th = jnp.tanh(da).astype(BF16)
    sg = jax.nn.sigmoid(pl_ref[:, LANES:]).astype(BF16)
    da = da.astype(BF16)
    for g in range(n_pairs):
        cs = slice(g * LANES, (g + 1) * LANES)
        rg, kg, vg = pr_ref[:, cs], pk_ref[:, cs], pv_ref[:, cs]
        pre = w0_ref[:, cs] + _dot(th, wd_ref[:, cs])
        softplus_neg = jnp.maximum(-pre, 0.0) + jnp.log(1.0 + jnp.exp(-jnp.abs(pre)))
        w_log = -softplus_neg - 0.5
        lw = -jnp.exp(w_log)
        a = jax.nn.sigmoid(a0_ref[:, cs] + _dot(da, wa_ref[:, cs]))
        gate = _dot(sg, wg_ref[:, cs])
        kk = kg * kk_ref[:, cs]
        kk = kk * lax.rsqrt(jnp.maximum(head_sum(kk * kk), 1e-24))
        k2 = kg * (1.0 + (a - 1.0) * ka_ref[:, cs])
        bonus = head_sum(rg * k2 * rk_ref[:, cs]) * vg
        r_s[:, cs] = rg
        lw_s[:, cs] = lw
        k_s[:, cs] = k2
        v_s[:, cs] = vg
        a_s[:, cs] = -kk
        b_s[:, cs] = kk * a
        g_s[:, cs] = gate
        bo_s[:, cs] = bonus

    ci = lax.broadcasted_iota(jnp.int32, (C, 2 * C), 0)
    cj = lax.broadcasted_iota(jnp.int32, (C, 2 * C), 1)
    left = cj < C
    strict = (cj % C) < ci
    incl = (cj % C) <= ci
    tri_incl = jnp.where(lax.broadcasted_iota(jnp.int32, (C, C), 1)
                         <= lax.broadcasted_iota(jnp.int32, (C, C), 0), 1.0, 0.0).astype(BF16)
    lane_c = lax.broadcasted_iota(jnp.int32, (C, LANES), 1)
    m0 = lane_c < 64
    eye = jnp.where(lane == sub, 1.0, 0.0).astype(F32)
    blockdiag = (lane // 64) == (sub // 64)

    csl = [slice(g * LANES, (g + 1) * LANES) for g in range(n_pairs)]
    P = range(n_pairs)
    cat0 = lambda *xs: jnp.concatenate(xs, axis=0)
    cat1 = lambda *xs: jnp.concatenate(xs, axis=1)
    bf = lambda x: x.astype(BF16)

    n_chunks = tt // C
    MM0, MM1, RTB, VB = range(4)
    BK, AK, ATB = range(3)
    ring64[...] = jnp.zeros_like(ring64)
    ring128[...] = jnp.zeros_like(ring128)
    ringg[...] = jnp.zeros_like(ringg)

    def stage1(c, out):
        rows = pl.ds(pl.multiple_of(c * C, C), C)
        slot = c % 4
        ld = lambda ref: [ref[rows, csl[g]] for g in P]
        rc, lwc, kc, vc, ac, bc = ld(r_s), ld(lw_s), ld(k_s), ld(v_s), ld(a_s), ld(b_s)
        cum = [_dot_exact_by_f32(tri_incl, lwc[g]) for g in P]
        yield
        last = [cum[g][C - 1:C, :] for g in P]
        rt = [rc[g] * jnp.exp(cum[g]) for g in P]
        at = [ac[g] * jnp.exp(cum[g] - lwc[g]) for g in P]
        ginv = [jnp.exp(-cum[g]) for g in P]
        btb = [bf(bc[g] * ginv[g]) for g in P]
        ktb = [bf(kc[g] * ginv[g]) for g in P]
        ghat = [jnp.exp(last[g] - cum[g]) for g in P]
        lhs0 = [bf(cat0(jnp.where(m0, at[g], 0.0), jnp.where(m0, rt[g], 0.0))) for g in P]
        lhs1 = [bf(cat0(jnp.where(m0, 0.0, at[g]), jnp.where(m0, 0.0, rt[g]))) for g in P]
        for g in P:
            ring128[slot, g, BK] = cat0(bf(bc[g] * ghat[g]), bf(kc[g] * ghat[g]))
            ring128[slot, g, ATB] = cat0(lhs0[g][:C], lhs1[g][:C])
            ring64[slot, g, VB] = bf(vc[g])
            ring64[slot, g, RTB] = bf(rt[g])
            ringg[slot, g, 0:1, :] = jnp.exp(last[g])
        x0 = [_dot_nt(lhs0[g], cat0(btb[g], ktb[g])) for g in P]
        x1 = [_dot_nt(lhs1[g], cat0(ktb[g], btb[g])) for g in P]
        yield
        n_bd = [cat0(jnp.where(left & strict, x0[g][:C], 0.0),
                     jnp.where((~left) & strict, x1[g][:C], 0.0)) for g in P]
        for g in P:
            ring128[slot, g, AK] = bf(cat0(jnp.where((~left) & strict, x0[g][:C], 0.0),
                                           jnp.where(left & strict, x1[g][:C], 0.0)))
            ring64[slot, g, MM0] = bf(jnp.where(incl, x0[g][C:], 0.0))
            ring64[slot, g, MM1] = bf(jnp.where(incl, x1[g][C:], 0.0))
        out["t"] = [eye + n_bd[g] for g in P]
        nb = [bf(n_bd[g]) for g in P]
        out["pw"] = [bf(_dot(nb[g], nb[g])) for g in P]

    def inverse_level(t, pw):
        res = [_dot(pw[g], cat1(pw[g], bf(t[g]))) for g in P]
        return [t[g] + res[g][:, LANES:] for g in P], [bf(res[g][:, :LANES]) for g in P]

    def stage2(inp, out):
        t, pw = inp["t"], inp["pw"]
        for level in range(3):
            t, pw = inverse_level(t, pw)
            if level < 2:
                yield
        out["t"], out["pw"] = t, pw

    def stage3(inp, c, out):
        slot = c % 4
        t, pw = inverse_level(inp["t"], inp["pw"])
        vb = [ring64[slot, g, VB] for g in P]
        av = [_dot(ring128[slot, g, AK], cat0(vb[g], vb[g])) for g in P]
        yield
        t = [t[g] + _dot(pw[g], bf(t[g])) for g in P]
        av = [cat0(jnp.where(m0, av[g][:C], 0.0), jnp.where(m0, 0.0, av[g][C:])) for g in P]
        yield
        wu = [_dot(bf(t[g]), cat1(ring128[slot, g, ATB], bf(av[g]))) for g in P]
        out["w"] = [bf(wu[g][:C, :LANES] + wu[g][C:, :LANES]) for g in P]
        out["u0"] = [wu[g][:C, LANES:] + wu[g][C:, LANES:] for g in P]

    def stage4(inp, c, valid):
        rows = pl.ds(pl.multiple_of(c * C, C), C)
        slot = c % 4
        vb = [ring64[slot, g, VB] for g in P]
        s = [s_ref[g] for g in P]
        sb = [bf(s[g]) for g in P]
        ub = [bf(_dot_nt(inp["w"][g], sb[g]) + inp["u0"][g]) for g in P]
        ys = [_dot_nt(ring64[slot, g, RTB], sb[g]) for g in P]
        yield
        uv = [cat0(ub[g], vb[g]) for g in P]
        y = [ys[g] + jnp.where(m0, _dot(ring64[slot, g, MM0], uv[g]),
                               _dot(ring64[slot, g, MM1], cat0(vb[g], ub[g]))) for g in P]
        upd = [_dot_tn(uv[g], ring128[slot, g, BK]) for g in P]
        yield
        for g in P:
            s_new = s[g] * ringg[slot, g, 0:1, :] + jnp.where(blockdiag, upd[g], 0.0)
            s_ref[g] = jnp.where(valid, s_new, s[g])
            y_s[rows, csl[g]] = y[g]

    def interleave(gens):
        live = list(gens)
        while live:
            still = []
            for gen in live:
                try:
                    next(gen)
                    still.append(gen)
                except StopIteration:
                    pass
            live = still

    def body(it, carry):
        o1, o2, o3 = carry
        n1, n2, n3 = {}, {}, {}
        interleave([stage4(o3, jnp.maximum(it - 3, 0), it >= 3),
                    stage3(o2, jnp.maximum(it - 2, 0), n3),
                    stage2(o1, n2),
                    stage1(jnp.minimum(it, n_chunks - 1), n1)])
        return n1, n2, n3

    zero_ref[...] = jnp.zeros_like(zero_ref)
    zb = lambda r: [zero_ref[0:r, :].astype(BF16) for _ in P]
    zf = lambda r: [zero_ref[0:r, :] for _ in P]
    front = lambda: dict(t=zf(2 * C), pw=zb(2 * C))
    lax.fori_loop(0, n_chunks + 3, body, (front(), front(), dict(w=zb(C), u0=zf(C))))

    for g in range(n_pairs):
        cs = slice(g * LANES, (g + 1) * LANES)
        y = y_s[:, cs]
        mean = head_sum(y) * (1.0 / 64.0)
        yc = y - mean
        var = head_sum(yc * yc) * (1.0 / 64.0)
        yn = yc * lax.rsqrt(var + GN_EPS) * lg_ref[:, cs] + lb_ref[:, cs]
        y_ref[:, cs] = ((yn + bo_s[:, cs]) * g_s[:, cs]).astype(BF16)


def _wkv(p_rwkv, prm, batch, seq, n_pairs, tt):
    m = p_rwkv.shape[0]
    c = prm["w0"].shape[1]
    gw = n_pairs * LANES
    n_col_blocks = c // gw
    nt = seq // tt
    low_w = LANES + prm["wg"].shape[0]

    def pspec(off):
        return pl.BlockSpec((tt, gw), lambda b, g, t: (b * nt + t, off * n_col_blocks + g))

    def vspec():
        return pl.BlockSpec((1, gw), lambda b, g, t: (0, g))

    def wspec(rows):
        return pl.BlockSpec((rows, gw), lambda b, g, t: (0, g))

    kern = functools.partial(_wkv_kernel, n_pairs=n_pairs, tt=tt)
    tile = pltpu.VMEM((tt, gw), F32)
    return pl.pallas_call(
        kern,
        grid=(batch, n_col_blocks, nt),
        in_specs=[
            pspec(0), pspec(1), pspec(2),
            pl.BlockSpec((tt, low_w), lambda b, g, t: (b * nt + t, (3 * c) // low_w)),
            vspec(), vspec(), vspec(), vspec(), vspec(), vspec(), vspec(),
            wspec(LANES), wspec(LANES), wspec(low_w - LANES),
        ],
        out_specs=pl.BlockSpec((tt, gw), lambda b, g, t: (b * nt + t, g)),
        out_shape=jax.ShapeDtypeStruct((m, c), BF16),
        scratch_shapes=[
            pltpu.VMEM((n_pairs, LANES, LANES), F32),
            tile, tile, tile, tile, tile, tile, tile, tile, tile,
            pltpu.VMEM((LANES, LANES), F32),
            pltpu.VMEM((4, n_pairs, 4, WKV_CHUNK, LANES), BF16),
            pltpu.VMEM((4, n_pairs, 3, 2 * WKV_CHUNK, LANES), BF16),
            pltpu.VMEM((4, n_pairs, 8, LANES), F32),
        ],
        compiler_params=_cparams(("parallel", "parallel", "arbitrary")),
        name="wkv7",
    )(p_rwkv, p_rwkv, p_rwkv, p_rwkv,
      prm["w0"], prm["a0"], prm["k_k"], prm["k_a"], prm["r_k"], prm["lnx_g"], prm["lnx_b"],
      prm["wd"], prm["wa"], prm["wg"])


def _sb_kernel(q_ref, k_ref, v_ref, o_ref, acc_ref, right_ref, *, tq, scale):
    i = pl.program_id(1)
    row = lax.broadcasted_iota(jnp.int32, (tq, tq), 0)
    col = lax.broadcasted_iota(jnp.int32, (tq, tq), 1)
    r2 = lax.broadcasted_iota(jnp.int32, (tq, 2 * tq), 0)
    c2 = lax.broadcasted_iota(jnp.int32, (tq, 2 * tq), 1)
    after_and_total = jnp.where((c2 >= tq) | (r2 > c2), 1.0, 0.0).astype(BF16)
    diag = col < row

    heads = range(SB_HEADS)
    hsl = [slice(h * LANES, (h + 1) * LANES) for h in heads]

    def blocks(js, first):
        nb = len(js)
        it = [(b, h) for b in range(nb) for h in heads]
        on_diag = lambda p: first and p[0] == 0
        ks = [pl.ds(pl.multiple_of(j * tq, tq), tq) for j in js]
        z = {p: _dot_nt(q_ref[:, hsl[p[1]]], k_ref[ks[p[0]], hsl[p[1]]]) * scale for p in it}
        sp = {p: jnp.maximum(z[p], 0.0) + jnp.log(1.0 + jnp.exp(-jnp.abs(z[p]))) for p in it}
        log_keep = {p: jnp.where(diag, -sp[p], 0.0) if on_diag(p) else -sp[p] for p in it}
        hi = {p: log_keep[p].astype(BF16) for p in it}
        lo = {p: (log_keep[p] - hi[p].astype(F32)).astype(BF16) for p in it}
        sums = {p: _dot(hi[p], after_and_total) + _dot(lo[p], after_and_total) for p in it}
        right = {}
        for h in heads:
            run = None if first else right_ref[h]
            for b in range(nb):
                right[(b, h)] = run
                tot = sums[(b, h)][:, tq:]
                run = tot if run is None else run + tot
            right[("end", h)] = run
        after = {p: sums[p][:, :tq] if right[p] is None else sums[p][:, :tq] + right[p] for p in it}
        attn = {p: jnp.exp(z[p] - sp[p] + after[p]) for p in it}
        attn = {p: jnp.where(diag, attn[p], 0.0) if on_diag(p) else attn[p] for p in it}
        pv = {p: _dot(attn[p].astype(BF16), v_ref[ks[p[0]], hsl[p[1]]]) for p in it}
        for h in heads:
            tot = pv[(0, h)]
            for b in range(1, nb):
                tot = tot + pv[(b, h)]
            if first:
                acc_ref[h] = tot
            else:
                acc_ref[h] += tot
            right_ref[h] = right[("end", h)]

    per = SB_BLOCKS_PER_ITER
    first_size = i % per + 1
    for size in range(1, per + 1):
        @pl.when(first_size == size)
        def _(size=size):
            blocks([i - b for b in range(size)], True)

    def body(jj, _):
        j = i - first_size - per * jj
        blocks([j - b for b in range(per)], False)
        return 0

    lax.fori_loop(0, (i + 1 - first_size) // per, body, 0)

    for h in heads:
        o_ref[:, hsl[h]] = acc_ref[h].astype(BF16)


def _sb_attn(p_attn, batch, seq, tq):
    m = p_attn.shape[0]
    w = SB_HEADS * LANES
    nq = seq // tq
    kern = functools.partial(_sb_kernel, tq=tq, scale=LANES ** -0.5)
    return pl.pallas_call(
        kern,
        grid=(batch, nq),
        in_specs=[
            pl.BlockSpec((tq, w), lambda b, i: (b * nq + i, 0)),
            pl.BlockSpec((seq, w), lambda b, i: (b, 1)),
            pl.BlockSpec((seq, w), lambda b, i: (b, 2)),
        ],
        out_specs=pl.BlockSpec((tq, w), lambda b, i: (b * nq + i, 0)),
        out_shape=jax.ShapeDtypeStruct((m, w), BF16),
        scratch_shapes=[pltpu.VMEM((SB_HEADS, tq, tq), F32), pltpu.VMEM((SB_HEADS, tq, tq), F32)],
        compiler_params=_cparams(("parallel", "arbitrary")),
        name="sb_attn",
    )(p_attn, p_attn, p_attn)


def _mem_kv_kernel(m_ref, w_ref, o_ref):
    o_ref[...] = _dot(m_ref[...].astype(BF16), w_ref[...]).astype(BF16)


def _mem_kv(mem2, w_bf16, tm):
    m, d = mem2.shape
    n = w_bf16.shape[1]
    return pl.pallas_call(
        _mem_kv_kernel,
        grid=(m // tm,),
        in_specs=[pl.BlockSpec((tm, d), lambda i: (i, 0)),
                  pl.BlockSpec((d, n), lambda i: (0, 0))],
        out_specs=pl.BlockSpec((tm, n), lambda i: (i, 0)),
        out_shape=jax.ShapeDtypeStruct((m, n), BF16),
        compiler_params=_cparams(("parallel",)),
        name="mem_kv",
    )(mem2, w_bf16)


def _mem_attn_kernel(q_ref, k_ref, v_ref, o_ref, *, scale):
    for h in range(MEM_HEADS):
        hs = slice(h * LANES, (h + 1) * LANES)
        s = _dot_nt(q_ref[:, hs], k_ref[:, hs]) * scale
        s = s - jnp.max(s, axis=-1, keepdims=True)
        e = jnp.exp(s)
        p = e / jnp.sum(e, axis=-1, keepdims=True)
        o_ref[:, hs] = _dot(p.astype(BF16), v_ref[:, hs]).astype(BF16)


def _mem_attn(p_attn, kv, batch, seq, mem_len, tq):
    m = p_attn.shape[0]
    w = MEM_HEADS * LANES
    nq = seq // tq
    kern = functools.partial(_mem_attn_kernel, scale=LANES ** -0.5)
    return pl.pallas_call(
        kern,
        grid=(batch, nq),
        in_specs=[
            pl.BlockSpec((tq, w), lambda b, i: (b * nq + i, 3)),
            pl.BlockSpec((mem_len, w), lambda b, i: (b, 0)),
            pl.BlockSpec((mem_len, w), lambda b, i: (b, 1)),
        ],
        out_specs=pl.BlockSpec((tq, w), lambda b, i: (b * nq + i, 0)),
        out_shape=jax.ShapeDtypeStruct((m, w), BF16),
        compiler_params=_cparams(("parallel", "parallel")),
        name="mem_attn",
    )(p_attn, kv, kv)


def _outproj_kernel(yr_ref, ys_ref, ym_ref, x_ref, g0_ref, b0_ref, w_ref, g_ref, b_ref, rw_ref, rb_ref,
                    h1_ref, lg_ref):
    c0 = yr_ref.shape[1]
    c1 = c0 + ys_ref.shape[1]
    mix = (_dot(yr_ref[...], w_ref[0:c0, :]) + _dot(ys_ref[...], w_ref[c0:c1, :])
           + _dot(ym_ref[...], w_ref[c1:, :]))
    h = _layer_norm(x_ref[...], g0_ref[...], b0_ref[...])
    h1 = _layer_norm(DEEPNORM_ALPHA * h + mix, g_ref[...], b_ref[...])
    h1_ref[...] = h1
    hi = h1.astype(BF16)
    lo = (h1 - hi.astype(F32)).astype(BF16)
    both = _dot(hi, rw_ref[...])
    lg_ref[...] = both[:, :LANES] + both[:, LANES:] + _dot(lo, rw_ref[:, :LANES]) + rb_ref[...]


def _outproj(y_r, y_s, y_m, x2, g0, b0, w, g, b, r_w, r_b, tm):
    m, d = x2.shape
    full = lambda a: pl.BlockSpec(a.shape, lambda i: (0, 0))
    rows = lambda a: pl.BlockSpec((tm, a.shape[1]), lambda i: (i, 0))
    return pl.pallas_call(
        _outproj_kernel,
        grid=(m // tm,),
        in_specs=[rows(y_r), rows(y_s), rows(y_m), rows(x2), full(g0), full(b0), full(w), full(g), full(b),
                  full(r_w), full(r_b)],
        out_specs=[pl.BlockSpec((tm, d), lambda i: (i, 0)),
                   pl.BlockSpec((tm, LANES), lambda i: (i, 0))],
        out_shape=[jax.ShapeDtypeStruct((m, d), F32),
                   jax.ShapeDtypeStruct((m, LANES), F32)],
        compiler_params=_cparams(("parallel",)),
        name="outproj",
    )(y_r, y_s, y_m, x2, g0, b0, w, g, b, r_w, r_b)


def _route_kernel(lg_ref, id_ref, wt_ref, *, n_groups, per_group):
    lg = lg_ref[...]
    lane_i = lax.broadcasted_iota(jnp.int32, lg.shape, 1)
    lane = lane_i.astype(F32)
    neg = jnp.float32(-jnp.inf)
    big = jnp.float32(2 ** 20)

    def first_max(vals):
        mx = jnp.max(vals, axis=-1, keepdims=True)
        idx = jnp.min(jnp.where(vals == mx, lane, big), axis=-1, keepdims=True)
        return mx, idx

    is_group = lane < n_groups
    gmax, gidx = first_max(jnp.where(is_group, lg, neg))
    gsum = jnp.sum(jnp.where(is_group, jnp.exp(lg - gmax), 0.0), axis=-1, keepdims=True)
    group_w = 1.0 / gsum
    lo = n_groups + gidx * per_group
    in_group = (lane >= lo) & (lane < lo + per_group)
    v1, i1 = first_max(jnp.where(in_group, lg, neg))
    v2, i2 = first_max(jnp.where(in_group & (lane != i1), lg, neg))
    e2 = jnp.exp(v2 - v1)
    w1 = group_w / (1.0 + e2)
    w2 = group_w * e2 / (1.0 + e2)
    ids = jnp.where(lane_i == 0, i1 - n_groups, jnp.where(lane_i == 1, i2 - n_groups, 0.0))
    id_ref[...] = ids.T[0:8, :].astype(jnp.int32)
    wt_ref[...] = jnp.where(lane_i == 0, w1, jnp.where(lane_i == 1, w2, 0.0))


def _route(logits, n_groups, per_group, tm):
    m = logits.shape[0]
    kern = functools.partial(_route_kernel, n_groups=n_groups, per_group=per_group)
    spec = pl.BlockSpec((tm, LANES), lambda i: (i, 0))
    return pl.pallas_call(
        kern,
        grid=(m // tm,),
        in_specs=[spec],
        out_specs=[pl.BlockSpec((8, tm), lambda i: (0, i)), spec],
        out_shape=[jax.ShapeDtypeStruct((8, m), jnp.int32),
                   jax.ShapeDtypeStruct((m, LANES), F32)],
        compiler_params=_cparams(("parallel",)),
        name="route",
    )(logits)


def _plan_kernel(id_ref, pos_ref, meta_ref, cnt_ref, base_ref, start_ref, *, tb, rows, n_experts, nb):
    phase = pl.program_id(0)
    j = pl.program_id(1)
    sub = lax.broadcasted_iota(jnp.int32, (LANES, tb), 0)
    e1 = id_ref[0:1, :]
    e2 = id_ref[1:2, :]
    hit1 = jnp.where(sub == e1, 1.0, 0.0)
    hit2 = jnp.where(sub == e2, 1.0, 0.0)
    hits = hit1 + hit2
    per_expert = jnp.sum(hits, axis=1, keepdims=True)

    @pl.when((phase == 0) & (j == 0))
    def _():
        cnt_ref[...] = jnp.zeros_like(cnt_ref)

    @pl.when(phase == 0)
    def _():
        cnt_ref[...] += per_expert

    sq_r = lax.broadcasted_iota(jnp.int32, (LANES, LANES), 0)
    sq_c = lax.broadcasted_iota(jnp.int32, (LANES, LANES), 1)

    @pl.when((phase == 1) & (j == 0))
    def _():
        n_blk = jnp.floor((cnt_ref[...] + (rows - 1)) * (1.0 / rows))
        before = jnp.where(sq_c < sq_r, 1.0, 0.0).astype(BF16)
        blk_start = _dot(before, jnp.broadcast_to(n_blk, (LANES, LANES)).astype(BF16))
        start_ref[...] = blk_start[:, 0:1]
        base_ref[...] = jnp.zeros_like(base_ref)
        blk_end = blk_start + n_blk
        n_used = jnp.sum(jnp.where(sq_r[:, 0:1] < n_experts, n_blk, 0.0), axis=0, keepdims=True)
        owner = jnp.sum(jnp.where((sq_r < n_experts) & (blk_end <= sq_c.astype(F32)), 1.0, 0.0),
                        axis=0, keepdims=True)
        last_owner = jnp.max(jnp.where(n_blk > 0.0, sq_r[:, 0:1].astype(F32), 0.0), axis=0, keepdims=True)
        blk = sq_c[0:1, :].astype(F32)
        in_use = blk < n_used
        owner = jnp.where(in_use, owner, last_owner)
        mine = sq_r.astype(F32) == owner
        cnt_o = jnp.sum(jnp.where(mine, cnt_ref[...], 0.0), axis=0, keepdims=True)
        start_o = jnp.sum(jnp.where(mine, blk_start, 0.0), axis=0, keepdims=True)
        valid = jnp.clip(cnt_o - (blk - start_o) * rows, 0.0, float(rows))
        valid = jnp.where(in_use, valid, 0.0)
        row8 = lax.broadcasted_iota(jnp.int32, (8, LANES), 0)
        meta = jnp.where(row8 == 0, owner, jnp.where(row8 == 1, valid, jnp.where(row8 == 2, n_used, 0.0)))
        meta_ref[...] = meta.astype(jnp.int32)

    @pl.when(phase == 1)
    def _():
        tr = lax.broadcasted_iota(jnp.int32, (tb, tb), 0)
        tc = lax.broadcasted_iota(jnp.int32, (tb, tb), 1)
        earlier = jnp.where(tr < tc, 1.0, 0.0).astype(BF16)
        seen = _dot(hits.astype(BF16), earlier)
        slot = start_ref[...] * rows + base_ref[...] + seen
        p1 = jnp.sum(hit1 * slot, axis=0, keepdims=True)
        p2 = jnp.sum(hit2 * slot, axis=0, keepdims=True)
        row8 = lax.broadcasted_iota(jnp.int32, (8, tb), 0)
        pos_ref[...] = jnp.where(row8 == 0, p1, jnp.where(row8 == 1, p2, 0.0)).astype(jnp.int32)
        base_ref[...] += per_expert


def _plan(ids_t, n_experts, rows, tb):
    m = ids_t.shape[1]
    nb = m // tb
    kern = functools.partial(_plan_kernel, tb=tb, rows=rows, n_experts=n_experts, nb=nb)
    col = pltpu.VMEM((LANES, 1), F32)
    return pl.pallas_call(
        kern,
        grid=(2, nb),
        in_specs=[pl.BlockSpec((8, tb), lambda p, j: (0, j))],
        out_specs=[pl.BlockSpec((8, tb), lambda p, j: (0, j * p)),
                   pl.BlockSpec((8, LANES), lambda p, j: (0, 0))],
        out_shape=[jax.ShapeDtypeStruct((8, m), jnp.int32),
                   jax.ShapeDtypeStruct((8, LANES), jnp.int32)],
        scratch_shapes=[col, col, col],
        compiler_params=_cparams(("arbitrary", "arbitrary")),
        name="moe_plan",
    )(ids_t)


def _for_range(lo, hi, body):
    full = (hi - lo) // ROW_GROUP
    first = lo // ROW_GROUP

    def group(g, _):
        for u in range(ROW_GROUP):
            body(first + g, u)
        return 0

    lax.fori_loop(0, full, group, 0)

    def one(r, _):
        body(r // ROW_GROUP, r % ROW_GROUP)
        return 0

    lax.fori_loop(lo + full * ROW_GROUP, hi, one, 0)


def _row_in(src_hbm, dst_vmem, sem, src_row, tile, sub):
    return pltpu.make_async_copy(src_hbm.at[pl.ds(src_row, 1)], dst_vmem.at[tile, pl.ds(sub, 1)], sem)


def _row_out(src_vmem, dst_hbm, sem, tile, sub, dst_row):
    return pltpu.make_async_copy(src_vmem.at[tile, pl.ds(sub, 1)], dst_hbm.at[pl.ds(dst_row, 1)], sem)


def _gather_cparams(sem):
    return pltpu.CompilerParams(dimension_semantics=sem, vmem_limit_bytes=VMEM_LIMIT,
                                disable_bounds_checks=True)


def _ffn_kernel(asg_ref, be_ref, nv_ref, nu_ref, h_hbm, wg_ref, wu_ref, wd_ref, y_hbm,
                xf_ref, xb_ref, acc_ref, sem_in, sem_out, *, rows, sub, nj, n_tok, n_blocks):
    i = pl.program_id(0)
    j = pl.program_id(1)
    n_used = nu_ref[0]
    used = i < n_used
    slot = i % 2

    def gather(block, lo, hi, buf):
        def body(tile, sub_row):
            a = asg_ref[block * rows + tile * ROW_GROUP + sub_row]
            tok = jnp.where(a >= n_tok, a - n_tok, a)
            _row_in(h_hbm, xf_ref.at[buf], sem_in.at[buf], tok, tile, sub_row).start()

        _for_range(lo, hi, body)

    def gather_wait(block, buf):
        _for_range(0, nv_ref[block],
                   lambda tile, sub_row: _row_in(h_hbm, xf_ref.at[buf], sem_in.at[buf], 0, tile, sub_row).wait())

    def emit(block, buf):
        def body(tile, sub_row):
            a = asg_ref[block * rows + tile * ROW_GROUP + sub_row]
            _row_out(acc_ref.at[buf], y_hbm, sem_out.at[buf], tile, sub_row, a).start()

        _for_range(0, nv_ref[block], body)

    def emit_wait(block, buf):
        _for_range(0, nv_ref[block],
                   lambda tile, sub_row: _row_out(acc_ref.at[buf], y_hbm, sem_out.at[buf], tile, sub_row, 0).wait())

    def as_rows(x):
        return x.reshape(x.shape[0] * ROW_GROUP, x.shape[2])

    def as_tiles(x):
        return x.reshape(x.shape[0] // ROW_GROUP, ROW_GROUP, x.shape[1])

    @pl.when((i == 0) & (j == 0))
    def _():
        xf_ref[...] = jnp.zeros_like(xf_ref)
        gather(0, 0, nv_ref[0], 0)

    @pl.when(used & (j == 0))
    def _():
        gather_wait(i, slot)

    @pl.when(i + 1 < n_used)
    def _():
        nxt = nv_ref[i + 1]
        q = rows // nj
        gather(i + 1, jnp.minimum(j * q, nxt), jnp.minimum((j + 1) * q, nxt), 1 - slot)

    n_sub = (nv_ref[i] + sub - 1) // sub
    for n in range(1, rows // sub + 1):
        @pl.when(used & (n_sub == n))
        def _(n=n):
            r = n * sub
            nt = r // ROW_GROUP

            @pl.when(j == 0)
            def _():
                xb_ref[0:r, :] = as_rows(xf_ref[slot, 0:nt]).astype(BF16)
                acc_ref[slot, 0:nt] = jnp.zeros((nt,) + acc_ref.shape[2:], F32)

            xb = xb_ref[0:r, :]
            gate = _dot(xb, wg_ref[...].astype(BF16))
            up = _dot(xb, wu_ref[...].astype(BF16))
            hid = (gate * jax.nn.sigmoid(gate)) * up
            acc_ref[slot, 0:nt] += as_tiles(_dot(hid.astype(BF16), wd_ref[...].astype(BF16)))

    @pl.when((j == nj - 1) & (i >= 1) & (i <= n_used))
    def _():
        emit_wait(i - 1, 1 - slot)

    @pl.when((j == nj - 1) & used)
    def _():
        emit(i, slot)

    @pl.when((j == nj - 1) & used & (i == n_blocks - 1))
    def _():
        emit_wait(i, slot)


def _moe_ffn(slot_asg, block_expert, n_valid, n_used, h1, w_gate, w_up, w_down, n_blocks, rows, tf):
    n_tok, d = h1.shape
    de = w_gate.shape[2]
    nj = de // tf
    kern = functools.partial(_ffn_kernel, rows=rows, sub=MOE_SUB, nj=nj, n_tok=n_tok, n_blocks=n_blocks)

    def jidx(i, j, nu):
        return jnp.where(i < nu[0], j, nj - 1)

    return pl.pallas_call(
        kern,
        grid_spec=pltpu.PrefetchScalarGridSpec(
            num_scalar_prefetch=4,
            grid=(n_blocks, nj),
            in_specs=[
                pl.BlockSpec(memory_space=pl.ANY),
                pl.BlockSpec((None, d, tf), lambda i, j, asg, be, nv, nu: (be[i], 0, jidx(i, j, nu))),
                pl.BlockSpec((None, d, tf), lambda i, j, asg, be, nv, nu: (be[i], 0, jidx(i, j, nu))),
                pl.BlockSpec((None, tf, d), lambda i, j, asg, be, nv, nu: (be[i], jidx(i, j, nu), 0)),
            ],
            out_specs=pl.BlockSpec(memory_space=pl.ANY),
            scratch_shapes=[pltpu.VMEM((2, rows // ROW_GROUP, ROW_GROUP, d), F32),
                            pltpu.VMEM((rows, d), BF16),
                            pltpu.VMEM((2, rows // ROW_GROUP, ROW_GROUP, d), F32),
                            pltpu.SemaphoreType.DMA((2,)),
                            pltpu.SemaphoreType.DMA((2,))],
        ),
        out_shape=jax.ShapeDtypeStruct((2 * n_tok, d), F32),
        compiler_params=_gather_cparams(("arbitrary", "arbitrary")),
        name="moe_ffn",
    )(slot_asg, block_expert, n_valid, n_used, h1, w_gate, w_up, w_down)


def _combine_kernel(h_ref, y0_ref, y1_ref, wt_ref, g_ref, b_ref, o_ref):
    wt = wt_ref[...]
    ffn = y0_ref[...] * wt[:, 0:1] + y1_ref[...] * wt[:, 1:2]
    o_ref[...] = _layer_norm(DEEPNORM_ALPHA * h_ref[...] + ffn, g_ref[...], b_ref[...])


def _combine(h1, ys, wts, g, b, tm):
    m, d = h1.shape
    nt = m // tm
    rows = pl.BlockSpec((tm, d), lambda i: (i, 0))
    vec = pl.BlockSpec((1, d), lambda i: (0, 0))
    return pl.pallas_call(
        _combine_kernel,
        grid=(nt,),
        in_specs=[rows, rows, pl.BlockSpec((tm, d), lambda i: (nt + i, 0)),
                  pl.BlockSpec((tm, LANES), lambda i: (i, 0)), vec, vec],
        out_specs=rows,
        out_shape=jax.ShapeDtypeStruct((m, d), F32),
        compiler_params=_cparams(("parallel",)),
        name="moe_combine",
    )(h1, ys, ys, wts, g, b)


def _dispatch_plan(ids_t, n_experts, rows):
    m = ids_t.shape[1]
    n_blocks = -(-(2 * m) // rows) + n_experts
    assert n_blocks <= LANES
    pos_t, meta = _plan(ids_t, n_experts, rows, _pick(m, 512))
    pos = pos_t[:2].reshape(-1)
    slot_asg = jnp.zeros((n_blocks * rows,), jnp.int32).at[pos].set(jnp.arange(2 * m, dtype=jnp.int32))
    return slot_asg, meta[2, :1], meta[0, :n_blocks], meta[1, :n_blocks], n_blocks


def _pick(n, pref):
    t = min(pref, n)
    while n % t:
        t //= 2
    return t


def kernel(x, mem, ln_in_g, ln_in_b, w_in, tshift_mu, w0, w_decay_up, a0, w_a_up, w_g_up, k_k, k_a, r_k,
           lnx_g, lnx_b, w_mem_kv, w_out, ln1_g, ln1_b, router_group, router_group_b, router_expert,
           router_expert_b, w_e_gate, w_e_up, w_e_down, ln2_g, ln2_b):
    batch, seq, d = x.shape
    mem_len = mem.shape[1]
    m = batch * seq
    c = w0.shape[1]
    dr, ar, gr = w_decay_up.shape[1], w_a_up.shape[1], w_g_up.shape[1]
    rwkv_cols = 3 * c + dr + ar + gr
    sb_w = SB_HEADS * LANES
    mem_w = MEM_HEADS * LANES
    assert dr + ar == LANES and c % (2 * LANES) == 0 and w_in.shape[0] == DEPTH
    assert w_in.shape[2] == rwkv_cols + 3 * sb_w + mem_w
    n_experts = router_expert.shape[2]
    row = lambda a: a.reshape(1, -1)

    tn = 512
    low_w = -(-(dr + ar + gr) // LANES) * LANES
    rw_pad = -(-(3 * c + low_w) // tn) * tn
    wi = w_in[0]
    w_rwkv = jnp.pad(wi[:, :rwkv_cols].astype(BF16), ((0, 0), (0, rw_pad - rwkv_cols)))
    w_attn = wi[:, rwkv_cols:].astype(BF16)
    mu = row(jnp.pad(tshift_mu[0], (0, rw_pad - rwkv_cols)))
    lp = low_w - (dr + ar + gr)
    prm = dict(
        w0=row(w0[0]), a0=row(a0[0]), k_k=row(k_k[0]), k_a=row(k_a[0]), r_k=row(r_k[0]),
        lnx_g=row(lnx_g[0]), lnx_b=row(lnx_b[0]),
        wd=jnp.concatenate([w_decay_up[0], jnp.zeros((ar, c), F32)], axis=0).astype(BF16),
        wa=jnp.concatenate([jnp.zeros((dr, c), F32), w_a_up[0]], axis=0).astype(BF16),
        wg=jnp.concatenate([w_g_up[0], jnp.zeros((lp, c), F32)], axis=0).astype(BF16),
    )
    assert (3 * c) % low_w == 0

    x2 = x.reshape(m, d)
    hb = _ln_in(x2, row(ln_in_g), row(ln_in_b), _pick(m, 256))
    p_rwkv, p_attn = _inproj(hb, w_rwkv, w_attn, mu, seq, _pick(seq, 1024), tn)

    y_rwkv = _wkv(p_rwkv, prm, batch, seq, n_pairs=4, tt=_pick(seq, 1024))
    y_sb = _sb_attn(p_attn, batch, seq, tq=LANES)
    kv = _mem_kv(mem.reshape(batch * mem_len, d), w_mem_kv[0].astype(BF16), _pick(batch * mem_len, 256))
    y_mem = _mem_attn(p_attn, kv, batch, seq, mem_len, tq=_pick(seq, 512))

    wo = w_out[0].astype(BF16)
    r_w = jnp.concatenate([router_group[0], router_expert[0],
                           jnp.zeros((d, LANES - N_GROUPS - n_experts), F32)], axis=1)
    r_b = jnp.concatenate([router_group_b[0], router_expert_b[0],
                           jnp.zeros((LANES - N_GROUPS - n_experts,), F32)]).reshape(1, LANES)
    r_hi = r_w.astype(BF16)
    r_w2 = jnp.concatenate([r_hi, (r_w - r_hi.astype(F32)).astype(BF16)], axis=1)
    h1, logits = _outproj(y_rwkv, y_sb, y_mem, x2, row(ln_in_g), row(ln_in_b), wo,
                          row(ln1_g[0]), row(ln1_b[0]), r_w2, r_b, _pick(m, 512))

    ids_t, wts = _route(logits, N_GROUPS, n_experts // N_GROUPS, _pick(m, 512))
    slot_asg, n_used, block_expert, n_valid, n_blocks = _dispatch_plan(ids_t, n_experts, MOE_ROWS)
    ys = _moe_ffn(slot_asg, block_expert, n_valid, n_used, h1, w_e_gate[0], w_e_up[0], w_e_down[0],
                  n_blocks, MOE_ROWS, tf=512)
    out = _combine(h1, ys, wts, row(ln2_g[0]), row(ln2_b[0]), _pick(m, 256))
    return out.reshape(batch, seq, d)
```

```python
import functools

import jax
import jax.numpy as jnp
from jax import lax
from jax.experimental import pallas as pl
from jax.experimental.pallas import tpu as pltpu

F32 = jnp.float32
BF16 = jnp.bfloat16

SB_HEADS = 4
MEM_HEADS = 4
N_GROUPS = 8
DEPTH = 1
DEEPNORM_ALPHA = (2.0 * DEPTH) ** 0.25
LN_EPS = 1e-5
GN_EPS = 64e-5

LANES = 128
WKV_CHUNK = 64
MOE_ROWS = 512
MOE_SUB = 64
ROW_GROUP = 8
SB_BLOCKS_PER_ITER = 4
VMEM_LIMIT = 56 * 1024 * 1024


def _cparams(sem):
    return pltpu.CompilerParams(dimension_semantics=sem, vmem_limit_bytes=VMEM_LIMIT)


def _layer_norm(x, g, b):
    mu = jnp.mean(x, axis=-1, keepdims=True)
    xc = x - mu
    var = jnp.mean(xc * xc, axis=-1, keepdims=True)
    return xc * lax.rsqrt(var + LN_EPS) * g + b


def _split2(x):
    hi = x.astype(BF16)
    return hi, (x - hi.astype(F32)).astype(BF16)


def _dot(a, b):
    return jnp.dot(a, b, preferred_element_type=F32)


def _dot_nt(a, b):
    return lax.dot_general(a, b, (((1,), (1,)), ((), ())), preferred_element_type=F32)


def _dot_tn(a, b):
    return lax.dot_general(a, b, (((0,), (0,)), ((), ())), preferred_element_type=F32)


def _dot_exact_by_f32(m, x):
    hi, lo = _split2(x)
    return _dot(m, hi) + _dot(m, lo)


def _ln_kernel(x_ref, g_ref, b_ref, hb_ref):
    hb_ref[...] = _layer_norm(x_ref[...], g_ref[...], b_ref[...]).astype(BF16)


def _ln_in(x2, g, b, tm):
    m, d = x2.shape
    rows = pl.BlockSpec((tm, d), lambda i: (i, 0))
    vec = pl.BlockSpec((1, d), lambda i: (0, 0))
    return pl.pallas_call(
        _ln_kernel,
        grid=(m // tm,),
        in_specs=[rows, vec, vec],
        out_specs=rows,
        out_shape=jax.ShapeDtypeStruct((m, d), BF16),
        compiler_params=_cparams(("parallel",)),
        name="ln_in",
    )(x2, g, b)


def _inproj_kernel(hb_ref, wr_ref, wa_ref, mu_ref, pr_ref, pa_ref, last_ref, *, n_f32_tiles, tiles_per_seq):
    i = pl.program_id(0)
    n = pl.program_id(1)

    @pl.when((i == 0) & (n == 0))
    def _():
        last_ref[...] = jnp.zeros_like(last_ref)

    @pl.when(n < n_f32_tiles)
    def _():
        tm = pr_ref.shape[0]
        p = _dot(hb_ref[...], wr_ref[...])
        slot = jnp.minimum(n, n_f32_tiles - 1)
        carried = jnp.where(i % tiles_per_seq == 0, 0.0, last_ref[slot])
        prev = pltpu.roll(p, shift=1, axis=0)
        prev = jnp.where(lax.broadcasted_iota(jnp.int32, (tm, 1), 0) == 0, carried, prev)
        last_ref[slot] = p[tm - 1:tm, :]
        pr_ref[...] = p + (prev - p) * mu_ref[...]

    @pl.when(n >= n_f32_tiles)
    def _():
        pa_ref[...] = _dot(hb_ref[...], wa_ref[...]).astype(BF16)


def _inproj(hb, w_rwkv, w_attn, mu, seq, tm, tn):
    m, d = hb.shape
    n_rwkv_cols, n_attn_cols = w_rwkv.shape[1], w_attn.shape[1]
    nf = n_rwkv_cols // tn
    kern = functools.partial(_inproj_kernel, n_f32_tiles=nf, tiles_per_seq=seq // tm)
    first = lambda i, n: (0, jnp.minimum(n, nf - 1))
    second = lambda i, n: (0, jnp.maximum(n - nf, 0))
    return pl.pallas_call(
        kern,
        grid=(m // tm, (n_rwkv_cols + n_attn_cols) // tn),
        in_specs=[
            pl.BlockSpec((tm, d), lambda i, n: (i, 0)),
            pl.BlockSpec((d, tn), first),
            pl.BlockSpec((d, tn), second),
            pl.BlockSpec((1, tn), first),
        ],
        out_specs=[
            pl.BlockSpec((tm, tn), lambda i, n: (i, jnp.minimum(n, nf - 1))),
            pl.BlockSpec((tm, tn), lambda i, n: (i, jnp.maximum(n - nf, 0))),
        ],
        out_shape=[
            jax.ShapeDtypeStruct((m, n_rwkv_cols), F32),
            jax.ShapeDtypeStruct((m, n_attn_cols), BF16),
        ],
        scratch_shapes=[pltpu.VMEM((nf, 1, tn), F32)],
        compiler_params=_cparams(("arbitrary", "arbitrary")),
        name="inproj",
    )(hb, w_rwkv, w_attn, mu)


def _wkv_kernel(pr_ref, pk_ref, pv_ref, pl_ref,
                w0_ref, a0_ref, kk_ref, ka_ref, rk_ref, lg_ref, lb_ref,
                wd_ref, wa_ref, wg_ref,
                y_ref,
                s_ref,
                r_s, lw_s, k_s, v_s, a_s, b_s, y_s, g_s, bo_s, zero_ref, ring64, ring128, ringg,
                *, n_pairs, tt):
    C = WKV_CHUNK
    t_idx = pl.program_id(2)

    @pl.when(t_idx == 0)
    def _():
        s_ref[...] = jnp.zeros_like(s_ref)

    lane = lax.broadcasted_iota(jnp.int32, (LANES, LANES), 1)
    sub = lax.broadcasted_iota(jnp.int32, (LANES, LANES), 0)
    head_ones = jnp.where((lane // 64) == (sub // 64), 1.0, 0.0).astype(BF16)

    def head_sum(x):
        hi = x.astype(BF16)
        lo = (x - hi.astype(F32)).astype(BF16)
        return _dot(hi, head_ones) + _dot(lo, head_ones)

    da = pl_ref[:, 0:LANES]
    th = jnp.tanh(da).astype(BF16)
    sg = jax.nn.sigmoid(pl_ref[:, LANES:]).astype(BF16)
    da = da.astype(BF16)
    for g in range(n_pairs):
        cs = slice(g * LANES, (g + 1) * LANES)
        rg, kg, vg = pr_ref[:, cs], pk_ref[:, cs], pv_ref[:, cs]
        pre = w0_ref[:, cs] + _dot(th, wd_ref[:, cs])
        softplus_neg = jnp.maximum(-pre, 0.0) + jnp.log(1.0 + jnp.exp(-jnp.abs(pre)))
        w_log = -softplus_neg - 0.5
        lw = -jnp.exp(w_log)
        a = jax.nn.sigmoid(a0_ref[:, cs] + _dot(da, wa_ref[:, cs]))
        gate = _dot(sg, wg_ref[:, cs])
        kk = kg * kk_ref[:, cs]
        kk = kk * lax.rsqrt(jnp.maximum(head_sum(kk * kk), 1e-24))
        k2 = kg * (1.0 + (a - 1.0) * ka_ref[:, cs])
        bonus = head_sum(rg * k2 * rk_ref[:, cs]) * vg
        r_s[:, cs] = rg
        lw_s[:, cs] = lw
        k_s[:, cs] = k2
        v_s[:, cs] = vg
        a_s[:, cs] = -kk
        b_s[:, cs] = kk * a
        g_s[:, cs] = gate
        bo_s[:, cs] = bonus

    ci = lax.broadcasted_iota(jnp.int32, (C, 2 * C), 0)
    cj = lax.broadcasted_iota(jnp.int32, (C, 2 * C), 1)
    left = cj < C
    strict = (cj % C) < ci
    incl = (cj % C) <= ci
    tri_incl = jnp.where(lax.broadcasted_iota(jnp.int32, (C, C), 1)
                         <= lax.broadcasted_iota(jnp.int32, (C, C), 0), 1.0, 0.0).astype(BF16)
    lane_c = lax.broadcasted_iota(jnp.int32, (C, LANES), 1)
    m0 = lane_c < 64
    eye = jnp.where(lane == sub, 1.0, 0.0).astype(F32)
    blockdiag = (lane // 64) == (sub // 64)

    csl = [slice(g * LANES, (g + 1) * LANES) for g in range(n_pairs)]
    P = range(n_pairs)
    cat0 = lambda *xs: jnp.concatenate(xs, axis=0)
    cat1 = lambda *xs: jnp.concatenate(xs, axis=1)
    bf = lambda x: x.astype(BF16)

    n_chunks = tt // C
    MM0, MM1, RTB, VB = range(4)
    BK, AK, ATB = range(3)
    ring64[...] = jnp.zeros_like(ring64)
    ring128[...] = jnp.zeros_like(ring128)
    ringg[...] = jnp.zeros_like(ringg)

    def stage1(c, out):
        rows = pl.ds(pl.multiple_of(c * C, C), C)
        slot = c % 4
        ld = lambda ref: [ref[rows, csl[g]] for g in P]
        rc, lwc, kc, vc, ac, bc = ld(r_s), ld(lw_s), ld(k_s), ld(v_s), ld(a_s), ld(b_s)
        cum = [_dot_exact_by_f32(tri_incl, lwc[g]) for g in P]
        yield
        last = [cum[g][C - 1:C, :] for g in P]
        rt = [rc[g] * jnp.exp(cum[g]) for g in P]
        at = [ac[g] * jnp.exp(cum[g] - lwc[g]) for g in P]
        ginv = [jnp.exp(-cum[g]) for g in P]
        btb = [bf(bc[g] * ginv[g]) for g in P]
        ktb = [bf(kc[g] * ginv[g]) for g in P]
        ghat = [jnp.exp(last[g] - cum[g]) for g in P]
        lhs0 = [bf(cat0(jnp.where(m0, at[g], 0.0), jnp.where(m0, rt[g], 0.0))) for g in P]
        lhs1 = [bf(cat0(jnp.where(m0, 0.0, at[g]), jnp.where(m0, 0.0, rt[g]))) for g in P]
        for g in P:
            ring128[slot, g, BK] = cat0(bf(bc[g] * ghat[g]), bf(kc[g] * ghat[g]))
            ring128[slot, g, ATB] = cat0(lhs0[g][:C], lhs1[g][:C])
            ring64[slot, g, VB] = bf(vc[g])
            ring64[slot, g, RTB] = bf(rt[g])
            ringg[slot, g, 0:1, :] = jnp.exp(last[g])
        x0 = [_dot_nt(lhs0[g], cat0(btb[g], ktb[g])) for g in P]
        x1 = [_dot_nt(lhs1[g], cat0(ktb[g], btb[g])) for g in P]
        yield
        n_bd = [cat0(jnp.where(left & strict, x0[g][:C], 0.0),
                     jnp.where((~left) & strict, x1[g][:C], 0.0)) for g in P]
        for g in P:
            ring128[slot, g, AK] = bf(cat0(jnp.where((~left) & strict, x0[g][:C], 0.0),
                                           jnp.where(left & strict, x1[g][:C], 0.0)))
            ring64[slot, g, MM0] = bf(jnp.where(incl, x0[g][C:], 0.0))
            ring64[slot, g, MM1] = bf(jnp.where(incl, x1[g][C:], 0.0))
        out["t"] = [eye + n_bd[g] for g in P]
        nb = [bf(n_bd[g]) for g in P]
        out["pw"] = [bf(_dot(nb[g], nb[g])) for g in P]

    def inverse_level(t, pw):
        res = [_dot(pw[g], cat1(pw[g], bf(t[g]))) for g in P]
        return [t[g] + res[g][:, LANES:] for g in P], [bf(res[g][:, :LANES]) for g in P]

    def stage2(inp, out):
        t, pw = inp["t"], inp["pw"]
        for level in range(3):
            t, pw = inverse_level(t, pw)
            if level < 2:
                yield
        out["t"], out["pw"] = t, pw

    def stage3(inp, c, out):
        slot = c % 4
        t, pw = inverse_level(inp["t"], inp["pw"])
        vb = [ring64[slot, g, VB] for g in P]
        av = [_dot(ring128[slot, g, AK], cat0(vb[g], vb[g])) for g in P]
        yield
        t = [t[g] + _dot(pw[g], bf(t[g])) for g in P]
        av = [cat0(jnp.where(m0, av[g][:C], 0.0), jnp.where(m0, 0.0, av[g][C:])) for g in P]
        yield
        wu = [_dot(bf(t[g]), cat1(ring128[slot, g, ATB], bf(av[g]))) for g in P]
        out["w"] = [bf(wu[g][:C, :LANES] + wu[g][C:, :LANES]) for g in P]
        out["u0"] = [wu[g][:C, LANES:] + wu[g][C:, LANES:] for g in P]

    def stage4(inp, c, valid):
        rows = pl.ds(pl.multiple_of(c * C, C), C)
        slot = c % 4
        vb = [ring64[slot, g, VB] for g in P]
        s = [s_ref[g] for g in P]
        sb = [bf(s[g]) for g in P]
        ub = [bf(_dot_nt(inp["w"][g], sb[g]) + inp["u0"][g]) for g in P]
        ys = [_dot_nt(ring64[slot, g, RTB], sb[g]) for g in P]
        yield
        uv = [cat0(ub[g], vb[g]) for g in P]
        y = [ys[g] + jnp.where(m0, _dot(ring64[slot, g, MM0], uv[g]),
                               _dot(ring64[slot, g, MM1], cat0(vb[g], ub[g]))) for g in P]
        upd = [_dot_tn(uv[g], ring128[slot, g, BK]) for g in P]
        yield
        for g in P:
            s_new = s[g] * ringg[slot, g, 0:1, :] + jnp.where(blockdiag, upd[g], 0.0)
            s_ref[g] = jnp.where(valid, s_new, s[g])
            y_s[rows, csl[g]] = y[g]

    def interleave(gens):
        live = list(gens)
        while live:
            still = []
            for gen in live:
                try:
                    next(gen)
                    still.append(gen)
                except StopIteration:
                    pass
            live = still

    def body(it, carry):
        o1, o2, o3 = carry
        n1, n2, n3 = {}, {}, {}
        interleave([stage4(o3, jnp.maximum(it - 3, 0), it >= 3),
                    stage3(o2, jnp.maximum(it - 2, 0), n3),
                    stage2(o1, n2),
                    stage1(jnp.minimum(it, n_chunks - 1), n1)])
        return n1, n2, n3

    zero_ref[...] = jnp.zeros_like(zero_ref)
    zb = lambda r: [zero_ref[0:r, :].astype(BF16) for _ in P]
    zf = lambda r: [zero_ref[0:r, :] for _ in P]
    front = lambda: dict(t=zf(2 * C), pw=zb(2 * C))
    lax.fori_loop(0, n_chunks + 3, body, (front(), front(), dict(w=zb(C), u0=zf(C))))

    for g in range(n_pairs):
        cs = slice(g * LANES, (g + 1) * LANES)
        y = y_s[:, cs]
        mean = head_sum(y) * (1.0 / 64.0)
        yc = y - mean
        var = head_sum(yc * yc) * (1.0 / 64.0)
        yn = yc * lax.rsqrt(var + GN_EPS) * lg_ref[:, cs] + lb_ref[:, cs]
        y_ref[:, cs] = ((yn + bo_s[:, cs]) * g_s[:, cs]).astype(BF16)


def _wkv(p_rwkv, prm, batch, seq, n_pairs, tt):
    m = p_rwkv.shape[0]
    c = prm["w0"].shape[1]
    gw = n_pairs * LANES
    n_col_blocks = c // gw
    nt = seq // tt
    low_w = LANES + prm["wg"].shape[0]

    def pspec(off):
        return pl.BlockSpec((tt, gw), lambda b, g, t: (b * nt + t, off * n_col_blocks + g))

    def vspec():
        return pl.BlockSpec((1, gw), lambda b, g, t: (0, g))

    def wspec(rows):
        return pl.BlockSpec((rows, gw), lambda b, g, t: (0, g))

    kern = functools.partial(_wkv_kernel, n_pairs=n_pairs, tt=tt)
    tile = pltpu.VMEM((tt, gw), F32)
    return pl.pallas_call(
        kern,
        grid=(batch, n_col_blocks, nt),
        in_specs=[
            pspec(0), pspec(1), pspec(2),
            pl.BlockSpec((tt, low_w), lambda b, g, t: (b * nt + t, (3 * c) // low_w)),
            vspec(), vspec(), vspec(), vspec(), vspec(), vspec(), vspec(),
            wspec(LANES), wspec(LANES), wspec(low_w - LANES),
        ],
        out_specs=pl.BlockSpec((tt, gw), lambda b, g, t: (b * nt + t, g)),
        out_shape=jax.ShapeDtypeStruct((m, c), BF16),
        scratch_shapes=[
            pltpu.VMEM((n_pairs, LANES, LANES), F32),
            tile, tile, tile, tile, tile, tile, tile, tile, tile,
            pltpu.VMEM((LANES, LANES), F32),
            pltpu.VMEM((4, n_pairs, 4, WKV_CHUNK, LANES), BF16),
            pltpu.VMEM((4, n_pairs, 3, 2 * WKV_CHUNK, LANES), BF16),
            pltpu.VMEM((4, n_pairs, 8, LANES), F32),
        ],
        compiler_params=_cparams(("parallel", "parallel", "arbitrary")),
        name="wkv7",
    )(p_rwkv, p_rwkv, p_rwkv, p_rwkv,
      prm["w0"], prm["a0"], prm["k_k"], prm["k_a"], prm["r_k"], prm["lnx_g"], prm["lnx_b"],
      prm["wd"], prm["wa"], prm["wg"])


def _sb_kernel(q_ref, k_ref, v_ref, o_ref, acc_ref, right_ref, *, tq, scale):
    i = pl.program_id(1)
    row = lax.broadcasted_iota(jnp.int32, (tq, tq), 0)
    col = lax.broadcasted_iota(jnp.int32, (tq, tq), 1)
    r2 = lax.broadcasted_iota(jnp.int32, (tq, 2 * tq), 0)
    c2 = lax.broadcasted_iota(jnp.int32, (tq, 2 * tq), 1)
    after_and_total = jnp.where((c2 >= tq) | (r2 > c2), 1.0, 0.0).astype(BF16)
    diag = col < row

    heads = range(SB_HEADS)
    hsl = [slice(h * LANES, (h + 1) * LANES) for h in heads]

    def blocks(js, first):
        nb = len(js)
        it = [(b, h) for b in range(nb) for h in heads]
        on_diag = lambda p: first and p[0] == 0
        ks = [pl.ds(pl.multiple_of(j * tq, tq), tq) for j in js]
        z = {p: _dot_nt(q_ref[:, hsl[p[1]]], k_ref[ks[p[0]], hsl[p[1]]]) * scale for p in it}
        sp = {p: jnp.maximum(z[p], 0.0) + jnp.log(1.0 + jnp.exp(-jnp.abs(z[p]))) for p in it}
        log_keep = {p: jnp.where(diag, -sp[p], 0.0) if on_diag(p) else -sp[p] for p in it}
        hi = {p: log_keep[p].astype(BF16) for p in it}
        lo = {p: (log_keep[p] - hi[p].astype(F32)).astype(BF16) for p in it}
        sums = {p: _dot(hi[p], after_and_total) + _dot(lo[p], after_and_total) for p in it}
        right = {}
        for h in heads:
            run = None if first else right_ref[h]
            for b in range(nb):
                right[(b, h)] = run
                tot = sums[(b, h)][:, tq:]
                run = tot if run is None else run + tot
            right[("end", h)] = run
        after = {p: sums[p][:, :tq] if right[p] is None else sums[p][:, :tq] + right[p] for p in it}
        attn = {p: jnp.exp(z[p] - sp[p] + after[p]) for p in it}
        attn = {p: jnp.where(diag, attn[p], 0.0) if on_diag(p) else attn[p] for p in it}
        pv = {p: _dot(attn[p].astype(BF16), v_ref[ks[p[0]], hsl[p[1]]]) for p in it}
        for h in heads:
            tot = pv[(0, h)]
            for b in range(1, nb):
                tot = tot + pv[(b, h)]
            if first:
                acc_ref[h] = tot
            else:
                acc_ref[h] += tot
            right_ref[h] = right[("end", h)]

    per = SB_BLOCKS_PER_ITER
    first_size = i % per + 1
    for size in range(1, per + 1):
        @pl.when(first_size == size)
        def _(size=size):
            blocks([i - b for b in range(size)], True)

    def body(jj, _):
        j = i - first_size - per * jj
        blocks([j - b for b in range(per)], False)
        return 0

    lax.fori_loop(0, (i + 1 - first_size) // per, body, 0)

    for h in heads:
        o_ref[:, hsl[h]] = acc_ref[h].astype(BF16)


def _sb_attn(p_attn, batch, seq, tq):
    m = p_attn.shape[0]
    w = SB_HEADS * LANES
    nq = seq // tq
    kern = functools.partial(_sb_kernel, tq=tq, scale=LANES ** -0.5)
    return pl.pallas_call(
        kern,
        grid=(batch, nq),
        in_specs=[
            pl.BlockSpec((tq, w), lambda b, i: (b * nq + i, 0)),
            pl.BlockSpec((seq, w), lambda b, i: (b, 1)),
            pl.BlockSpec((seq, w), lambda b, i: (b, 2)),
        ],
        out_specs=pl.BlockSpec((tq, w), lambda b, i: (b * nq + i, 0)),
        out_shape=jax.ShapeDtypeStruct((m, w), BF16),
        scratch_shapes=[pltpu.VMEM((SB_HEADS, tq, tq), F32), pltpu.VMEM((SB_HEADS, tq, tq), F32)],
        compiler_params=_cparams(("parallel", "arbitrary")),
        name="sb_attn",
    )(p_attn, p_attn, p_attn)


def _mem_kv_kernel(m_ref, w_ref, o_ref):
    o_ref[...] = _dot(m_ref[...].astype(BF16), w_ref[...]).astype(BF16)


def _mem_kv(mem2, w_bf16, tm):
    m, d = mem2.shape
    n = w_bf16.shape[1]
    return pl.pallas_call(
        _mem_kv_kernel,
        grid=(m // tm,),
        in_specs=[pl.BlockSpec((tm, d), lambda i: (i, 0)),
                  pl.BlockSpec((d, n), lambda i: (0, 0))],
        out_specs=pl.BlockSpec((tm, n), lambda i: (i, 0)),
        out_shape=jax.ShapeDtypeStruct((m, n), BF16),
        compiler_params=_cparams(("parallel",)),
        name="mem_kv",
    )(mem2, w_bf16)


def _mem_attn_kernel(q_ref, k_ref, v_ref, o_ref, *, scale):
    for h in range(MEM_HEADS):
        hs = slice(h * LANES, (h + 1) * LANES)
        s = _dot_nt(q_ref[:, hs], k_ref[:, hs]) * scale
        s = s - jnp.max(s, axis=-1, keepdims=True)
        e = jnp.exp(s)
        p = e / jnp.sum(e, axis=-1, keepdims=True)
        o_ref[:, hs] = _dot(p.astype(BF16), v_ref[:, hs]).astype(BF16)


def _mem_attn(p_attn, kv, batch, seq, mem_len, tq):
    m = p_attn.shape[0]
    w = MEM_HEADS * LANES
    nq = seq // tq
    kern = functools.partial(_mem_attn_kernel, scale=LANES ** -0.5)
    return pl.pallas_call(
        kern,
        grid=(batch, nq),
        in_specs=[
            pl.BlockSpec((tq, w), lambda b, i: (b * nq + i, 3)),
            pl.BlockSpec((mem_len, w), lambda b, i: (b, 0)),
            pl.BlockSpec((mem_len, w), lambda b, i: (b, 1)),
        ],
        out_specs=pl.BlockSpec((tq, w), lambda b, i: (b * nq + i, 0)),
        out_shape=jax.ShapeDtypeStruct((m, w), BF16),
        compiler_params=_cparams(("parallel", "parallel")),
        name="mem_attn",
    )(p_attn, kv, kv)


def _outproj_kernel(yr_ref, ys_ref, ym_ref, x_ref, g0_ref, b0_ref, w_ref, g_ref, b_ref, rw_ref, rb_ref,
                    h1_ref, lg_ref):
    c0 = yr_ref.shape[1]
    c1 = c0 + ys_ref.shape[1]
    mix = (_dot(yr_ref[...], w_ref[0:c0, :]) + _dot(ys_ref[...], w_ref[c0:c1, :])
           + _dot(ym_ref[...], w_ref[c1:, :]))
    h = _layer_norm(x_ref[...], g0_ref[...], b0_ref[...])
    h1 = _layer_norm(DEEPNORM_ALPHA * h + mix, g_ref[...], b_ref[...])
    h1_ref[...] = h1
    hi = h1.astype(BF16)
    lo = (h1 - hi.astype(F32)).astype(BF16)
    both = _dot(hi, rw_ref[...])
    lg_ref[...] = both[:, :LANES] + both[:, LANES:] + _dot(lo, rw_ref[:, :LANES]) + rb_ref[...]


def _outproj(y_r, y_s, y_m, x2, g0, b0, w, g, b, r_w, r_b, tm):
    m, d = x2.shape
    full = lambda a: pl.BlockSpec(a.shape, lambda i: (0, 0))
    rows = lambda a: pl.BlockSpec((tm, a.shape[1]), lambda i: (i, 0))
    return pl.pallas_call(
        _outproj_kernel,
        grid=(m // tm,),
        in_specs=[rows(y_r), rows(y_s), rows(y_m), rows(x2), full(g0), full(b0), full(w), full(g), full(b),
                  full(r_w), full(r_b)],
        out_specs=[pl.BlockSpec((tm, d), lambda i: (i, 0)),
                   pl.BlockSpec((tm, LANES), lambda i: (i, 0))],
        out_shape=[jax.ShapeDtypeStruct((m, d), F32),
                   jax.ShapeDtypeStruct((m, LANES), F32)],
        compiler_params=_cparams(("parallel",)),
        name="outproj",
    )(y_r, y_s, y_m, x2, g0, b0, w, g, b, r_w, r_b)


def _route_kernel(lg_ref, id_ref, wt_ref, *, n_groups, per_group):
    lg = lg_ref[...]
    lane_i = lax.broadcasted_iota(jnp.int32, lg.shape, 1)
    lane = lane_i.astype(F32)
    neg = jnp.float32(-jnp.inf)
    big = jnp.float32(2 ** 20)

    def first_max(vals):
        mx = jnp.max(vals, axis=-1, keepdims=True)
        idx = jnp.min(jnp.where(vals == mx, lane, big), axis=-1, keepdims=True)
        return mx, idx

    is_group = lane < n_groups
    gmax, gidx = first_max(jnp.where(is_group, lg, neg))
    gsum = jnp.sum(jnp.where(is_group, jnp.exp(lg - gmax), 0.0), axis=-1, keepdims=True)
    group_w = 1.0 / gsum
    lo = n_groups + gidx * per_group
    in_group = (lane >= lo) & (lane < lo + per_group)
    v1, i1 = first_max(jnp.where(in_group, lg, neg))
    v2, i2 = first_max(jnp.where(in_group & (lane != i1), lg, neg))
    e2 = jnp.exp(v2 - v1)
    w1 = group_w / (1.0 + e2)
    w2 = group_w * e2 / (1.0 + e2)
    ids = jnp.where(lane_i == 0, i1 - n_groups, jnp.where(lane_i == 1, i2 - n_groups, 0.0))
    id_ref[...] = ids.T[0:8, :].astype(jnp.int32)
    wt_ref[...] = jnp.where(lane_i == 0, w1, jnp.where(lane_i == 1, w2, 0.0))


def _route(logits, n_groups, per_group, tm):
    m = logits.shape[0]
    kern = functools.partial(_route_kernel, n_groups=n_groups, per_group=per_group)
    spec = pl.BlockSpec((tm, LANES), lambda i: (i, 0))
    return pl.pallas_call(
        kern,
        grid=(m // tm,),
        in_specs=[spec],
        out_specs=[pl.BlockSpec((8, tm), lambda i: (0, i)), spec],
        out_shape=[jax.ShapeDtypeStruct((8, m), jnp.int32),
                   jax.ShapeDtypeStruct((m, LANES), F32)],
        compiler_params=_cparams(("parallel",)),
        name="route",
    )(logits)


def _plan_kernel(id_ref, pos_ref, meta_ref, cnt_ref, base_ref, start_ref, *, tb, rows, n_experts, nb):
    phase = pl.program_id(0)
    j = pl.program_id(1)
    sub = lax.broadcasted_iota(jnp.int32, (LANES, tb), 0)
    e1 = id_ref[0:1, :]
    e2 = id_ref[1:2, :]
    hit1 = jnp.where(sub == e1, 1.0, 0.0)
    hit2 = jnp.where(sub == e2, 1.0, 0.0)
    hits = hit1 + hit2
    per_expert = jnp.sum(hits, axis=1, keepdims=True)

    @pl.when((phase == 0) & (j == 0))
    def _():
        cnt_ref[...] = jnp.zeros_like(cnt_ref)

    @pl.when(phase == 0)
    def _():
        cnt_ref[...] += per_expert

    sq_r = lax.broadcasted_iota(jnp.int32, (LANES, LANES), 0)
    sq_c = lax.broadcasted_iota(jnp.int32, (LANES, LANES), 1)

    @pl.when((phase == 1) & (j == 0))
    def _():
        n_blk = jnp.floor((cnt_ref[...] + (rows - 1)) * (1.0 / rows))
        before = jnp.where(sq_c < sq_r, 1.0, 0.0).astype(BF16)
        blk_start = _dot(before, jnp.broadcast_to(n_blk, (LANES, LANES)).astype(BF16))
        start_ref[...] = blk_start[:, 0:1]
        base_ref[...] = jnp.zeros_like(base_ref)
        blk_end = blk_start + n_blk
        n_used = jnp.sum(jnp.where(sq_r[:, 0:1] < n_experts, n_blk, 0.0), axis=0, keepdims=True)
        owner = jnp.sum(jnp.where((sq_r < n_experts) & (blk_end <= sq_c.astype(F32)), 1.0, 0.0),
                        axis=0, keepdims=True)
        last_owner = jnp.max(jnp.where(n_blk > 0.0, sq_r[:, 0:1].astype(F32), 0.0), axis=0, keepdims=True)
        blk = sq_c[0:1, :].astype(F32)
        in_use = blk < n_used
        owner = jnp.where(in_use, owner, last_owner)
        mine = sq_r.astype(F32) == owner
        cnt_o = jnp.sum(jnp.where(mine, cnt_ref[...], 0.0), axis=0, keepdims=True)
        start_o = jnp.sum(jnp.where(mine, blk_start, 0.0), axis=0, keepdims=True)
        valid = jnp.clip(cnt_o - (blk - start_o) * rows, 0.0, float(rows))
        valid = jnp.where(in_use, valid, 0.0)
        row8 = lax.broadcasted_iota(jnp.int32, (8, LANES), 0)
        meta = jnp.where(row8 == 0, owner, jnp.where(row8 == 1, valid, jnp.where(row8 == 2, n_used, 0.0)))
        meta_ref[...] = meta.astype(jnp.int32)

    @pl.when(phase == 1)
    def _():
        tr = lax.broadcasted_iota(jnp.int32, (tb, tb), 0)
        tc = lax.broadcasted_iota(jnp.int32, (tb, tb), 1)
        earlier = jnp.where(tr < tc, 1.0, 0.0).astype(BF16)
        seen = _dot(hits.astype(BF16), earlier)
        slot = start_ref[...] * rows + base_ref[...] + seen
        p1 = jnp.sum(hit1 * slot, axis=0, keepdims=True)
        p2 = jnp.sum(hit2 * slot, axis=0, keepdims=True)
        row8 = lax.broadcasted_iota(jnp.int32, (8, tb), 0)
        pos_ref[...] = jnp.where(row8 == 0, p1, jnp.where(row8 == 1, p2, 0.0)).astype(jnp.int32)
        base_ref[...] += per_expert


def _plan(ids_t, n_experts, rows, tb):
    m = ids_t.shape[1]
    nb = m // tb
    kern = functools.partial(_plan_kernel, tb=tb, rows=rows, n_experts=n_experts, nb=nb)
    col = pltpu.VMEM((LANES, 1), F32)
    return pl.pallas_call(
        kern,
        grid=(2, nb),
        in_specs=[pl.BlockSpec((8, tb), lambda p, j: (0, j))],
        out_specs=[pl.BlockSpec((8, tb), lambda p, j: (0, j * p)),
                   pl.BlockSpec((8, LANES), lambda p, j: (0, 0))],
        out_shape=[jax.ShapeDtypeStruct((8, m), jnp.int32),
                   jax.ShapeDtypeStruct((8, LANES), jnp.int32)],
        scratch_shapes=[col, col, col],
        compiler_params=_cparams(("arbitrary", "arbitrary")),
        name="moe_plan",
    )(ids_t)


def _for_range(lo, hi, body):
    full = (hi - lo) // ROW_GROUP
    first = lo // ROW_GROUP

    def group(g, _):
        for u in range(ROW_GROUP):
            body(first + g, u)
        return 0

    lax.fori_loop(0, full, group, 0)

    def one(r, _):
        body(r // ROW_GROUP, r % ROW_GROUP)
        return 0

    lax.fori_loop(lo + full * ROW_GROUP, hi, one, 0)


def _row_in(src_hbm, dst_vmem, sem, src_row, tile, sub):
    return pltpu.make_async_copy(src_hbm.at[pl.ds(src_row, 1)], dst_vmem.at[tile, pl.ds(sub, 1)], sem)


def _row_out(src_vmem, dst_hbm, sem, tile, sub, dst_row):
    return pltpu.make_async_copy(src_vmem.at[tile, pl.ds(sub, 1)], dst_hbm.at[pl.ds(dst_row, 1)], sem)


def _gather_cparams(sem):
    return pltpu.CompilerParams(dimension_semantics=sem, vmem_limit_bytes=VMEM_LIMIT,
                                disable_bounds_checks=True)


def _ffn_kernel(asg_ref, be_ref, nv_ref, nu_ref, h_hbm, wg_ref, wu_ref, wd_ref, y_hbm,
                xf_ref, xb_ref, acc_ref, sem_in, sem_out, *, rows, sub, nj, n_tok, n_blocks):
    i = pl.program_id(0)
    j = pl.program_id(1)
    n_used = nu_ref[0]
    used = i < n_used
    slot = i % 2

    def gather(block, lo, hi, buf):
        def body(tile, sub_row):
            a = asg_ref[block * rows + tile * ROW_GROUP + sub_row]
            tok = jnp.where(a >= n_tok, a - n_tok, a)
            _row_in(h_hbm, xf_ref.at[buf], sem_in.at[buf], tok, tile, sub_row).start()

        _for_range(lo, hi, body)

    def gather_wait(block, buf):
        _for_range(0, nv_ref[block],
                   lambda tile, sub_row: _row_in(h_hbm, xf_ref.at[buf], sem_in.at[buf], 0, tile, sub_row).wait())

    def emit(block, buf):
        def body(tile, sub_row):
            a = asg_ref[block * rows + tile * ROW_GROUP + sub_row]
            _row_out(acc_ref.at[buf], y_hbm, sem_out.at[buf], tile, sub_row, a).start()

        _for_range(0, nv_ref[block], body)

    def emit_wait(block, buf):
        _for_range(0, nv_ref[block],
                   lambda tile, sub_row: _row_out(acc_ref.at[buf], y_hbm, sem_out.at[buf], tile, sub_row, 0).wait())

    def as_rows(x):
        return x.reshape(x.shape[0] * ROW_GROUP, x.shape[2])

    def as_tiles(x):
        return x.reshape(x.shape[0] // ROW_GROUP, ROW_GROUP, x.shape[1])

    @pl.when((i == 0) & (j == 0))
    def _():
        xf_ref[...] = jnp.zeros_like(xf_ref)
        gather(0, 0, nv_ref[0], 0)

    @pl.when(used & (j == 0))
    def _():
        gather_wait(i, slot)

    @pl.when(i + 1 < n_used)
    def _():
        nxt = nv_ref[i + 1]
        q = rows // nj
        gather(i + 1, jnp.minimum(j * q, nxt), jnp.minimum((j + 1) * q, nxt), 1 - slot)

    n_sub = (nv_ref[i] + sub - 1) // sub
    for n in range(1, rows // sub + 1):
        @pl.when(used & (n_sub == n))
        def _(n=n):
            r = n * sub
            nt = r // ROW_GROUP

            @pl.when(j == 0)
            def _():
                xb_ref[0:r, :] = as_rows(xf_ref[slot, 0:nt]).astype(BF16)
                acc_ref[slot, 0:nt] = jnp.zeros((nt,) + acc_ref.shape[2:], F32)

            xb = xb_ref[0:r, :]
            gate = _dot(xb, wg_ref[...].astype(BF16))
            up = _dot(xb, wu_ref[...].astype(BF16))
            hid = (gate * jax.nn.sigmoid(gate)) * up
            acc_ref[slot, 0:nt] += as_tiles(_dot(hid.astype(BF16), wd_ref[...].astype(BF16)))

    @pl.when((j == nj - 1) & (i >= 1) & (i <= n_used))
    def _():
        emit_wait(i - 1, 1 - slot)

    @pl.when((j == nj - 1) & used)
    def _():
        emit(i, slot)

    @pl.when((j == nj - 1) & used & (i == n_blocks - 1))
    def _():
        emit_wait(i, slot)


def _moe_ffn(slot_asg, block_expert, n_valid, n_used, h1, w_gate, w_up, w_down, n_blocks, rows, tf):
    n_tok, d = h1.shape
    de = w_gate.shape[2]
    nj = de // tf
    kern = functools.partial(_ffn_kernel, rows=rows, sub=MOE_SUB, nj=nj, n_tok=n_tok, n_blocks=n_blocks)

    def jidx(i, j, nu):
        return jnp.where(i < nu[0], j, nj - 1)

    return pl.pallas_call(
        kern,
        grid_spec=pltpu.PrefetchScalarGridSpec(
            num_scalar_prefetch=4,
            grid=(n_blocks, nj),
            in_specs=[
                pl.BlockSpec(memory_space=pl.ANY),
                pl.BlockSpec((None, d, tf), lambda i, j, asg, be, nv, nu: (be[i], 0, jidx(i, j, nu))),
                pl.BlockSpec((None, d, tf), lambda i, j, asg, be, nv, nu: (be[i], 0, jidx(i, j, nu))),
                pl.BlockSpec((None, tf, d), lambda i, j, asg, be, nv, nu: (be[i], jidx(i, j, nu), 0)),
            ],
            out_specs=pl.BlockSpec(memory_space=pl.ANY),
            scratch_shapes=[pltpu.VMEM((2, rows // ROW_GROUP, ROW_GROUP, d), F32),
                            pltpu.VMEM((rows, d), BF16),
                            pltpu.VMEM((2, rows // ROW_GROUP, ROW_GROUP, d), F32),
                            pltpu.SemaphoreType.DMA((2,)),
                            pltpu.SemaphoreType.DMA((2,))],
        ),
        out_shape=jax.ShapeDtypeStruct((2 * n_tok, d), F32),
        compiler_params=_gather_cparams(("arbitrary", "arbitrary")),
        name="moe_ffn",
    )(slot_asg, block_expert, n_valid, n_used, h1, w_gate, w_up, w_down)


def _combine_kernel(h_ref, y0_ref, y1_ref, wt_ref, g_ref, b_ref, o_ref):
    wt = wt_ref[...]
    ffn = y0_ref[...] * wt[:, 0:1] + y1_ref[...] * wt[:, 1:2]
    o_ref[...] = _layer_norm(DEEPNORM_ALPHA * h_ref[...] + ffn, g_ref[...], b_ref[...])


def _combine(h1, ys, wts, g, b, tm):
    m, d = h1.shape
    nt = m // tm
    rows = pl.BlockSpec((tm, d), lambda i: (i, 0))
    vec = pl.BlockSpec((1, d), lambda i: (0, 0))
    return pl.pallas_call(
        _combine_kernel,
        grid=(nt,),
        in_specs=[rows, rows, pl.BlockSpec((tm, d), lambda i: (nt + i, 0)),
                  pl.BlockSpec((tm, LANES), lambda i: (i, 0)), vec, vec],
        out_specs=rows,
        out_shape=jax.ShapeDtypeStruct((m, d), F32),
        compiler_params=_cparams(("parallel",)),
        name="moe_combine",
    )(h1, ys, ys, wts, g, b)


def _dispatch_plan(ids_t, n_experts, rows):
    m = ids_t.shape[1]
    n_blocks = -(-(2 * m) // rows) + n_experts
    assert n_blocks <= LANES
    pos_t, meta = _plan(ids_t, n_experts, rows, _pick(m, 512))
    pos = pos_t[:2].reshape(-1)
    slot_asg = jnp.zeros((n_blocks * rows,), jnp.int32).at[pos].set(jnp.arange(2 * m, dtype=jnp.int32))
    return slot_asg, meta[2, :1], meta[0, :n_blocks], meta[1, :n_blocks], n_blocks


def _pick(n, pref):
    t = min(pref, n)
    while n % t:
        t //= 2
    return t


def kernel(x, mem, ln_in_g, ln_in_b, w_in, tshift_mu, w0, w_decay_up, a0, w_a_up, w_g_up, k_k, k_a, r_k,
           lnx_g, lnx_b, w_mem_kv, w_out, ln1_g, ln1_b, router_group, router_group_b, router_expert,
           router_expert_b, w_e_gate, w_e_up, w_e_down, ln2_g, ln2_b):
    batch, seq, d = x.shape
    mem_len = mem.shape[1]
    m = batch * seq
    c = w0.shape[1]
    dr, ar, gr = w_decay_up.shape[1], w_a_up.shape[1], w_g_up.shape[1]
    rwkv_cols = 3 * c + dr + ar + gr
    sb_w = SB_HEADS * LANES
    mem_w = MEM_HEADS * LANES
    assert dr + ar == LANES and c % (2 * LANES) == 0 and w_in.shape[0] == DEPTH
    assert w_in.shape[2] == rwkv_cols + 3 * sb_w + mem_w
    n_experts = router_expert.shape[2]
    row = lambda a: a.reshape(1, -1)

    tn = 512
    low_w = -(-(dr + ar + gr) // LANES) * LANES
    rw_pad = -(-(3 * c + low_w) // tn) * tn
    wi = w_in[0]
    w_rwkv = jnp.pad(wi[:, :rwkv_cols].astype(BF16), ((0, 0), (0, rw_pad - rwkv_cols)))
    w_attn = wi[:, rwkv_cols:].astype(BF16)
    mu = row(jnp.pad(tshift_mu[0], (0, rw_pad - rwkv_cols)))
    lp = low_w - (dr + ar + gr)
    prm = dict(
        w0=row(w0[0]), a0=row(a0[0]), k_k=row(k_k[0]), k_a=row(k_a[0]), r_k=row(r_k[0]),
        lnx_g=row(lnx_g[0]), lnx_b=row(lnx_b[0]),
        wd=jnp.concatenate([w_decay_up[0], jnp.zeros((ar, c), F32)], axis=0).astype(BF16),
        wa=jnp.concatenate([jnp.zeros((dr, c), F32), w_a_up[0]], axis=0).astype(BF16),
        wg=jnp.concatenate([w_g_up[0], jnp.zeros((lp, c), F32)], axis=0).astype(BF16),
    )
    assert (3 * c) % low_w == 0

    x2 = x.reshape(m, d)
    hb = _ln_in(x2, row(ln_in_g), row(ln_in_b), _pick(m, 512))
    p_rwkv, p_attn = _inproj(hb, w_rwkv, w_attn, mu, seq, _pick(seq, 2048), tn)

    y_rwkv = _wkv(p_rwkv, prm, batch, seq, n_pairs=4, tt=_pick(seq, 1024))
    y_sb = _sb_attn(p_attn, batch, seq, tq=LANES)
    kv = _mem_kv(mem.reshape(batch * mem_len, d), w_mem_kv[0].astype(BF16), _pick(batch * mem_len, 256))
    y_mem = _mem_attn(p_attn, kv, batch, seq, mem_len, tq=_pick(seq, 512))

    wo = w_out[0].astype(BF16)
    r_w = jnp.concatenate([router_group[0], router_expert[0],
                           jnp.zeros((d, LANES - N_GROUPS - n_experts), F32)], axis=1)
    r_b = jnp.concatenate([router_group_b[0], router_expert_b[0],
                           jnp.zeros((LANES - N_GROUPS - n_experts,), F32)]).reshape(1, LANES)
    r_hi = r_w.astype(BF16)
    r_w2 = jnp.concatenate([r_hi, (r_w - r_hi.astype(F32)).astype(BF16)], axis=1)
    h1, logits = _outproj(y_rwkv, y_sb, y_mem, x2, row(ln_in_g), row(ln_in_b), wo,
                          row(ln1_g[0]), row(ln1_b[0]), r_w2, r_b, _pick(m, 512))

    ids_t, wts = _route(logits, N_GROUPS, n_experts // N_GROUPS, _pick(m, 512))
    slot_asg, n_used, block_expert, n_valid, n_blocks = _dispatch_plan(ids_t, n_experts, MOE_ROWS)
    ys = _moe_ffn(slot_asg, block_expert, n_valid, n_used, h1, w_e_gate[0], w_e_up[0], w_e_down[0],
                  n_blocks, MOE_ROWS, tf=512)
    out = _combine(h1, ys, wts, row(ln2_g[0]), row(ln2_b[0]), _pick(m, 256))
    return out.reshape(batch, seq, d)
```

```python
import functools

import jax
import jax.numpy as jnp
from jax import lax
from jax.experimental import pallas as pl
from jax.experimental.pallas import tpu as pltpu

F32 = jnp.float32
BF16 = jnp.bfloat16

SB_HEADS = 4
MEM_HEADS = 4
N_GROUPS = 8
DEPTH = 1
DEEPNORM_ALPHA = (2.0 * DEPTH) ** 0.25
LN_EPS = 1e-5
GN_EPS = 64e-5

LANES = 128
WKV_CHUNK = 64
MOE_ROWS = 512
MOE_SUB = 64
ROW_GROUP = 8
SB_BLOCKS_PER_ITER = 4
VMEM_LIMIT = 56 * 1024 * 1024


def _cparams(sem):
    return pltpu.CompilerParams(dimension_semantics=sem, vmem_limit_bytes=VMEM_LIMIT)


def _layer_norm(x, g, b):
    mu = jnp.mean(x, axis=-1, keepdims=True)
    xc = x - mu
    var = jnp.mean(xc * xc, axis=-1, keepdims=True)
    return xc * lax.rsqrt(var + LN_EPS) * g + b


def _split2(x):
    hi = x.astype(BF16)
    return hi, (x - hi.astype(F32)).astype(BF16)


def _dot(a, b):
    return jnp.dot(a, b, preferred_element_type=F32)


def _dot_nt(a, b):
    return lax.dot_general(a, b, (((1,), (1,)), ((), ())), preferred_element_type=F32)


def _dot_tn(a, b):
    return lax.dot_general(a, b, (((0,), (0,)), ((), ())), preferred_element_type=F32)


def _dot_exact_by_f32(m, x):
    hi, lo = _split2(x)
    return _dot(m, hi) + _dot(m, lo)


def _ln_kernel(x_ref, g_ref, b_ref, hb_ref):
    hb_ref[...] = _layer_norm(x_ref[...], g_ref[...], b_ref[...]).astype(BF16)


def _ln_in(x2, g, b, tm):
    m, d = x2.shape
    rows = pl.BlockSpec((tm, d), lambda i: (i, 0))
    vec = pl.BlockSpec((1, d), lambda i: (0, 0))
    return pl.pallas_call(
        _ln_kernel,
        grid=(m // tm,),
        in_specs=[rows, vec, vec],
        out_specs=rows,
        out_shape=jax.ShapeDtypeStruct((m, d), BF16),
        compiler_params=_cparams(("parallel",)),
        name="ln_in",
    )(x2, g, b)


def _inproj_kernel(hb_ref, wr_ref, wa_ref, mu_ref, pr_ref, pa_ref, last_ref, *, n_f32_tiles, tiles_per_seq):
    i = pl.program_id(0)
    n = pl.program_id(1)

    @pl.when((i == 0) & (n == 0))
    def _():
        last_ref[...] = jnp.zeros_like(last_ref)

    @pl.when(n < n_f32_tiles)
    def _():
        tm = pr_ref.shape[0]
        p = _dot(hb_ref[...], wr_ref[...])
        slot = jnp.minimum(n, n_f32_tiles - 1)
        carried = jnp.where(i % tiles_per_seq == 0, 0.0, last_ref[slot])
        prev = pltpu.roll(p, shift=1, axis=0)
        prev = jnp.where(lax.broadcasted_iota(jnp.int32, (tm, 1), 0) == 0, carried, prev)
        last_ref[slot] = p[tm - 1:tm, :]
        pr_ref[...] = p + (prev - p) * mu_ref[...]

    @pl.when(n >= n_f32_tiles)
    def _():
        pa_ref[...] = _dot(hb_ref[...], wa_ref[...]).astype(BF16)


def _inproj(hb, w_rwkv, w_attn, mu, seq, tm, tn):
    m, d = hb.shape
    n_rwkv_cols, n_attn_cols = w_rwkv.shape[1], w_attn.shape[1]
    nf = n_rwkv_cols // tn
    kern = functools.partial(_inproj_kernel, n_f32_tiles=nf, tiles_per_seq=seq // tm)
    first = lambda i, n: (0, jnp.minimum(n, nf - 1))
    second = lambda i, n: (0, jnp.maximum(n - nf, 0))
    return pl.pallas_call(
        kern,
        grid=(m // tm, (n_rwkv_cols + n_attn_cols) // tn),
        in_specs=[
            pl.BlockSpec((tm, d), lambda i, n: (i, 0)),
            pl.BlockSpec((d, tn), first),
            pl.BlockSpec((d, tn), second),
            pl.BlockSpec((1, tn), first),
        ],
        out_specs=[
            pl.BlockSpec((tm, tn), lambda i, n: (i, jnp.minimum(n, nf - 1))),
            pl.BlockSpec((tm, tn), lambda i, n: (i, jnp.maximum(n - nf, 0))),
        ],
        out_shape=[
            jax.ShapeDtypeStruct((m, n_rwkv_cols), F32),
            jax.ShapeDtypeStruct((m, n_attn_cols), BF16),
        ],
        scratch_shapes=[pltpu.VMEM((nf, 1, tn), F32)],
        compiler_params=_cparams(("arbitrary", "arbitrary")),
        name="inproj",
    )(hb, w_rwkv, w_attn, mu)


def _wkv_kernel(pr_ref, pk_ref, pv_ref, pl_ref,
                w0_ref, a0_ref, kk_ref, ka_ref, rk_ref, lg_ref, lb_ref,
                wd_ref, wa_ref, wg_ref,
                y_ref,
                s_ref,
                r_s, lw_s, k_s, v_s, a_s, b_s, y_s, g_s, bo_s, zero_ref, ring64, ring128, ringg,
                *, n_pairs, tt):
    C = WKV_CHUNK
    t_idx = pl.program_id(2)

    @pl.when(t_idx == 0)
    def _():
        s_ref[...] = jnp.zeros_like(s_ref)

    lane = lax.broadcasted_iota(jnp.int32, (LANES, LANES), 1)
    sub = lax.broadcasted_iota(jnp.int32, (LANES, LANES), 0)
    head_ones = jnp.where((lane // 64) == (sub // 64), 1.0, 0.0).astype(BF16)

    def head_sum(x):
        hi = x.astype(BF16)
        lo = (x - hi.astype(F32)).astype(BF16)
        return _dot(hi, head_ones) + _dot(lo, head_ones)

    da = pl_ref[:, 0:LANES]
    th = jnp.tanh(da).astype(BF16)
    sg = jax.nn.sigmoid(pl_ref[:, LANES:]).astype(BF16)
    da = da.astype(BF16)
    for g in range(n_pairs):
        cs = slice(g * LANES, (g + 1) * LANES)
        rg, kg, vg = pr_ref[:, cs], pk_ref[:, cs], pv_ref[:, cs]
        pre = w0_ref[:, cs] + _dot(th, wd_ref[:, cs])
        softplus_neg = jnp.maximum(-pre, 0.0) + jnp.log(1.0 + jnp.exp(-jnp.abs(pre)))
        w_log = -softplus_neg - 0.5
        lw = -jnp.exp(w_log)
        a = jax.nn.sigmoid(a0_ref[:, cs] + _dot(da, wa_ref[:, cs]))
        gate = _dot(sg, wg_ref[:, cs])
        kk = kg * kk_ref[:, cs]
        kk = kk * lax.rsqrt(jnp.maximum(head_sum(kk * kk), 1e-24))
        k2 = kg * (1.0 + (a - 1.0) * ka_ref[:, cs])
        bonus = head_sum(rg * k2 * rk_ref[:, cs]) * vg
        r_s[:, cs] = rg
        lw_s[:, cs] = lw
        k_s[:, cs] = k2
        v_s[:, cs] = vg
        a_s[:, cs] = -kk
        b_s[:, cs] = kk * a
        g_s[:, cs] = gate
        bo_s[:, cs] = bonus

    ci = lax.broadcasted_iota(jnp.int32, (C, 2 * C), 0)
    cj = lax.broadcasted_iota(jnp.int32, (C, 2 * C), 1)
    left = cj < C
    strict = (cj % C) < ci
    incl = (cj % C) <= ci
    tri_incl = jnp.where(lax.broadcasted_iota(jnp.int32, (C, C), 1)
                         <= lax.broadcasted_iota(jnp.int32, (C, C), 0), 1.0, 0.0).astype(BF16)
    lane_c = lax.broadcasted_iota(jnp.int32, (C, LANES), 1)
    m0 = lane_c < 64
    eye = jnp.where(lane == sub, 1.0, 0.0).astype(F32)
    blockdiag = (lane // 64) == (sub // 64)

    csl = [slice(g * LANES, (g + 1) * LANES) for g in range(n_pairs)]
    P = range(n_pairs)
    cat0 = lambda *xs: jnp.concatenate(xs, axis=0)
    cat1 = lambda *xs: jnp.concatenate(xs, axis=1)
    bf = lambda x: x.astype(BF16)

    n_chunks = tt // C
    MM0, MM1, RTB, VB = range(4)
    BK, AK, ATB = range(3)
    ring64[...] = jnp.zeros_like(ring64)
    ring128[...] = jnp.zeros_like(ring128)
    ringg[...] = jnp.zeros_like(ringg)

    def stage1(c, out):
        rows = pl.ds(pl.multiple_of(c * C, C), C)
        slot = c % 4
        ld = lambda ref: [ref[rows, csl[g]] for g in P]
        rc, lwc, kc, vc, ac, bc = ld(r_s), ld(lw_s), ld(k_s), ld(v_s), ld(a_s), ld(b_s)
        cum = [_dot_exact_by_f32(tri_incl, lwc[g]) for g in P]
        yield
        last = [cum[g][C - 1:C, :] for g in P]
        rt = [rc[g] * jnp.exp(cum[g]) for g in P]
        at = [ac[g] * jnp.exp(cum[g] - lwc[g]) for g in P]
        ginv = [jnp.exp(-cum[g]) for g in P]
        btb = [bf(bc[g] * ginv[g]) for g in P]
        ktb = [bf(kc[g] * ginv[g]) for g in P]
        ghat = [jnp.exp(last[g] - cum[g]) for g in P]
        lhs0 = [bf(cat0(jnp.where(m0, at[g], 0.0), jnp.where(m0, rt[g], 0.0))) for g in P]
        lhs1 = [bf(cat0(jnp.where(m0, 0.0, at[g]), jnp.where(m0, 0.0, rt[g]))) for g in P]
        for g in P:
            ring128[slot, g, BK] = cat0(bf(bc[g] * ghat[g]), bf(kc[g] * ghat[g]))
            ring128[slot, g, ATB] = cat0(lhs0[g][:C], lhs1[g][:C])
            ring64[slot, g, VB] = bf(vc[g])
            ring64[slot, g, RTB] = bf(rt[g])
            ringg[slot, g, 0:1, :] = jnp.exp(last[g])
        x0 = [_dot_nt(lhs0[g], cat0(btb[g], ktb[g])) for g in P]
        x1 = [_dot_nt(lhs1[g], cat0(ktb[g], btb[g])) for g in P]
        yield
        n_bd = [cat0(jnp.where(left & strict, x0[g][:C], 0.0),
                     jnp.where((~left) & strict, x1[g][:C], 0.0)) for g in P]
        for g in P:
            ring128[slot, g, AK] = bf(cat0(jnp.where((~left) & strict, x0[g][:C], 0.0),
                                           jnp.where(left & strict, x1[g][:C], 0.0)))
            ring64[slot, g, MM0] = bf(jnp.where(incl, x0[g][C:], 0.0))
            ring64[slot, g, MM1] = bf(jnp.where(incl, x1[g][C:], 0.0))
        out["t"] = [eye + n_bd[g] for g in P]
        nb = [bf(n_bd[g]) for g in P]
        out["pw"] = [bf(_dot(nb[g], nb[g])) for g in P]

    def inverse_level(t, pw):
        res = [_dot(pw[g], cat1(pw[g], bf(t[g]))) for g in P]
        return [t[g] + res[g][:, LANES:] for g in P], [bf(res[g][:, :LANES]) for g in P]

    def stage2(inp, out):
        t, pw = inp["t"], inp["pw"]
        for level in range(3):
            t, pw = inverse_level(t, pw)
            if level < 2:
                yield
        out["t"], out["pw"] = t, pw

    def stage3(inp, c, out):
        slot = c % 4
        t, pw = inverse_level(inp["t"], inp["pw"])
        vb = [ring64[slot, g, VB] for g in P]
        av = [_dot(ring128[slot, g, AK], cat0(vb[g], vb[g])) for g in P]
        yield
        t = [t[g] + _dot(pw[g], bf(t[g])) for g in P]
        av = [cat0(jnp.where(m0, av[g][:C], 0.0), jnp.where(m0, 0.0, av[g][C:])) for g in P]
        yield
        wu = [_dot(bf(t[g]), cat1(ring128[slot, g, ATB], bf(av[g]))) for g in P]
        out["w"] = [bf(wu[g][:C, :LANES] + wu[g][C:, :LANES]) for g in P]
        out["u0"] = [wu[g][:C, LANES:] + wu[g][C:, LANES:] for g in P]

    def stage4(inp, c, valid):
        rows = pl.ds(pl.multiple_of(c * C, C), C)
        slot = c % 4
        vb = [ring64[slot, g, VB] for g in P]
        s = [s_ref[g] for g in P]
        sb = [bf(s[g]) for g in P]
        ub = [bf(_dot_nt(inp["w"][g], sb[g]) + inp["u0"][g]) for g in P]
        ys = [_dot_nt(ring64[slot, g, RTB], sb[g]) for g in P]
        yield
        uv = [cat0(ub[g], vb[g]) for g in P]
        y = [ys[g] + jnp.where(m0, _dot(ring64[slot, g, MM0], uv[g]),
                               _dot(ring64[slot, g, MM1], cat0(vb[g], ub[g]))) for g in P]
        upd = [_dot_tn(uv[g], ring128[slot, g, BK]) for g in P]
        yield
        for g in P:
            s_new = s[g] * ringg[slot, g, 0:1, :] + jnp.where(blockdiag, upd[g], 0.0)
            s_ref[g] = jnp.where(valid, s_new, s[g])
            y_s[rows, csl[g]] = y[g]

    def interleave(gens):
        live = list(gens)
        while live:
            still = []
            for gen in live:
                try:
                    next(gen)
                    still.append(gen)
                except StopIteration:
                    pass
            live = still

    def body(it, carry):
        o1, o2, o3 = carry
        n1, n2, n3 = {}, {}, {}
        interleave([stage4(o3, jnp.maximum(it - 3, 0), it >= 3),
                    stage3(o2, jnp.maximum(it - 2, 0), n3),
                    stage2(o1, n2),
                    stage1(jnp.minimum(it, n_chunks - 1), n1)])
        return n1, n2, n3

    zero_ref[...] = jnp.zeros_like(zero_ref)
    zb = lambda r: [zero_ref[0:r, :].astype(BF16) for _ in P]
    zf = lambda r: [zero_ref[0:r, :] for _ in P]
    front = lambda: dict(t=zf(2 * C), pw=zb(2 * C))
    lax.fori_loop(0, n_chunks + 3, body, (front(), front(), dict(w=zb(C), u0=zf(C))))

    for g in range(n_pairs):
        cs = slice(g * LANES, (g + 1) * LANES)
        y = y_s[:, cs]
        mean = head_sum(y) * (1.0 / 64.0)
        yc = y - mean
        var = head_sum(yc * yc) * (1.0 / 64.0)
        yn = yc * lax.rsqrt(var + GN_EPS) * lg_ref[:, cs] + lb_ref[:, cs]
        y_ref[:, cs] = ((yn + bo_s[:, cs]) * g_s[:, cs]).astype(BF16)


def _wkv(p_rwkv, prm, batch, seq, n_pairs, tt):
    m = p_rwkv.shape[0]
    c = prm["w0"].shape[1]
    gw = n_pairs * LANES
    n_col_blocks = c // gw
    nt = seq // tt
    low_w = LANES + prm["wg"].shape[0]

    def pspec(off):
        return pl.BlockSpec((tt, gw), lambda b, g, t: (b * nt + t, off * n_col_blocks + g))

    def vspec():
        return pl.BlockSpec((1, gw), lambda b, g, t: (0, g))

    def wspec(rows):
        return pl.BlockSpec((rows, gw), lambda b, g, t: (0, g))

    kern = functools.partial(_wkv_kernel, n_pairs=n_pairs, tt=tt)
    tile = pltpu.VMEM((tt, gw), F32)
    return pl.pallas_call(
        kern,
        grid=(batch, n_col_blocks, nt),
        in_specs=[
            pspec(0), pspec(1), pspec(2),
            pl.BlockSpec((tt, low_w), lambda b, g, t: (b * nt + t, (3 * c) // low_w)),
            vspec(), vspec(), vspec(), vspec(), vspec(), vspec(), vspec(),
            wspec(LANES), wspec(LANES), wspec(low_w - LANES),
        ],
        out_specs=pl.BlockSpec((tt, gw), lambda b, g, t: (b * nt + t, g)),
        out_shape=jax.ShapeDtypeStruct((m, c), BF16),
        scratch_shapes=[
            pltpu.VMEM((n_pairs, LANES, LANES), F32),
            tile, tile, tile, tile, tile, tile, tile, tile, tile,
            pltpu.VMEM((LANES, LANES), F32),
            pltpu.VMEM((4, n_pairs, 4, WKV_CHUNK, LANES), BF16),
            pltpu.VMEM((4, n_pairs, 3, 2 * WKV_CHUNK, LANES), BF16),
            pltpu.VMEM((4, n_pairs, 8, LANES), F32),
        ],
        compiler_params=_cparams(("parallel", "parallel", "arbitrary")),
        name="wkv7",
    )(p_rwkv, p_rwkv, p_rwkv, p_rwkv,
      prm["w0"], prm["a0"], prm["k_k"], prm["k_a"], prm["r_k"], prm["lnx_g"], prm["lnx_b"],
      prm["wd"], prm["wa"], prm["wg"])


def _sb_kernel(q_ref, k_ref, v_ref, o_ref, acc_ref, right_ref, *, tq, scale):
    i = pl.program_id(1)
    row = lax.broadcasted_iota(jnp.int32, (tq, tq), 0)
    col = lax.broadcasted_iota(jnp.int32, (tq, tq), 1)
    r2 = lax.broadcasted_iota(jnp.int32, (tq, 2 * tq), 0)
    c2 = lax.broadcasted_iota(jnp.int32, (tq, 2 * tq), 1)
    after_and_total = jnp.where((c2 >= tq) | (r2 > c2), 1.0, 0.0).astype(BF16)
    diag = col < row

    heads = range(SB_HEADS)
    hsl = [slice(h * LANES, (h + 1) * LANES) for h in heads]

    def blocks(js, first):
        nb = len(js)
        it = [(b, h) for b in range(nb) for h in heads]
        on_diag = lambda p: first and p[0] == 0
        ks = [pl.ds(pl.multiple_of(j * tq, tq), tq) for j in js]
        z = {p: _dot_nt(q_ref[:, hsl[p[1]]], k_ref[ks[p[0]], hsl[p[1]]]) * scale for p in it}
        sp = {p: jnp.maximum(z[p], 0.0) + jnp.log(1.0 + jnp.exp(-jnp.abs(z[p]))) for p in it}
        log_keep = {p: jnp.where(diag, -sp[p], 0.0) if on_diag(p) else -sp[p] for p in it}
        hi = {p: log_keep[p].astype(BF16) for p in it}
        lo = {p: (log_keep[p] - hi[p].astype(F32)).astype(BF16) for p in it}
        sums = {p: _dot(hi[p], after_and_total) + _dot(lo[p], after_and_total) for p in it}
        right = {}
        for h in heads:
            run = None if first else right_ref[h]
            for b in range(nb):
                right[(b, h)] = run
                tot = sums[(b, h)][:, tq:]
                run = tot if run is None else run + tot
            right[("end", h)] = run
        after = {p: sums[p][:, :tq] if right[p] is None else sums[p][:, :tq] + right[p] for p in it}
        attn = {p: jnp.exp(z[p] - sp[p] + after[p]) for p in it}
        attn = {p: jnp.where(diag, attn[p], 0.0) if on_diag(p) else attn[p] for p in it}
        pv = {p: _dot(attn[p].astype(BF16), v_ref[ks[p[0]], hsl[p[1]]]) for p in it}
        for h in heads:
            tot = pv[(0, h)]
            for b in range(1, nb):
                tot = tot + pv[(b, h)]
            if first:
                acc_ref[h] = tot
            else:
                acc_ref[h] += tot
            right_ref[h] = right[("end", h)]

    per = SB_BLOCKS_PER_ITER
    first_size = i % per + 1
    for size in range(1, per + 1):
        @pl.when(first_size == size)
        def _(size=size):
            blocks([i - b for b in range(size)], True)

    def body(jj, _):
        j = i - first_size - per * jj
        blocks([j - b for b in range(per)], False)
        return 0

    lax.fori_loop(0, (i + 1 - first_size) // per, body, 0)

    for h in heads:
        o_ref[:, hsl[h]] = acc_ref[h].astype(BF16)


def _sb_attn(p_attn, batch, seq, tq):
    m = p_attn.shape[0]
    w = SB_HEADS * LANES
    nq = seq // tq
    kern = functools.partial(_sb_kernel, tq=tq, scale=LANES ** -0.5)
    return pl.pallas_call(
        kern,
        grid=(batch, nq),
        in_specs=[
            pl.BlockSpec((tq, w), lambda b, i: (b * nq + i, 0)),
            pl.BlockSpec((seq, w), lambda b, i: (b, 1)),
            pl.BlockSpec((seq, w), lambda b, i: (b, 2)),
        ],
        out_specs=pl.BlockSpec((tq, w), lambda b, i: (b * nq + i, 0)),
        out_shape=jax.ShapeDtypeStruct((m, w), BF16),
        scratch_shapes=[pltpu.VMEM((SB_HEADS, tq, tq), F32), pltpu.VMEM((SB_HEADS, tq, tq), F32)],
        compiler_params=_cparams(("parallel", "arbitrary")),
        name="sb_attn",
    )(p_attn, p_attn, p_attn)


def _mem_kv_kernel(m_ref, w_ref, o_ref):
    o_ref[...] = _dot(m_ref[...].astype(BF16), w_ref[...]).astype(BF16)


def _mem_kv(mem2, w_bf16, tm):
    m, d = mem2.shape
    n = w_bf16.shape[1]
    return pl.pallas_call(
        _mem_kv_kernel,
        grid=(m // tm,),
        in_specs=[pl.BlockSpec((tm, d), lambda i: (i, 0)),
                  pl.BlockSpec((d, n), lambda i: (0, 0))],
        out_specs=pl.BlockSpec((tm, n), lambda i: (i, 0)),
        out_shape=jax.ShapeDtypeStruct((m, n), BF16),
        compiler_params=_cparams(("parallel",)),
        name="mem_kv",
    )(mem2, w_bf16)


def _mem_attn_kernel(q_ref, k_ref, v_ref, o_ref, *, scale):
    for h in range(MEM_HEADS):
        hs = slice(h * LANES, (h + 1) * LANES)
        s = _dot_nt(q_ref[:, hs], k_ref[:, hs]) * scale
        s = s - jnp.max(s, axis=-1, keepdims=True)
        e = jnp.exp(s)
        p = e / jnp.sum(e, axis=-1, keepdims=True)
        o_ref[:, hs] = _dot(p.astype(BF16), v_ref[:, hs]).astype(BF16)


def _mem_attn(p_attn, kv, batch, seq, mem_len, tq):
    m = p_attn.shape[0]
    w = MEM_HEADS * LANES
    nq = seq // tq
    kern = functools.partial(_mem_attn_kernel, scale=LANES ** -0.5)
    return pl.pallas_call(
        kern,
        grid=(batch, nq),
        in_specs=[
            pl.BlockSpec((tq, w), lambda b, i: (b * nq + i, 3)),
            pl.BlockSpec((mem_len, w), lambda b, i: (b, 0)),
            pl.BlockSpec((mem_len, w), lambda b, i: (b, 1)),
        ],
        out_specs=pl.BlockSpec((tq, w), lambda b, i: (b * nq + i, 0)),
        out_shape=jax.ShapeDtypeStruct((m, w), BF16),
        compiler_params=_cparams(("parallel", "parallel")),
        name="mem_attn",
    )(p_attn, kv, kv)


def _outproj_kernel(yr_ref, ys_ref, ym_ref, x_ref, g0_ref, b0_ref, w_ref, g_ref, b_ref, rw_ref, rb_ref,
                    h1_ref, lg_ref):
    c0 = yr_ref.shape[1]
    c1 = c0 + ys_ref.shape[1]
    tm = x_ref.shape[0]
    part = min(tm, 256)
    for r0 in range(0, tm, part):
        rs = slice(r0, r0 + part)
        mix = (_dot(yr_ref[rs, :], w_ref[0:c0, :]) + _dot(ys_ref[rs, :], w_ref[c0:c1, :])
               + _dot(ym_ref[rs, :], w_ref[c1:, :]))
        h = _layer_norm(x_ref[rs, :], g0_ref[...], b0_ref[...])
        h1 = _layer_norm(DEEPNORM_ALPHA * h + mix, g_ref[...], b_ref[...])
        h1_ref[rs, :] = h1
        hi = h1.astype(BF16)
        lo = (h1 - hi.astype(F32)).astype(BF16)
        both = _dot(hi, rw_ref[...])
        lg_ref[rs, :] = both[:, :LANES] + both[:, LANES:] + _dot(lo, rw_ref[:, :LANES]) + rb_ref[...]


def _outproj(y_r, y_s, y_m, x2, g0, b0, w, g, b, r_w, r_b, tm):
    m, d = x2.shape
    full = lambda a: pl.BlockSpec(a.shape, lambda i: (0, 0))
    rows = lambda a: pl.BlockSpec((tm, a.shape[1]), lambda i: (i, 0))
    return pl.pallas_call(
        _outproj_kernel,
        grid=(m // tm,),
        in_specs=[rows(y_r), rows(y_s), rows(y_m), rows(x2), full(g0), full(b0), full(w), full(g), full(b),
                  full(r_w), full(r_b)],
        out_specs=[pl.BlockSpec((tm, d), lambda i: (i, 0)),
                   pl.BlockSpec((tm, LANES), lambda i: (i, 0))],
        out_shape=[jax.ShapeDtypeStruct((m, d), F32),
                   jax.ShapeDtypeStruct((m, LANES), F32)],
        compiler_params=_cparams(("parallel",)),
        name="outproj",
    )(y_r, y_s, y_m, x2, g0, b0, w, g, b, r_w, r_b)


def _route_kernel(lg_ref, id_ref, wt_ref, *, n_groups, per_group):
    lg = lg_ref[...]
    lane_i = lax.broadcasted_iota(jnp.int32, lg.shape, 1)
    lane = lane_i.astype(F32)
    neg = jnp.float32(-jnp.inf)
    big = jnp.float32(2 ** 20)

    def first_max(vals):
        mx = jnp.max(vals, axis=-1, keepdims=True)
        idx = jnp.min(jnp.where(vals == mx, lane, big), axis=-1, keepdims=True)
        return mx, idx

    is_group = lane < n_groups
    gmax, gidx = first_max(jnp.where(is_group, lg, neg))
    gsum = jnp.sum(jnp.where(is_group, jnp.exp(lg - gmax), 0.0), axis=-1, keepdims=True)
    group_w = 1.0 / gsum
    lo = n_groups + gidx * per_group
    in_group = (lane >= lo) & (lane < lo + per_group)
    v1, i1 = first_max(jnp.where(in_group, lg, neg))
    v2, i2 = first_max(jnp.where(in_group & (lane != i1), lg, neg))
    e2 = jnp.exp(v2 - v1)
    w1 = group_w / (1.0 + e2)
    w2 = group_w * e2 / (1.0 + e2)
    ids = jnp.where(lane_i == 0, i1 - n_groups, jnp.where(lane_i == 1, i2 - n_groups, 0.0))
    id_ref[...] = ids.T[0:8, :].astype(jnp.int32)
    wt_ref[...] = jnp.where(lane_i == 0, w1, jnp.where(lane_i == 1, w2, 0.0))


def _route(logits, n_groups, per_group, tm):
    m = logits.shape[0]
    kern = functools.partial(_route_kernel, n_groups=n_groups, per_group=per_group)
    spec = pl.BlockSpec((tm, LANES), lambda i: (i, 0))
    return pl.pallas_call(
        kern,
        grid=(m // tm,),
        in_specs=[spec],
        out_specs=[pl.BlockSpec((8, tm), lambda i: (0, i)), spec],
        out_shape=[jax.ShapeDtypeStruct((8, m), jnp.int32),
                   jax.ShapeDtypeStruct((m, LANES), F32)],
        compiler_params=_cparams(("parallel",)),
        name="route",
    )(logits)


def _plan_kernel(id_ref, pos_ref, meta_ref, cnt_ref, base_ref, start_ref, *, tb, rows, n_experts, nb):
    phase = pl.program_id(0)
    j = pl.program_id(1)
    sub = lax.broadcasted_iota(jnp.int32, (LANES, tb), 0)
    e1 = id_ref[0:1, :]
    e2 = id_ref[1:2, :]
    hit1 = jnp.where(sub == e1, 1.0, 0.0)
    hit2 = jnp.where(sub == e2, 1.0, 0.0)
    hits = hit1 + hit2
    per_expert = jnp.sum(hits, axis=1, keepdims=True)

    @pl.when((phase == 0) & (j == 0))
    def _():
        cnt_ref[...] = jnp.zeros_like(cnt_ref)

    @pl.when(phase == 0)
    def _():
        cnt_ref[...] += per_expert

    sq_r = lax.broadcasted_iota(jnp.int32, (LANES, LANES), 0)
    sq_c = lax.broadcasted_iota(jnp.int32, (LANES, LANES), 1)

    @pl.when((phase == 1) & (j == 0))
    def _():
        n_blk = jnp.floor((cnt_ref[...] + (rows - 1)) * (1.0 / rows))
        before = jnp.where(sq_c < sq_r, 1.0, 0.0).astype(BF16)
        blk_start = _dot(before, jnp.broadcast_to(n_blk, (LANES, LANES)).astype(BF16))
        start_ref[...] = blk_start[:, 0:1]
        base_ref[...] = jnp.zeros_like(base_ref)
        blk_end = blk_start + n_blk
        n_used = jnp.sum(jnp.where(sq_r[:, 0:1] < n_experts, n_blk, 0.0), axis=0, keepdims=True)
        owner = jnp.sum(jnp.where((sq_r < n_experts) & (blk_end <= sq_c.astype(F32)), 1.0, 0.0),
                        axis=0, keepdims=True)
        last_owner = jnp.max(jnp.where(n_blk > 0.0, sq_r[:, 0:1].astype(F32), 0.0), axis=0, keepdims=True)
        blk = sq_c[0:1, :].astype(F32)
        in_use = blk < n_used
        owner = jnp.where(in_use, owner, last_owner)
        mine = sq_r.astype(F32) == owner
        cnt_o = jnp.sum(jnp.where(mine, cnt_ref[...], 0.0), axis=0, keepdims=True)
        start_o = jnp.sum(jnp.where(mine, blk_start, 0.0), axis=0, keepdims=True)
        valid = jnp.clip(cnt_o - (blk - start_o) * rows, 0.0, float(rows))
        valid = jnp.where(in_use, valid, 0.0)
        row8 = lax.broadcasted_iota(jnp.int32, (8, LANES), 0)
        meta = jnp.where(row8 == 0, owner, jnp.where(row8 == 1, valid, jnp.where(row8 == 2, n_used, 0.0)))
        meta_ref[...] = meta.astype(jnp.int32)

    @pl.when(phase == 1)
    def _():
        tr = lax.broadcasted_iota(jnp.int32, (tb, tb), 0)
        tc = lax.broadcasted_iota(jnp.int32, (tb, tb), 1)
        earlier = jnp.where(tr < tc, 1.0, 0.0).astype(BF16)
        seen = _dot(hits.astype(BF16), earlier)
        slot = start_ref[...] * rows + base_ref[...] + seen
        p1 = jnp.sum(hit1 * slot, axis=0, keepdims=True)
        p2 = jnp.sum(hit2 * slot, axis=0, keepdims=True)
        row8 = lax.broadcasted_iota(jnp.int32, (8, tb), 0)
        pos_ref[...] = jnp.where(row8 == 0, p1, jnp.where(row8 == 1, p2, 0.0)).astype(jnp.int32)
        base_ref[...] += per_expert


def _plan(ids_t, n_experts, rows, tb):
    m = ids_t.shape[1]
    nb = m // tb
    kern = functools.partial(_plan_kernel, tb=tb, rows=rows, n_experts=n_experts, nb=nb)
    col = pltpu.VMEM((LANES, 1), F32)
    return pl.pallas_call(
        kern,
        grid=(2, nb),
        in_specs=[pl.BlockSpec((8, tb), lambda p, j: (0, j))],
        out_specs=[pl.BlockSpec((8, tb), lambda p, j: (0, j * p)),
                   pl.BlockSpec((8, LANES), lambda p, j: (0, 0))],
        out_shape=[jax.ShapeDtypeStruct((8, m), jnp.int32),
                   jax.ShapeDtypeStruct((8, LANES), jnp.int32)],
        scratch_shapes=[col, col, col],
        compiler_params=_cparams(("arbitrary", "arbitrary")),
        name="moe_plan",
    )(ids_t)


def _for_range(lo, hi, body):
    full = (hi - lo) // ROW_GROUP
    first = lo // ROW_GROUP

    def group(g, _):
        for u in range(ROW_GROUP):
            body(first + g, u)
        return 0

    lax.fori_loop(0, full, group, 0)

    def one(r, _):
        body(r // ROW_GROUP, r % ROW_GROUP)
        return 0

    lax.fori_loop(lo + full * ROW_GROUP, hi, one, 0)


def _row_in(src_hbm, dst_vmem, sem, src_row, tile, sub):
    return pltpu.make_async_copy(src_hbm.at[pl.ds(src_row, 1)], dst_vmem.at[tile, pl.ds(sub, 1)], sem)


def _row_out(src_vmem, dst_hbm, sem, tile, sub, dst_row):
    return pltpu.make_async_copy(src_vmem.at[tile, pl.ds(sub, 1)], dst_hbm.at[pl.ds(dst_row, 1)], sem)


def _gather_cparams(sem):
    return pltpu.CompilerParams(dimension_semantics=sem, vmem_limit_bytes=VMEM_LIMIT,
                                disable_bounds_checks=True)


def _ffn_kernel(asg_ref, be_ref, nv_ref, nu_ref, h_hbm, wg_ref, wu_ref, wd_ref, y_hbm,
                xf_ref, xb_ref, acc_ref, sem_in, sem_out, *, rows, sub, nj, n_tok, n_blocks):
    i = pl.program_id(0)
    j = pl.program_id(1)
    n_used = nu_ref[0]
    used = i < n_used
    slot = i % 2

    def gather(block, lo, hi, buf):
        def body(tile, sub_row):
            a = asg_ref[block * rows + tile * ROW_GROUP + sub_row]
            tok = jnp.where(a >= n_tok, a - n_tok, a)
            _row_in(h_hbm, xf_ref.at[buf], sem_in.at[buf], tok, tile, sub_row).start()

        _for_range(lo, hi, body)

    def gather_wait(block, buf):
        _for_range(0, nv_ref[block],
                   lambda tile, sub_row: _row_in(h_hbm, xf_ref.at[buf], sem_in.at[buf], 0, tile, sub_row).wait())

    def emit(block, buf):
        def body(tile, sub_row):
            a = asg_ref[block * rows + tile * ROW_GROUP + sub_row]
            _row_out(acc_ref.at[buf], y_hbm, sem_out.at[buf], tile, sub_row, a).start()

        _for_range(0, nv_ref[block], body)

    def emit_wait(block, buf):
        _for_range(0, nv_ref[block],
                   lambda tile, sub_row: _row_out(acc_ref.at[buf], y_hbm, sem_out.at[buf], tile, sub_row, 0).wait())

    def as_rows(x):
        return x.reshape(x.shape[0] * ROW_GROUP, x.shape[2])

    def as_tiles(x):
        return x.reshape(x.shape[0] // ROW_GROUP, ROW_GROUP, x.shape[1])

    @pl.when((i == 0) & (j == 0))
    def _():
        xf_ref[...] = jnp.zeros_like(xf_ref)
        gather(0, 0, nv_ref[0], 0)

    @pl.when(used & (j == 0))
    def _():
        gather_wait(i, slot)

    @pl.when(i + 1 < n_used)
    def _():
        nxt = nv_ref[i + 1]
        q = rows // nj
        gather(i + 1, jnp.minimum(j * q, nxt), jnp.minimum((j + 1) * q, nxt), 1 - slot)

    n_sub = (nv_ref[i] + sub - 1) // sub
    for n in range(1, rows // sub + 1):
        @pl.when(used & (n_sub == n))
        def _(n=n):
            r = n * sub
            nt = r // ROW_GROUP

            @pl.when(j == 0)
            def _():
                xb_ref[0:r, :] = as_rows(xf_ref[slot, 0:nt]).astype(BF16)
                acc_ref[slot, 0:nt] = jnp.zeros((nt,) + acc_ref.shape[2:], F32)

            xb = xb_ref[0:r, :]
            gate = _dot(xb, wg_ref[...].astype(BF16))
            up = _dot(xb, wu_ref[...].astype(BF16))
            hid = (gate * jax.nn.sigmoid(gate)) * up
            acc_ref[slot, 0:nt] += as_tiles(_dot(hid.astype(BF16), wd_ref[...].astype(BF16)))

    @pl.when((j == nj - 1) & (i >= 1) & (i <= n_used))
    def _():
        emit_wait(i - 1, 1 - slot)

    @pl.when((j == nj - 1) & used)
    def _():
        emit(i, slot)

    @pl.when((j == nj - 1) & used & (i == n_blocks - 1))
    def _():
        emit_wait(i, slot)


def _moe_ffn(slot_asg, block_expert, n_valid, n_used, h1, w_gate, w_up, w_down, n_blocks, rows, tf):
    n_tok, d = h1.shape
    de = w_gate.shape[2]
    nj = de // tf
    kern = functools.partial(_ffn_kernel, rows=rows, sub=MOE_SUB, nj=nj, n_tok=n_tok, n_blocks=n_blocks)

    def jidx(i, j, nu):
        return jnp.where(i < nu[0], j, nj - 1)

    return pl.pallas_call(
        kern,
        grid_spec=pltpu.PrefetchScalarGridSpec(
            num_scalar_prefetch=4,
            grid=(n_blocks, nj),
            in_specs=[
                pl.BlockSpec(memory_space=pl.ANY),
                pl.BlockSpec((None, d, tf), lambda i, j, asg, be, nv, nu: (be[i], 0, jidx(i, j, nu))),
                pl.BlockSpec((None, d, tf), lambda i, j, asg, be, nv, nu: (be[i], 0, jidx(i, j, nu))),
                pl.BlockSpec((None, tf, d), lambda i, j, asg, be, nv, nu: (be[i], jidx(i, j, nu), 0)),
            ],
            out_specs=pl.BlockSpec(memory_space=pl.ANY),
            scratch_shapes=[pltpu.VMEM((2, rows // ROW_GROUP, ROW_GROUP, d), F32),
                            pltpu.VMEM((rows, d), BF16),
                            pltpu.VMEM((2, rows // ROW_GROUP, ROW_GROUP, d), F32),
                            pltpu.SemaphoreType.DMA((2,)),
                            pltpu.SemaphoreType.DMA((2,))],
        ),
        out_shape=jax.ShapeDtypeStruct((2 * n_tok, d), F32),
        compiler_params=_gather_cparams(("arbitrary", "arbitrary")),
        name="moe_ffn",
    )(slot_asg, block_expert, n_valid, n_used, h1, w_gate, w_up, w_down)


def _combine_kernel(h_ref, y0_ref, y1_ref, wt_ref, g_ref, b_ref, o_ref):
    wt = wt_ref[...]
    ffn = y0_ref[...] * wt[:, 0:1] + y1_ref[...] * wt[:, 1:2]
    o_ref[...] = _layer_norm(DEEPNORM_ALPHA * h_ref[...] + ffn, g_ref[...], b_ref[...])


def _combine(h1, ys, wts, g, b, tm):
    m, d = h1.shape
    nt = m // tm
    rows = pl.BlockSpec((tm, d), lambda i: (i, 0))
    vec = pl.BlockSpec((1, d), lambda i: (0, 0))
    return pl.pallas_call(
        _combine_kernel,
        grid=(nt,),
        in_specs=[rows, rows, pl.BlockSpec((tm, d), lambda i: (nt + i, 0)),
                  pl.BlockSpec((tm, LANES), lambda i: (i, 0)), vec, vec],
        out_specs=rows,
        out_shape=jax.ShapeDtypeStruct((m, d), F32),
        compiler_params=_cparams(("parallel",)),
        name="moe_combine",
    )(h1, ys, ys, wts, g, b)


def _dispatch_plan(ids_t, n_experts, rows):
    m = ids_t.shape[1]
    n_blocks = -(-(2 * m) // rows) + n_experts
    assert n_blocks <= LANES
    pos_t, meta = _plan(ids_t, n_experts, rows, _pick(m, 512))
    pos = pos_t[:2].reshape(-1)
    slot_asg = jnp.zeros((n_blocks * rows,), jnp.int32).at[pos].set(jnp.arange(2 * m, dtype=jnp.int32))
    return slot_asg, meta[2, :1], meta[0, :n_blocks], meta[1, :n_blocks], n_blocks


def _pick(n, pref):
    t = min(pref, n)
    while n % t:
        t //= 2
    return t


def kernel(x, mem, ln_in_g, ln_in_b, w_in, tshift_mu, w0, w_decay_up, a0, w_a_up, w_g_up, k_k, k_a, r_k,
           lnx_g, lnx_b, w_mem_kv, w_out, ln1_g, ln1_b, router_group, router_group_b, router_expert,
           router_expert_b, w_e_gate, w_e_up, w_e_down, ln2_g, ln2_b):
    batch, seq, d = x.shape
    mem_len = mem.shape[1]
    m = batch * seq
    c = w0.shape[1]
    dr, ar, gr = w_decay_up.shape[1], w_a_up.shape[1], w_g_up.shape[1]
    rwkv_cols = 3 * c + dr + ar + gr
    sb_w = SB_HEADS * LANES
    mem_w = MEM_HEADS * LANES
    assert dr + ar == LANES and c % (2 * LANES) == 0 and w_in.shape[0] == DEPTH
    assert w_in.shape[2] == rwkv_cols + 3 * sb_w + mem_w
    n_experts = router_expert.shape[2]
    row = lambda a: a.reshape(1, -1)

    tn = 512
    low_w = -(-(dr + ar + gr) // LANES) * LANES
    rw_pad = -(-(3 * c + low_w) // tn) * tn
    wi = w_in[0]
    w_rwkv = jnp.pad(wi[:, :rwkv_cols].astype(BF16), ((0, 0), (0, rw_pad - rwkv_cols)))
    w_attn = wi[:, rwkv_cols:].astype(BF16)
    mu = row(jnp.pad(tshift_mu[0], (0, rw_pad - rwkv_cols)))
    lp = low_w - (dr + ar + gr)
    prm = dict(
        w0=row(w0[0]), a0=row(a0[0]), k_k=row(k_k[0]), k_a=row(k_a[0]), r_k=row(r_k[0]),
        lnx_g=row(lnx_g[0]), lnx_b=row(lnx_b[0]),
        wd=jnp.concatenate([w_decay_up[0], jnp.zeros((ar, c), F32)], axis=0).astype(BF16),
        wa=jnp.concatenate([jnp.zeros((dr, c), F32), w_a_up[0]], axis=0).astype(BF16),
        wg=jnp.concatenate([w_g_up[0], jnp.zeros((lp, c), F32)], axis=0).astype(BF16),
    )
    assert (3 * c) % low_w == 0

    x2 = x.reshape(m, d)
    hb = _ln_in(x2, row(ln_in_g), row(ln_in_b), _pick(m, 512))
    p_rwkv, p_attn = _inproj(hb, w_rwkv, w_attn, mu, seq, _pick(seq, 2048), tn)

    y_rwkv = _wkv(p_rwkv, prm, batch, seq, n_pairs=4, tt=_pick(seq, 1024))
    y_sb = _sb_attn(p_attn, batch, seq, tq=LANES)
    kv = _mem_kv(mem.reshape(batch * mem_len, d), w_mem_kv[0].astype(BF16), _pick(batch * mem_len, 256))
    y_mem = _mem_attn(p_attn, kv, batch, seq, mem_len, tq=_pick(seq, 512))

    wo = w_out[0].astype(BF16)
    r_w = jnp.concatenate([router_group[0], router_expert[0],
                           jnp.zeros((d, LANES - N_GROUPS - n_experts), F32)], axis=1)
    r_b = jnp.concatenate([router_group_b[0], router_expert_b[0],
                           jnp.zeros((LANES - N_GROUPS - n_experts,), F32)]).reshape(1, LANES)
    r_hi = r_w.astype(BF16)
    r_w2 = jnp.concatenate([r_hi, (r_w - r_hi.astype(F32)).astype(BF16)], axis=1)
    h1, logits = _outproj(y_rwkv, y_sb, y_mem, x2, row(ln_in_g), row(ln_in_b), wo,
                          row(ln1_g[0]), row(ln1_b[0]), r_w2, r_b, _pick(m, 512))

    ids_t, wts = _route(logits, N_GROUPS, n_experts // N_GROUPS, _pick(m, 512))
    slot_asg, n_used, block_expert, n_valid, n_blocks = _dispatch_plan(ids_t, n_experts, MOE_ROWS)
    ys = _moe_ffn(slot_asg, block_expert, n_valid, n_used, h1, w_e_gate[0], w_e_up[0], w_e_down[0],
                  n_blocks, MOE_ROWS, tf=512)
    out = _combine(h1, ys, wts, row(ln2_g[0]), row(ln2_b[0]), _pick(m, 256))
    return out.reshape(batch, seq, d)
```

```python
import functools

import jax
import jax.numpy as jnp
from jax import lax
from jax.experimental import pallas as pl
from jax.experimental.pallas import tpu as pltpu

F32 = jnp.float32
BF16 = jnp.bfloat16

SB_HEADS = 4
MEM_HEADS = 4
N_GROUPS = 8
DEPTH = 1
DEEPNORM_ALPHA = (2.0 * DEPTH) ** 0.25
LN_EPS = 1e-5
GN_EPS = 64e-5

LANES = 128
WKV_CHUNK = 64
MOE_ROWS = 512
MOE_SUB = 64
ROW_GROUP = 8
SB_BLOCKS_PER_ITER = 4
VMEM_LIMIT = 56 * 1024 * 1024


def _cparams(sem):
    return pltpu.CompilerParams(dimension_semantics=sem, vmem_limit_bytes=VMEM_LIMIT)


def _layer_norm(x, g, b):
    mu = jnp.mean(x, axis=-1, keepdims=True)
    xc = x - mu
    var = jnp.mean(xc * xc, axis=-1, keepdims=True)
    return xc * lax.rsqrt(var + LN_EPS) * g + b


def _split2(x):
    hi = x.astype(BF16)
    return hi, (x - hi.astype(F32)).astype(BF16)


def _dot(a, b):
    return jnp.dot(a, b, preferred_element_type=F32)


def _dot_nt(a, b):
    return lax.dot_general(a, b, (((1,), (1,)), ((), ())), preferred_element_type=F32)


def _dot_tn(a, b):
    return lax.dot_general(a, b, (((0,), (0,)), ((), ())), preferred_element_type=F32)


def _dot_exact_by_f32(m, x):
    hi, lo = _split2(x)
    return _dot(m, hi) + _dot(m, lo)


def _ln_kernel(x_ref, g_ref, b_ref, hb_ref):
    hb_ref[...] = _layer_norm(x_ref[...], g_ref[...], b_ref[...]).astype(BF16)


def _ln_in(x2, g, b, tm):
    m, d = x2.shape
    rows = pl.BlockSpec((tm, d), lambda i: (i, 0))
    vec = pl.BlockSpec((1, d), lambda i: (0, 0))
    return pl.pallas_call(
        _ln_kernel,
        grid=(m // tm,),
        in_specs=[rows, vec, vec],
        out_specs=rows,
        out_shape=jax.ShapeDtypeStruct((m, d), BF16),
        compiler_params=_cparams(("parallel",)),
        name="ln_in",
    )(x2, g, b)


def _inproj_kernel(hb_ref, wr_ref, wa_ref, mu_ref, pr_ref, pa_ref, last_ref, *, n_f32_tiles, tiles_per_seq):
    i = pl.program_id(0)
    n = pl.program_id(1)

    @pl.when((i == 0) & (n == 0))
    def _():
        last_ref[...] = jnp.zeros_like(last_ref)

    @pl.when(n < n_f32_tiles)
    def _():
        tm = pr_ref.shape[0]
        p = _dot(hb_ref[...], wr_ref[...])
        slot = jnp.minimum(n, n_f32_tiles - 1)
        carried = jnp.where(i % tiles_per_seq == 0, 0.0, last_ref[slot])
        prev = pltpu.roll(p, shift=1, axis=0)
        prev = jnp.where(lax.broadcasted_iota(jnp.int32, (tm, 1), 0) == 0, carried, prev)
        last_ref[slot] = p[tm - 1:tm, :]
        pr_ref[...] = p + (prev - p) * mu_ref[...]

    @pl.when(n >= n_f32_tiles)
    def _():
        pa_ref[...] = _dot(hb_ref[...], wa_ref[...]).astype(BF16)


def _inproj(hb, w_rwkv, w_attn, mu, seq, tm, tn):
    m, d = hb.shape
    n_rwkv_cols, n_attn_cols = w_rwkv.shape[1], w_attn.shape[1]
    nf = n_rwkv_cols // tn
    kern = functools.partial(_inproj_kernel, n_f32_tiles=nf, tiles_per_seq=seq // tm)
    first = lambda i, n: (0, jnp.minimum(n, nf - 1))
    second = lambda i, n: (0, jnp.maximum(n - nf, 0))
    return pl.pallas_call(
        kern,
        grid=(m // tm, (n_rwkv_cols + n_attn_cols) // tn),
        in_specs=[
            pl.BlockSpec((tm, d), lambda i, n: (i, 0)),
            pl.BlockSpec((d, tn), first),
            pl.BlockSpec((d, tn), second),
            pl.BlockSpec((1, tn), first),
        ],
        out_specs=[
            pl.BlockSpec((tm, tn), lambda i, n: (i, jnp.minimum(n, nf - 1))),
            pl.BlockSpec((tm, tn), lambda i, n: (i, jnp.maximum(n - nf, 0))),
        ],
        out_shape=[
            jax.ShapeDtypeStruct((m, n_rwkv_cols), F32),
            jax.ShapeDtypeStruct((m, n_attn_cols), BF16),
        ],
        scratch_shapes=[pltpu.VMEM((nf, 1, tn), F32)],
        compiler_params=_cparams(("arbitrary", "arbitrary")),
        name="inproj",
    )(hb, w_rwkv, w_attn, mu)


def _wkv_kernel(pr_ref, pk_ref, pv_ref, pl_ref,
                w0_ref, a0_ref, kk_ref, ka_ref, rk_ref, lg_ref, lb_ref,
                wd_ref, wa_ref, wg_ref,
                y_ref,
                s_ref,
                r_s, lw_s, k_s, v_s, a_s, b_s, y_s, g_s, bo_s, zero_ref, ring64, ring128, ringg,
                *, n_pairs, tt):
    C = WKV_CHUNK
    t_idx = pl.program_id(2)

    @pl.when(t_idx == 0)
    def _():
        s_ref[...] = jnp.zeros_like(s_ref)

    lane = lax.broadcasted_iota(jnp.int32, (LANES, LANES), 1)
    sub = lax.broadcasted_iota(jnp.int32, (LANES, LANES), 0)
    head_ones = jnp.where((lane // 64) == (sub // 64), 1.0, 0.0).astype(BF16)

    def head_sum(x):
        hi = x.astype(BF16)
        lo = (x - hi.astype(F32)).astype(BF16)
        return _dot(hi, head_ones) + _dot(lo, head_ones)

    da = pl_ref[:, 0:LANES]
    th = jnp.tanh(da).astype(BF16)
    sg = jax.nn.sigmoid(pl_ref[:, LANES:]).astype(BF16)
    da = da.astype(BF16)
    for g in range(n_pairs):
        cs = slice(g * LANES, (g + 1) * LANES)
        rg, kg, vg = pr_ref[:, cs], pk_ref[:, cs], pv_ref[:, cs]
        pre = w0_ref[:, cs] + _dot(th, wd_ref[:, cs])
        softplus_neg = jnp.maximum(-pre, 0.0) + jnp.log(1.0 + jnp.exp(-jnp.abs(pre)))
        w_log = -softplus_neg - 0.5
        lw = -jnp.exp(w_log)
        a = jax.nn.sigmoid(a0_ref[:, cs] + _dot(da, wa_ref[:, cs]))
        gate = _dot(sg, wg_ref[:, cs])
        kk = kg * kk_ref[:, cs]
        kk = kk * lax.rsqrt(jnp.maximum(head_sum(kk * kk), 1e-24))
        k2 = kg * (1.0 + (a - 1.0) * ka_ref[:, cs])
        bonus = head_sum(rg * k2 * rk_ref[:, cs]) * vg
        r_s[:, cs] = rg
        lw_s[:, cs] = lw
        k_s[:, cs] = k2
        v_s[:, cs] = vg
        a_s[:, cs] = -kk
        b_s[:, cs] = kk * a
        g_s[:, cs] = gate
        bo_s[:, cs] = bonus

    ci = lax.broadcasted_iota(jnp.int32, (C, 2 * C), 0)
    cj = lax.broadcasted_iota(jnp.int32, (C, 2 * C), 1)
    left = cj < C
    strict = (cj % C) < ci
    incl = (cj % C) <= ci
    tri_incl = jnp.where(lax.broadcasted_iota(jnp.int32, (C, C), 1)
                         <= lax.broadcasted_iota(jnp.int32, (C, C), 0), 1.0, 0.0).astype(BF16)
    lane_c = lax.broadcasted_iota(jnp.int32, (C, LANES), 1)
    m0 = lane_c < 64
    eye = jnp.where(lane == sub, 1.0, 0.0).astype(F32)
    blockdiag = (lane // 64) == (sub // 64)

    csl = [slice(g * LANES, (g + 1) * LANES) for g in range(n_pairs)]
    P = range(n_pairs)
    cat0 = lambda *xs: jnp.concatenate(xs, axis=0)
    cat1 = lambda *xs: jnp.concatenate(xs, axis=1)
    bf = lambda x: x.astype(BF16)

    n_chunks = tt // C
    MM0, MM1, RTB, VB = range(4)
    BK, AK, ATB = range(3)
    ring64[...] = jnp.zeros_like(ring64)
    ring128[...] = jnp.zeros_like(ring128)
    ringg[...] = jnp.zeros_like(ringg)

    def stage1(c, out):
        rows = pl.ds(pl.multiple_of(c * C, C), C)
        slot = c % 4
        ld = lambda ref: [ref[rows, csl[g]] for g in P]
        rc, lwc, kc, vc, ac, bc = ld(r_s), ld(lw_s), ld(k_s), ld(v_s), ld(a_s), ld(b_s)
        cum = [_dot_exact_by_f32(tri_incl, lwc[g]) for g in P]
        yield
        last = [cum[g][C - 1:C, :] for g in P]
        rt = [rc[g] * jnp.exp(cum[g]) for g in P]
        at = [ac[g] * jnp.exp(cum[g] - lwc[g]) for g in P]
        ginv = [jnp.exp(-cum[g]) for g in P]
        btb = [bf(bc[g] * ginv[g]) for g in P]
        ktb = [bf(kc[g] * ginv[g]) for g in P]
        ghat = [jnp.exp(last[g] - cum[g]) for g in P]
        lhs0 = [bf(cat0(jnp.where(m0, at[g], 0.0), jnp.where(m0, rt[g], 0.0))) for g in P]
        lhs1 = [bf(cat0(jnp.where(m0, 0.0, at[g]), jnp.where(m0, 0.0, rt[g]))) for g in P]
        for g in P:
            ring128[slot, g, BK] = cat0(bf(bc[g] * ghat[g]), bf(kc[g] * ghat[g]))
            ring128[slot, g, ATB] = cat0(lhs0[g][:C], lhs1[g][:C])
            ring64[slot, g, VB] = bf(vc[g])
            ring64[slot, g, RTB] = bf(rt[g])
            ringg[slot, g, 0:1, :] = jnp.exp(last[g])
        x0 = [_dot_nt(lhs0[g], cat0(btb[g], ktb[g])) for g in P]
        x1 = [_dot_nt(lhs1[g], cat0(ktb[g], btb[g])) for g in P]
        yield
        n_bd = [cat0(jnp.where(left & strict, x0[g][:C], 0.0),
                     jnp.where((~left) & strict, x1[g][:C], 0.0)) for g in P]
        for g in P:
            ring128[slot, g, AK] = bf(cat0(jnp.where((~left) & strict, x0[g][:C], 0.0),
                                           jnp.where(left & strict, x1[g][:C], 0.0)))
            ring64[slot, g, MM0] = bf(jnp.where(incl, x0[g][C:], 0.0))
            ring64[slot, g, MM1] = bf(jnp.where(incl, x1[g][C:], 0.0))
        out["t"] = [eye + n_bd[g] for g in P]
        nb = [bf(n_bd[g]) for g in P]
        out["pw"] = [bf(_dot(nb[g], nb[g])) for g in P]

    def inverse_level(t, pw):
        res = [_dot(pw[g], cat1(pw[g], bf(t[g]))) for g in P]
        return [t[g] + res[g][:, LANES:] for g in P], [bf(res[g][:, :LANES]) for g in P]

    def stage2(inp, out):
        t, pw = inp["t"], inp["pw"]
        for level in range(3):
            t, pw = inverse_level(t, pw)
            if level < 2:
                yield
        out["t"], out["pw"] = t, pw

    def stage3(inp, c, out):
        slot = c % 4
        t, pw = inverse_level(inp["t"], inp["pw"])
        vb = [ring64[slot, g, VB] for g in P]
        av = [_dot(ring128[slot, g, AK], cat0(vb[g], vb[g])) for g in P]
        yield
        t = [t[g] + _dot(pw[g], bf(t[g])) for g in P]
        av = [cat0(jnp.where(m0, av[g][:C], 0.0), jnp.where(m0, 0.0, av[g][C:])) for g in P]
        yield
        wu = [_dot(bf(t[g]), cat1(ring128[slot, g, ATB], bf(av[g]))) for g in P]
        out["w"] = [bf(wu[g][:C, :LANES] + wu[g][C:, :LANES]) for g in P]
        out["u0"] = [wu[g][:C, LANES:] + wu[g][C:, LANES:] for g in P]

    def stage4(inp, c, valid):
        rows = pl.ds(pl.multiple_of(c * C, C), C)
        slot = c % 4
        vb = [ring64[slot, g, VB] for g in P]
        s = [s_ref[g] for g in P]
        sb = [bf(s[g]) for g in P]
        ub = [bf(_dot_nt(inp["w"][g], sb[g]) + inp["u0"][g]) for g in P]
        ys = [_dot_nt(ring64[slot, g, RTB], sb[g]) for g in P]
        yield
        uv = [cat0(ub[g], vb[g]) for g in P]
        y = [ys[g] + jnp.where(m0, _dot(ring64[slot, g, MM0], uv[g]),
                               _dot(ring64[slot, g, MM1], cat0(vb[g], ub[g]))) for g in P]
        upd = [_dot_tn(uv[g], ring128[slot, g, BK]) for g in P]
        yield
        for g in P:
            s_new = s[g] * ringg[slot, g, 0:1, :] + jnp.where(blockdiag, upd[g], 0.0)
            s_ref[g] = jnp.where(valid, s_new, s[g])
            y_s[rows, csl[g]] = y[g]

    def interleave(gens):
        live = list(gens)
        while live:
            still = []
            for gen in live:
                try:
                    next(gen)
                    still.append(gen)
                except StopIteration:
                    pass
            live = still

    def body(it, carry):
        o1, o2, o3 = carry
        n1, n2, n3 = {}, {}, {}
        interleave([stage4(o3, jnp.maximum(it - 3, 0), it >= 3),
                    stage3(o2, jnp.maximum(it - 2, 0), n3),
                    stage2(o1, n2),
                    stage1(jnp.minimum(it, n_chunks - 1), n1)])
        return n1, n2, n3

    zero_ref[...] = jnp.zeros_like(zero_ref)
    zb = lambda r: [zero_ref[0:r, :].astype(BF16) for _ in P]
    zf = lambda r: [zero_ref[0:r, :] for _ in P]
    front = lambda: dict(t=zf(2 * C), pw=zb(2 * C))
    lax.fori_loop(0, n_chunks + 3, body, (front(), front(), dict(w=zb(C), u0=zf(C))))

    for g in range(n_pairs):
        cs = slice(g * LANES, (g + 1) * LANES)
        y = y_s[:, cs]
        mean = head_sum(y) * (1.0 / 64.0)
        yc = y - mean
        var = head_sum(yc * yc) * (1.0 / 64.0)
        yn = yc * lax.rsqrt(var + GN_EPS) * lg_ref[:, cs] + lb_ref[:, cs]
        y_ref[:, cs] = ((yn + bo_s[:, cs]) * g_s[:, cs]).astype(BF16)


def _wkv(p_rwkv, prm, batch, seq, n_pairs, tt):
    m = p_rwkv.shape[0]
    c = prm["w0"].shape[1]
    gw = n_pairs * LANES
    n_col_blocks = c // gw
    nt = seq // tt
    low_w = LANES + prm["wg"].shape[0]

    def pspec(off):
        return pl.BlockSpec((tt, gw), lambda b, g, t: (b * nt + t, off * n_col_blocks + g))

    def vspec():
        return pl.BlockSpec((1, gw), lambda b, g, t: (0, g))

    def wspec(rows):
        return pl.BlockSpec((rows, gw), lambda b, g, t: (0, g))

    kern = functools.partial(_wkv_kernel, n_pairs=n_pairs, tt=tt)
    tile = pltpu.VMEM((tt, gw), F32)
    return pl.pallas_call(
        kern,
        grid=(batch, n_col_blocks, nt),
        in_specs=[
            pspec(0), pspec(1), pspec(2),
            pl.BlockSpec((tt, low_w), lambda b, g, t: (b * nt + t, (3 * c) // low_w)),
            vspec(), vspec(), vspec(), vspec(), vspec(), vspec(), vspec(),
            wspec(LANES), wspec(LANES), wspec(low_w - LANES),
        ],
        out_specs=pl.BlockSpec((tt, gw), lambda b, g, t: (b * nt + t, g)),
        out_shape=jax.ShapeDtypeStruct((m, c), BF16),
        scratch_shapes=[
            pltpu.VMEM((n_pairs, LANES, LANES), F32),
            tile, tile, tile, tile, tile, tile, tile, tile, tile,
            pltpu.VMEM((LANES, LANES), F32),
            pltpu.VMEM((4, n_pairs, 4, WKV_CHUNK, LANES), BF16),
            pltpu.VMEM((4, n_pairs, 3, 2 * WKV_CHUNK, LANES), BF16),
            pltpu.VMEM((4, n_pairs, 8, LANES), F32),
        ],
        compiler_params=_cparams(("parallel", "parallel", "arbitrary")),
        name="wkv7",
    )(p_rwkv, p_rwkv, p_rwkv, p_rwkv,
      prm["w0"], prm["a0"], prm["k_k"], prm["k_a"], prm["r_k"], prm["lnx_g"], prm["lnx_b"],
      prm["wd"], prm["wa"], prm["wg"])


def _sb_kernel(q_ref, k_ref, v_ref, o_ref, acc_ref, right_ref, *, tq, scale):
    i = pl.program_id(1)
    row = lax.broadcasted_iota(jnp.int32, (tq, tq), 0)
    col = lax.broadcasted_iota(jnp.int32, (tq, tq), 1)
    r2 = lax.broadcasted_iota(jnp.int32, (tq, 2 * tq), 0)
    c2 = lax.broadcasted_iota(jnp.int32, (tq, 2 * tq), 1)
    after_and_total = jnp.where((c2 >= tq) | (r2 > c2), 1.0, 0.0).astype(BF16)
    diag = col < row

    heads = range(SB_HEADS)
    hsl = [slice(h * LANES, (h + 1) * LANES) for h in heads]

    def blocks(js, first):
        nb = len(js)
        it = [(b, h) for b in range(nb) for h in heads]
        on_diag = lambda p: first and p[0] == 0
        ks = [pl.ds(pl.multiple_of(j * tq, tq), tq) for j in js]
        z = {p: _dot_nt(q_ref[:, hsl[p[1]]], k_ref[ks[p[0]], hsl[p[1]]]) * scale for p in it}
        sp = {p: jnp.maximum(z[p], 0.0) + jnp.log(1.0 + jnp.exp(-jnp.abs(z[p]))) for p in it}
        log_keep = {p: jnp.where(diag, -sp[p], 0.0) if on_diag(p) else -sp[p] for p in it}
        hi = {p: log_keep[p].astype(BF16) for p in it}
        lo = {p: (log_keep[p] - hi[p].astype(F32)).astype(BF16) for p in it}
        sums = {p: _dot(hi[p], after_and_total) + _dot(lo[p], after_and_total) for p in it}
        right = {}
        for h in heads:
            run = None if first else right_ref[h]
            for b in range(nb):
                right[(b, h)] = run
                tot = sums[(b, h)][:, tq:]
                run = tot if run is None else run + tot
            right[("end", h)] = run
        after = {p: sums[p][:, :tq] if right[p] is None else sums[p][:, :tq] + right[p] for p in it}
        attn = {p: jnp.exp(z[p] - sp[p] + after[p]) for p in it}
        attn = {p: jnp.where(diag, attn[p], 0.0) if on_diag(p) else attn[p] for p in it}
        pv = {p: _dot(attn[p].astype(BF16), v_ref[ks[p[0]], hsl[p[1]]]) for p in it}
        for h in heads:
            tot = pv[(0, h)]
            for b in range(1, nb):
                tot = tot + pv[(b, h)]
            if first:
                acc_ref[h] = tot
            else:
                acc_ref[h] += tot
            right_ref[h] = right[("end", h)]

    per = SB_BLOCKS_PER_ITER
    first_size = i % per + 1
    for size in range(1, per + 1):
        @pl.when(first_size == size)
        def _(size=size):
            blocks([i - b for b in range(size)], True)

    def body(jj, _):
        j = i - first_size - per * jj
        blocks([j - b for b in range(per)], False)
        return 0

    lax.fori_loop(0, (i + 1 - first_size) // per, body, 0)

    for h in heads:
        o_ref[:, hsl[h]] = acc_ref[h].astype(BF16)


def _sb_attn(p_attn, batch, seq, tq):
    m = p_attn.shape[0]
    w = SB_HEADS * LANES
    nq = seq // tq
    kern = functools.partial(_sb_kernel, tq=tq, scale=LANES ** -0.5)
    return pl.pallas_call(
        kern,
        grid=(batch, nq),
        in_specs=[
            pl.BlockSpec((tq, w), lambda b, i: (b * nq + i, 0)),
            pl.BlockSpec((seq, w), lambda b, i: (b, 1)),
            pl.BlockSpec((seq, w), lambda b, i: (b, 2)),
        ],
        out_specs=pl.BlockSpec((tq, w), lambda b, i: (b * nq + i, 0)),
        out_shape=jax.ShapeDtypeStruct((m, w), BF16),
        scratch_shapes=[pltpu.VMEM((SB_HEADS, tq, tq), F32), pltpu.VMEM((SB_HEADS, tq, tq), F32)],
        compiler_params=_cparams(("parallel", "arbitrary")),
        name="sb_attn",
    )(p_attn, p_attn, p_attn)


def _mem_kv_kernel(m_ref, w_ref, o_ref):
    o_ref[...] = _dot(m_ref[...].astype(BF16), w_ref[...]).astype(BF16)


def _mem_kv(mem2, w_bf16, tm):
    m, d = mem2.shape
    n = w_bf16.shape[1]
    return pl.pallas_call(
        _mem_kv_kernel,
        grid=(m // tm,),
        in_specs=[pl.BlockSpec((tm, d), lambda i: (i, 0)),
                  pl.BlockSpec((d, n), lambda i: (0, 0))],
        out_specs=pl.BlockSpec((tm, n), lambda i: (i, 0)),
        out_shape=jax.ShapeDtypeStruct((m, n), BF16),
        compiler_params=_cparams(("parallel",)),
        name="mem_kv",
    )(mem2, w_bf16)


def _mem_attn_kernel(q_ref, k_ref, v_ref, o_ref, *, scale):
    for h in range(MEM_HEADS):
        hs = slice(h * LANES, (h + 1) * LANES)
        s = _dot_nt(q_ref[:, hs], k_ref[:, hs]) * scale
        s = s - jnp.max(s, axis=-1, keepdims=True)
        e = jnp.exp(s)
        p = e / jnp.sum(e, axis=-1, keepdims=True)
        o_ref[:, hs] = _dot(p.astype(BF16), v_ref[:, hs]).astype(BF16)


def _mem_attn(p_attn, kv, batch, seq, mem_len, tq):
    m = p_attn.shape[0]
    w = MEM_HEADS * LANES
    nq = seq // tq
    kern = functools.partial(_mem_attn_kernel, scale=LANES ** -0.5)
    return pl.pallas_call(
        kern,
        grid=(batch, nq),
        in_specs=[
            pl.BlockSpec((tq, w), lambda b, i: (b * nq + i, 3)),
            pl.BlockSpec((mem_len, w), lambda b, i: (b, 0)),
            pl.BlockSpec((mem_len, w), lambda b, i: (b, 1)),
        ],
        out_specs=pl.BlockSpec((tq, w), lambda b, i: (b * nq + i, 0)),
        out_shape=jax.ShapeDtypeStruct((m, w), BF16),
        compiler_params=_cparams(("parallel", "parallel")),
        name="mem_attn",
    )(p_attn, kv, kv)


def _outproj_kernel(yr_ref, ys_ref, ym_ref, x_ref, g0_ref, b0_ref, w_ref, g_ref, b_ref, rw_ref, rb_ref,
                    h1_ref, lg_ref):
    c0 = yr_ref.shape[1]
    c1 = c0 + ys_ref.shape[1]
    tm = x_ref.shape[0]
    part = min(tm, 256)
    for r0 in range(0, tm, part):
        rs = slice(r0, r0 + part)
        mix = (_dot(yr_ref[rs, :], w_ref[0:c0, :]) + _dot(ys_ref[rs, :], w_ref[c0:c1, :])
               + _dot(ym_ref[rs, :], w_ref[c1:, :]))
        h = _layer_norm(x_ref[rs, :], g0_ref[...], b0_ref[...])
        h1 = _layer_norm(DEEPNORM_ALPHA * h + mix, g_ref[...], b_ref[...])
        h1_ref[rs, :] = h1
        hi = h1.astype(BF16)
        lo = (h1 - hi.astype(F32)).astype(BF16)
        both = _dot(hi, rw_ref[...])
        lg_ref[rs, :] = both[:, :LANES] + both[:, LANES:] + _dot(lo, rw_ref[:, :LANES]) + rb_ref[...]


def _outproj(y_r, y_s, y_m, x2, g0, b0, w, g, b, r_w, r_b, tm):
    m, d = x2.shape
    full = lambda a: pl.BlockSpec(a.shape, lambda i: (0, 0))
    rows = lambda a: pl.BlockSpec((tm, a.shape[1]), lambda i: (i, 0))
    return pl.pallas_call(
        _outproj_kernel,
        grid=(m // tm,),
        in_specs=[rows(y_r), rows(y_s), rows(y_m), rows(x2), full(g0), full(b0), full(w), full(g), full(b),
                  full(r_w), full(r_b)],
        out_specs=[pl.BlockSpec((tm, d), lambda i: (i, 0)),
                   pl.BlockSpec((tm, LANES), lambda i: (i, 0))],
        out_shape=[jax.ShapeDtypeStruct((m, d), F32),
                   jax.ShapeDtypeStruct((m, LANES), F32)],
        compiler_params=_cparams(("parallel",)),
        name="outproj",
    )(y_r, y_s, y_m, x2, g0, b0, w, g, b, r_w, r_b)


def _route_kernel(lg_ref, id_ref, wt_ref, *, n_groups, per_group):
    lg = lg_ref[...]
    lane_i = lax.broadcasted_iota(jnp.int32, lg.shape, 1)
    lane = lane_i.astype(F32)
    neg = jnp.float32(-jnp.inf)
    big = jnp.float32(2 ** 20)

    def first_max(vals):
        mx = jnp.max(vals, axis=-1, keepdims=True)
        idx = jnp.min(jnp.where(vals == mx, lane, big), axis=-1, keepdims=True)
        return mx, idx

    is_group = lane < n_groups
    gmax, gidx = first_max(jnp.where(is_group, lg, neg))
    gsum = jnp.sum(jnp.where(is_group, jnp.exp(lg - gmax), 0.0), axis=-1, keepdims=True)
    group_w = 1.0 / gsum
    lo = n_groups + gidx * per_group
    in_group = (lane >= lo) & (lane < lo + per_group)
    v1, i1 = first_max(jnp.where(in_group, lg, neg))
    v2, i2 = first_max(jnp.where(in_group & (lane != i1), lg, neg))
    e2 = jnp.exp(v2 - v1)
    w1 = group_w / (1.0 + e2)
    w2 = group_w * e2 / (1.0 + e2)
    ids = jnp.where(lane_i == 0, i1 - n_groups, jnp.where(lane_i == 1, i2 - n_groups, 0.0))
    id_ref[...] = ids.T[0:8, :].astype(jnp.int32)
    wt_ref[...] = jnp.where(lane_i == 0, w1, jnp.where(lane_i == 1, w2, 0.0))


def _route(logits, n_groups, per_group, tm):
    m = logits.shape[0]
    kern = functools.partial(_route_kernel, n_groups=n_groups, per_group=per_group)
    spec = pl.BlockSpec((tm, LANES), lambda i: (i, 0))
    return pl.pallas_call(
        kern,
        grid=(m // tm,),
        in_specs=[spec],
        out_specs=[pl.BlockSpec((8, tm), lambda i: (0, i)), spec],
        out_shape=[jax.ShapeDtypeStruct((8, m), jnp.int32),
                   jax.ShapeDtypeStruct((m, LANES), F32)],
        compiler_params=_cparams(("parallel",)),
        name="route",
    )(logits)


def _plan_kernel(id_ref, pos_ref, meta_ref, cnt_ref, base_ref, start_ref, *, tb, rows, n_experts, nb):
    phase = pl.program_id(0)
    j = pl.program_id(1)
    sub = lax.broadcasted_iota(jnp.int32, (LANES, tb), 0)
    e1 = id_ref[0:1, :]
    e2 = id_ref[1:2, :]
    hit1 = jnp.where(sub == e1, 1.0, 0.0)
    hit2 = jnp.where(sub == e2, 1.0, 0.0)
    hits = hit1 + hit2
    per_expert = jnp.sum(hits, axis=1, keepdims=True)

    @pl.when((phase == 0) & (j == 0))
    def _():
        cnt_ref[...] = jnp.zeros_like(cnt_ref)

    @pl.when(phase == 0)
    def _():
        cnt_ref[...] += per_expert

    sq_r = lax.broadcasted_iota(jnp.int32, (LANES, LANES), 0)
    sq_c = lax.broadcasted_iota(jnp.int32, (LANES, LANES), 1)

    @pl.when((phase == 1) & (j == 0))
    def _():
        n_blk = jnp.floor((cnt_ref[...] + (rows - 1)) * (1.0 / rows))
        before = jnp.where(sq_c < sq_r, 1.0, 0.0).astype(BF16)
        blk_start = _dot(before, jnp.broadcast_to(n_blk, (LANES, LANES)).astype(BF16))
        start_ref[...] = blk_start[:, 0:1]
        base_ref[...] = jnp.zeros_like(base_ref)
        blk_end = blk_start + n_blk
        n_used = jnp.sum(jnp.where(sq_r[:, 0:1] < n_experts, n_blk, 0.0), axis=0, keepdims=True)
        owner = jnp.sum(jnp.where((sq_r < n_experts) & (blk_end <= sq_c.astype(F32)), 1.0, 0.0),
                        axis=0, keepdims=True)
        last_owner = jnp.max(jnp.where(n_blk > 0.0, sq_r[:, 0:1].astype(F32), 0.0), axis=0, keepdims=True)
        blk = sq_c[0:1, :].astype(F32)
        in_use = blk < n_used
        owner = jnp.where(in_use, owner, last_owner)
        mine = sq_r.astype(F32) == owner
        cnt_o = jnp.sum(jnp.where(mine, cnt_ref[...], 0.0), axis=0, keepdims=True)
        start_o = jnp.sum(jnp.where(mine, blk_start, 0.0), axis=0, keepdims=True)
        valid = jnp.clip(cnt_o - (blk - start_o) * rows, 0.0, float(rows))
        valid = jnp.where(in_use, valid, 0.0)
        row8 = lax.broadcasted_iota(jnp.int32, (8, LANES), 0)
        meta = jnp.where(row8 == 0, owner, jnp.where(row8 == 1, valid, jnp.where(row8 == 2, n_used, 0.0)))
        meta_ref[...] = meta.astype(jnp.int32)

    @pl.when(phase == 1)
    def _():
        tr = lax.broadcasted_iota(jnp.int32, (tb, tb), 0)
        tc = lax.broadcasted_iota(jnp.int32, (tb, tb), 1)
        earlier = jnp.where(tr < tc, 1.0, 0.0).astype(BF16)
        seen = _dot(hits.astype(BF16), earlier)
        slot = start_ref[...] * rows + base_ref[...] + seen
        p1 = jnp.sum(hit1 * slot, axis=0, keepdims=True)
        p2 = jnp.sum(hit2 * slot, axis=0, keepdims=True)
        row8 = lax.broadcasted_iota(jnp.int32, (8, tb), 0)
        pos_ref[...] = jnp.where(row8 == 0, p1, jnp.where(row8 == 1, p2, 0.0)).astype(jnp.int32)
        base_ref[...] += per_expert


def _plan(ids_t, n_experts, rows, tb):
    m = ids_t.shape[1]
    nb = m // tb
    kern = functools.partial(_plan_kernel, tb=tb, rows=rows, n_experts=n_experts, nb=nb)
    col = pltpu.VMEM((LANES, 1), F32)
    return pl.pallas_call(
        kern,
        grid=(2, nb),
        in_specs=[pl.BlockSpec((8, tb), lambda p, j: (0, j))],
        out_specs=[pl.BlockSpec((8, tb), lambda p, j: (0, j * p)),
                   pl.BlockSpec((8, LANES), lambda p, j: (0, 0))],
        out_shape=[jax.ShapeDtypeStruct((8, m), jnp.int32),
                   jax.ShapeDtypeStruct((8, LANES), jnp.int32)],
        scratch_shapes=[col, col, col],
        compiler_params=_cparams(("arbitrary", "arbitrary")),
        name="moe_plan",
    )(ids_t)


def _for_range(lo, hi, body):
    full = (hi - lo) // ROW_GROUP
    first = lo // ROW_GROUP

    def group(g, _):
        for u in range(ROW_GROUP):
            body(first + g, u)
        return 0

    lax.fori_loop(0, full, group, 0)

    def one(r, _):
        body(r // ROW_GROUP, r % ROW_GROUP)
        return 0

    lax.fori_loop(lo + full * ROW_GROUP, hi, one, 0)


def _row_in(src_hbm, dst_vmem, sem, src_row, tile, sub):
    return pltpu.make_async_copy(src_hbm.at[pl.ds(src_row, 1)], dst_vmem.at[tile, pl.ds(sub, 1)], sem)


def _row_out(src_vmem, dst_hbm, sem, tile, sub, dst_row):
    return pltpu.make_async_copy(src_vmem.at[tile, pl.ds(sub, 1)], dst_hbm.at[pl.ds(dst_row, 1)], sem)


def _gather_cparams(sem):
    return pltpu.CompilerParams(dimension_semantics=sem, vmem_limit_bytes=VMEM_LIMIT,
                                disable_bounds_checks=True)


def _ffn_kernel(asg_ref, first_ref, be_ref, nv_ref, nu_ref, h_hbm, wg_ref, wu_ref, wd_ref, y_hbm,
                xf_ref, xb_ref, acc_ref, sem_in, sem_out, *, rows, sub, nj, n_tok, n_blocks):
    i = pl.program_id(0)
    j = pl.program_id(1)
    n_used = nu_ref[0]
    used = i < n_used
    slot = i % 2

    def gather(block, lo, hi, buf):
        base = first_ref[block]

        def body(tile, sub_row):
            a = asg_ref[base + tile * ROW_GROUP + sub_row]
            tok = jnp.where(a >= n_tok, a - n_tok, a)
            _row_in(h_hbm, xf_ref.at[buf], sem_in.at[buf], tok, tile, sub_row).start()

        _for_range(lo, hi, body)

    def gather_wait(block, buf):
        _for_range(0, nv_ref[block],
                   lambda tile, sub_row: _row_in(h_hbm, xf_ref.at[buf], sem_in.at[buf], 0, tile, sub_row).wait())

    def emit(block, buf):
        base = first_ref[block]

        def body(tile, sub_row):
            a = asg_ref[base + tile * ROW_GROUP + sub_row]
            _row_out(acc_ref.at[buf], y_hbm, sem_out.at[buf], tile, sub_row, a).start()

        _for_range(0, nv_ref[block], body)

    def emit_wait(block, buf):
        _for_range(0, nv_ref[block],
                   lambda tile, sub_row: _row_out(acc_ref.at[buf], y_hbm, sem_out.at[buf], tile, sub_row, 0).wait())

    def as_rows(x):
        return x.reshape(x.shape[0] * ROW_GROUP, x.shape[2])

    def as_tiles(x):
        return x.reshape(x.shape[0] // ROW_GROUP, ROW_GROUP, x.shape[1])

    @pl.when((i == 0) & (j == 0))
    def _():
        xf_ref[...] = jnp.zeros_like(xf_ref)
        gather(0, 0, nv_ref[0], 0)

    @pl.when(used & (j == 0))
    def _():
        gather_wait(i, slot)

    @pl.when(i + 1 < n_used)
    def _():
        nxt = nv_ref[i + 1]
        q = rows // nj
        gather(i + 1, jnp.minimum(j * q, nxt), jnp.minimum((j + 1) * q, nxt), 1 - slot)

    n_sub = (nv_ref[i] + sub - 1) // sub
    for n in range(1, rows // sub + 1):
        @pl.when(used & (n_sub == n))
        def _(n=n):
            r = n * sub
            nt = r // ROW_GROUP

            @pl.when(j == 0)
            def _():
                xb_ref[0:r, :] = as_rows(xf_ref[slot, 0:nt]).astype(BF16)
                acc_ref[slot, 0:nt] = jnp.zeros((nt,) + acc_ref.shape[2:], F32)

            xb = xb_ref[0:r, :]
            gate = _dot(xb, wg_ref[...].astype(BF16))
            up = _dot(xb, wu_ref[...].astype(BF16))
            hid = (gate * jax.nn.sigmoid(gate)) * up
            acc_ref[slot, 0:nt] += as_tiles(_dot(hid.astype(BF16), wd_ref[...].astype(BF16)))

    @pl.when((j == nj - 1) & (i >= 1) & (i <= n_used))
    def _():
        emit_wait(i - 1, 1 - slot)

    @pl.when((j == nj - 1) & used)
    def _():
        emit(i, slot)

    @pl.when((j == nj - 1) & used & (i == n_blocks - 1))
    def _():
        emit_wait(i, slot)


def _moe_ffn(asg_order, block_first, block_expert, n_valid, n_used, h1, w_gate, w_up, w_down, n_blocks, rows,
             tf):
    n_tok, d = h1.shape
    de = w_gate.shape[2]
    nj = de // tf
    kern = functools.partial(_ffn_kernel, rows=rows, sub=MOE_SUB, nj=nj, n_tok=n_tok, n_blocks=n_blocks)

    def jidx(i, j, nu):
        return jnp.where(i < nu[0], j, nj - 1)

    return pl.pallas_call(
        kern,
        grid_spec=pltpu.PrefetchScalarGridSpec(
            num_scalar_prefetch=5,
            grid=(n_blocks, nj),
            in_specs=[
                pl.BlockSpec(memory_space=pl.ANY),
                pl.BlockSpec((None, d, tf), lambda i, j, asg, fi, be, nv, nu: (be[i], 0, jidx(i, j, nu))),
                pl.BlockSpec((None, d, tf), lambda i, j, asg, fi, be, nv, nu: (be[i], 0, jidx(i, j, nu))),
                pl.BlockSpec((None, tf, d), lambda i, j, asg, fi, be, nv, nu: (be[i], jidx(i, j, nu), 0)),
            ],
            out_specs=pl.BlockSpec(memory_space=pl.ANY),
            scratch_shapes=[pltpu.VMEM((2, rows // ROW_GROUP, ROW_GROUP, d), F32),
                            pltpu.VMEM((rows, d), BF16),
                            pltpu.VMEM((2, rows // ROW_GROUP, ROW_GROUP, d), F32),
                            pltpu.SemaphoreType.DMA((2,)),
                            pltpu.SemaphoreType.DMA((2,))],
        ),
        out_shape=jax.ShapeDtypeStruct((2 * n_tok, d), F32),
        compiler_params=_gather_cparams(("arbitrary", "arbitrary")),
        name="moe_ffn",
    )(asg_order, block_first, block_expert, n_valid, n_used, h1, w_gate, w_up, w_down)


def _combine_kernel(h_ref, y0_ref, y1_ref, wt_ref, g_ref, b_ref, o_ref):
    wt = wt_ref[...]
    ffn = y0_ref[...] * wt[:, 0:1] + y1_ref[...] * wt[:, 1:2]
    o_ref[...] = _layer_norm(DEEPNORM_ALPHA * h_ref[...] + ffn, g_ref[...], b_ref[...])


def _combine(h1, ys, wts, g, b, tm):
    m, d = h1.shape
    nt = m // tm
    rows = pl.BlockSpec((tm, d), lambda i: (i, 0))
    vec = pl.BlockSpec((1, d), lambda i: (0, 0))
    return pl.pallas_call(
        _combine_kernel,
        grid=(nt,),
        in_specs=[rows, rows, pl.BlockSpec((tm, d), lambda i: (nt + i, 0)),
                  pl.BlockSpec((tm, LANES), lambda i: (i, 0)), vec, vec],
        out_specs=rows,
        out_shape=jax.ShapeDtypeStruct((m, d), F32),
        compiler_params=_cparams(("parallel",)),
        name="moe_combine",
    )(h1, ys, ys, wts, g, b)


def _dispatch_plan(ids_t, n_experts, rows):
    m = ids_t.shape[1]
    n_blocks = -(-(2 * m) // rows) + n_experts
    assert n_blocks <= LANES
    pos_t, meta = _plan(ids_t, n_experts, rows, _pick(m, 512))
    pos = pos_t[:2].reshape(-1)
    asg_order = jnp.argsort(pos).astype(jnp.int32)
    n_valid = meta[1, :n_blocks]
    block_first = (jnp.cumsum(n_valid) - n_valid).astype(jnp.int32)
    return asg_order, block_first, meta[2, :1], meta[0, :n_blocks], n_valid, n_blocks


def _pick(n, pref):
    t = min(pref, n)
    while n % t:
        t //= 2
    return t


def kernel(x, mem, ln_in_g, ln_in_b, w_in, tshift_mu, w0, w_decay_up, a0, w_a_up, w_g_up, k_k, k_a, r_k,
           lnx_g, lnx_b, w_mem_kv, w_out, ln1_g, ln1_b, router_group, router_group_b, router_expert,
           router_expert_b, w_e_gate, w_e_up, w_e_down, ln2_g, ln2_b):
    batch, seq, d = x.shape
    mem_len = mem.shape[1]
    m = batch * seq
    c = w0.shape[1]
    dr, ar, gr = w_decay_up.shape[1], w_a_up.shape[1], w_g_up.shape[1]
    rwkv_cols = 3 * c + dr + ar + gr
    sb_w = SB_HEADS * LANES
    mem_w = MEM_HEADS * LANES
    assert dr + ar == LANES and c % (2 * LANES) == 0 and w_in.shape[0] == DEPTH
    assert w_in.shape[2] == rwkv_cols + 3 * sb_w + mem_w
    n_experts = router_expert.shape[2]
    row = lambda a: a.reshape(1, -1)

    tn = 512
    low_w = -(-(dr + ar + gr) // LANES) * LANES
    rw_pad = -(-(3 * c + low_w) // tn) * tn
    wi = w_in[0]
    w_rwkv = jnp.pad(wi[:, :rwkv_cols].astype(BF16), ((0, 0), (0, rw_pad - rwkv_cols)))
    w_attn = wi[:, rwkv_cols:].astype(BF16)
    mu = row(jnp.pad(tshift_mu[0], (0, rw_pad - rwkv_cols)))
    lp = low_w - (dr + ar + gr)
    prm = dict(
        w0=row(w0[0]), a0=row(a0[0]), k_k=row(k_k[0]), k_a=row(k_a[0]), r_k=row(r_k[0]),
        lnx_g=row(lnx_g[0]), lnx_b=row(lnx_b[0]),
        wd=jnp.concatenate([w_decay_up[0], jnp.zeros((ar, c), F32)], axis=0).astype(BF16),
        wa=jnp.concatenate([jnp.zeros((dr, c), F32), w_a_up[0]], axis=0).astype(BF16),
        wg=jnp.concatenate([w_g_up[0], jnp.zeros((lp, c), F32)], axis=0).astype(BF16),
    )
    assert (3 * c) % low_w == 0

    x2 = x.reshape(m, d)
    hb = _ln_in(x2, row(ln_in_g), row(ln_in_b), _pick(m, 512))
    p_rwkv, p_attn = _inproj(hb, w_rwkv, w_attn, mu, seq, _pick(seq, 2048), tn)

    y_rwkv = _wkv(p_rwkv, prm, batch, seq, n_pairs=4, tt=_pick(seq, 1024))
    y_sb = _sb_attn(p_attn, batch, seq, tq=LANES)
    kv = _mem_kv(mem.reshape(batch * mem_len, d), w_mem_kv[0].astype(BF16), _pick(batch * mem_len, 256))
    y_mem = _mem_attn(p_attn, kv, batch, seq, mem_len, tq=_pick(seq, 512))

    wo = w_out[0].astype(BF16)
    r_w = jnp.concatenate([router_group[0], router_expert[0],
                           jnp.zeros((d, LANES - N_GROUPS - n_experts), F32)], axis=1)
    r_b = jnp.concatenate([router_group_b[0], router_expert_b[0],
                           jnp.zeros((LANES - N_GROUPS - n_experts,), F32)]).reshape(1, LANES)
    r_hi = r_w.astype(BF16)
    r_w2 = jnp.concatenate([r_hi, (r_w - r_hi.astype(F32)).astype(BF16)], axis=1)
    h1, logits = _outproj(y_rwkv, y_sb, y_mem, x2, row(ln_in_g), row(ln_in_b), wo,
                          row(ln1_g[0]), row(ln1_b[0]), r_w2, r_b, _pick(m, 512))

    ids_t, wts = _route(logits, N_GROUPS, n_experts // N_GROUPS, _pick(m, 512))
    asg_order, block_first, n_used, block_expert, n_valid, n_blocks = _dispatch_plan(ids_t, n_experts, MOE_ROWS)
    ys = _moe_ffn(asg_order, block_first, block_expert, n_valid, n_used, h1, w_e_gate[0], w_e_up[0], w_e_down[0],
                  n_blocks, MOE_ROWS, tf=512)
    out = _combine(h1, ys, wts, row(ln2_g[0]), row(ln2_b[0]), _pick(m, 256))
    return out.reshape(batch, seq, d)
```

```python
import functools

import jax
import jax.numpy as jnp
from jax import lax
from jax.experimental import pallas as pl
from jax.experimental.pallas import tpu as pltpu

F32 = jnp.float32
BF16 = jnp.bfloat16

SB_HEADS = 4
MEM_HEADS = 4
N_GROUPS = 8
DEPTH = 1
DEEPNORM_ALPHA = (2.0 * DEPTH) ** 0.25
LN_EPS = 1e-5
GN_EPS = 64e-5

LANES = 128
WKV_CHUNK = 64
MOE_ROWS = 512
MOE_SUB = 64
ROW_GROUP = 8
SB_BLOCKS_PER_ITER = 4
VMEM_LIMIT = 56 * 1024 * 1024


def _cparams(sem):
    return pltpu.CompilerParams(dimension_semantics=sem, vmem_limit_bytes=VMEM_LIMIT)


def _layer_norm(x, g, b):
    mu = jnp.mean(x, axis=-1, keepdims=True)
    xc = x - mu
    var = jnp.mean(xc * xc, axis=-1, keepdims=True)
    return xc * lax.rsqrt(var + LN_EPS) * g + b


def _split2(x):
    hi = x.astype(BF16)
    return hi, (x - hi.astype(F32)).astype(BF16)


def _dot(a, b):
    return jnp.dot(a, b, preferred_element_type=F32)


def _dot_nt(a, b):
    return lax.dot_general(a, b, (((1,), (1,)), ((), ())), preferred_element_type=F32)


def _dot_tn(a, b):
    return lax.dot_general(a, b, (((0,), (0,)), ((), ())), preferred_element_type=F32)


def _dot_exact_by_f32(m, x):
    hi, lo = _split2(x)
    return _dot(m, hi) + _dot(m, lo)


def _ln_kernel(x_ref, g_ref, b_ref, hb_ref):
    hb_ref[...] = _layer_norm(x_ref[...], g_ref[...], b_ref[...]).astype(BF16)


def _ln_in(x2, g, b, tm):
    m, d = x2.shape
    rows = pl.BlockSpec((tm, d), lambda i: (i, 0))
    vec = pl.BlockSpec((1, d), lambda i: (0, 0))
    return pl.pallas_call(
        _ln_kernel,
        grid=(m // tm,),
        in_specs=[rows, vec, vec],
        out_specs=rows,
        out_shape=jax.ShapeDtypeStruct((m, d), BF16),
        compiler_params=_cparams(("parallel",)),
        name="ln_in",
    )(x2, g, b)


def _inproj_kernel(hb_ref, wr_ref, wa_ref, mu_ref, pr_ref, pa_ref, last_ref, *, n_f32_tiles, tiles_per_seq):
    i = pl.program_id(0)
    n = pl.program_id(1)

    @pl.when((i == 0) & (n == 0))
    def _():
        last_ref[...] = jnp.zeros_like(last_ref)

    @pl.when(n < n_f32_tiles)
    def _():
        tm = pr_ref.shape[0]
        p = _dot(hb_ref[...], wr_ref[...])
        slot = jnp.minimum(n, n_f32_tiles - 1)
        carried = jnp.where(i % tiles_per_seq == 0, 0.0, last_ref[slot])
        prev = pltpu.roll(p, shift=1, axis=0)
        prev = jnp.where(lax.broadcasted_iota(jnp.int32, (tm, 1), 0) == 0, carried, prev)
        last_ref[slot] = p[tm - 1:tm, :]
        pr_ref[...] = p + (prev - p) * mu_ref[...]

    @pl.when(n >= n_f32_tiles)
    def _():
        pa_ref[...] = _dot(hb_ref[...], wa_ref[...]).astype(BF16)


def _inproj(hb, w_rwkv, w_attn, mu, seq, tm, tn):
    m, d = hb.shape
    n_rwkv_cols, n_attn_cols = w_rwkv.shape[1], w_attn.shape[1]
    nf = n_rwkv_cols // tn
    kern = functools.partial(_inproj_kernel, n_f32_tiles=nf, tiles_per_seq=seq // tm)
    first = lambda i, n: (0, jnp.minimum(n, nf - 1))
    second = lambda i, n: (0, jnp.maximum(n - nf, 0))
    return pl.pallas_call(
        kern,
        grid=(m // tm, (n_rwkv_cols + n_attn_cols) // tn),
        in_specs=[
            pl.BlockSpec((tm, d), lambda i, n: (i, 0)),
            pl.BlockSpec((d, tn), first),
            pl.BlockSpec((d, tn), second),
            pl.BlockSpec((1, tn), first),
        ],
        out_specs=[
            pl.BlockSpec((tm, tn), lambda i, n: (i, jnp.minimum(n, nf - 1))),
            pl.BlockSpec((tm, tn), lambda i, n: (i, jnp.maximum(n - nf, 0))),
        ],
        out_shape=[
            jax.ShapeDtypeStruct((m, n_rwkv_cols), F32),
            jax.ShapeDtypeStruct((m, n_attn_cols), BF16),
        ],
        scratch_shapes=[pltpu.VMEM((nf, 1, tn), F32)],
        compiler_params=_cparams(("arbitrary", "arbitrary")),
        name="inproj",
    )(hb, w_rwkv, w_attn, mu)


def _wkv_kernel(pr_ref, pk_ref, pv_ref, pl_ref,
                w0_ref, a0_ref, kk_ref, ka_ref, rk_ref, lg_ref, lb_ref,
                wd_ref, wa_ref, wg_ref,
                y_ref,
                s_ref,
                r_s, lw_s, k_s, v_s, a_s, b_s, y_s, g_s, bo_s, zero_ref, ring64, ring128, ringg,
                *, n_pairs, tt):
    C = WKV_CHUNK
    t_idx = pl.program_id(2)

    @pl.when(t_idx == 0)
    def _():
        s_ref[...] = jnp.zeros_like(s_ref)

    lane = lax.broadcasted_iota(jnp.int32, (LANES, LANES), 1)
    sub = lax.broadcasted_iota(jnp.int32, (LANES, LANES), 0)
    head_ones = jnp.where((lane // 64) == (sub // 64), 1.0, 0.0).astype(BF16)

    def head_sum(x):
        hi = x.astype(BF16)
        lo = (x - hi.astype(F32)).astype(BF16)
        return _dot(hi, head_ones) + _dot(lo, head_ones)

    da = pl_ref[:, 0:LANES]
    th = jnp.tanh(da).astype(BF16)
    sg = jax.nn.sigmoid(pl_ref[:, LANES:]).astype(BF16)
    da = da.astype(BF16)
    for g in range(n_pairs):
        cs = slice(g * LANES, (g + 1) * LANES)
        rg, kg, vg = pr_ref[:, cs], pk_ref[:, cs], pv_ref[:, cs]
        pre = w0_ref[:, cs] + _dot(th, wd_ref[:, cs])
        softplus_neg = jnp.maximum(-pre, 0.0) + jnp.log(1.0 + jnp.exp(-jnp.abs(pre)))
        w_log = -softplus_neg - 0.5
        lw = -jnp.exp(w_log)
        a = jax.nn.sigmoid(a0_ref[:, cs] + _dot(da, wa_ref[:, cs]))
        gate = _dot(sg, wg_ref[:, cs])
        kk = kg * kk_ref[:, cs]
        kk = kk * lax.rsqrt(jnp.maximum(head_sum(kk * kk), 1e-24))
        k2 = kg * (1.0 + (a - 1.0) * ka_ref[:, cs])
        bonus = head_sum(rg * k2 * rk_ref[:, cs]) * vg
        r_s[:, cs] = rg
        lw_s[:, cs] = lw
        k_s[:, cs] = k2
        v_s[:, cs] = vg
        a_s[:, cs] = -kk
        b_s[:, cs] = kk * a
        g_s[:, cs] = gate
        bo_s[:, cs] = bonus

    ci = lax.broadcasted_iota(jnp.int32, (C, 2 * C), 0)
    cj = lax.broadcasted_iota(jnp.int32, (C, 2 * C), 1)
    left = cj < C
    strict = (cj % C) < ci
    incl = (cj % C) <= ci
    tri_incl = jnp.where(lax.broadcasted_iota(jnp.int32, (C, C), 1)
                         <= lax.broadcasted_iota(jnp.int32, (C, C), 0), 1.0, 0.0).astype(BF16)
    lane_c = lax.broadcasted_iota(jnp.int32, (C, LANES), 1)
    m0 = lane_c < 64
    eye = jnp.where(lane == sub, 1.0, 0.0).astype(F32)
    blockdiag = (lane // 64) == (sub // 64)

    csl = [slice(g * LANES, (g + 1) * LANES) for g in range(n_pairs)]
    P = range(n_pairs)
    cat0 = lambda *xs: jnp.concatenate(xs, axis=0)
    cat1 = lambda *xs: jnp.concatenate(xs, axis=1)
    bf = lambda x: x.astype(BF16)

    n_chunks = tt // C
    MM0, MM1, RTB, VB = range(4)
    BK, AK, ATB = range(3)
    ring64[...] = jnp.zeros_like(ring64)
    ring128[...] = jnp.zeros_like(ring128)
    ringg[...] = jnp.zeros_like(ringg)

    def stage1(c, out):
        rows = pl.ds(pl.multiple_of(c * C, C), C)
        slot = c % 4
        ld = lambda ref: [ref[rows, csl[g]] for g in P]
        rc, lwc, kc, vc, ac, bc = ld(r_s), ld(lw_s), ld(k_s), ld(v_s), ld(a_s), ld(b_s)
        cum = [_dot_exact_by_f32(tri_incl, lwc[g]) for g in P]
        yield
        last = [cum[g][C - 1:C, :] for g in P]
        rt = [rc[g] * jnp.exp(cum[g]) for g in P]
        at = [ac[g] * jnp.exp(cum[g] - lwc[g]) for g in P]
        ginv = [jnp.exp(-cum[g]) for g in P]
        btb = [bf(bc[g] * ginv[g]) for g in P]
        ktb = [bf(kc[g] * ginv[g]) for g in P]
        ghat = [jnp.exp(last[g] - cum[g]) for g in P]
        lhs0 = [bf(cat0(jnp.where(m0, at[g], 0.0), jnp.where(m0, rt[g], 0.0))) for g in P]
        lhs1 = [bf(cat0(jnp.where(m0, 0.0, at[g]), jnp.where(m0, 0.0, rt[g]))) for g in P]
        for g in P:
            ring128[slot, g, BK] = cat0(bf(bc[g] * ghat[g]), bf(kc[g] * ghat[g]))
            ring128[slot, g, ATB] = cat0(lhs0[g][:C], lhs1[g][:C])
            ring64[slot, g, VB] = bf(vc[g])
            ring64[slot, g, RTB] = bf(rt[g])
            ringg[slot, g, 0:1, :] = jnp.exp(last[g])
        x0 = [_dot_nt(lhs0[g], cat0(btb[g], ktb[g])) for g in P]
        x1 = [_dot_nt(lhs1[g], cat0(ktb[g], btb[g])) for g in P]
        yield
        n_bd = [cat0(jnp.where(left & strict, x0[g][:C], 0.0),
                     jnp.where((~left) & strict, x1[g][:C], 0.0)) for g in P]
        for g in P:
            ring128[slot, g, AK] = bf(cat0(jnp.where((~left) & strict, x0[g][:C], 0.0),
                                           jnp.where(left & strict, x1[g][:C], 0.0)))
            ring64[slot, g, MM0] = bf(jnp.where(incl, x0[g][C:], 0.0))
            ring64[slot, g, MM1] = bf(jnp.where(incl, x1[g][C:], 0.0))
        out["t"] = [eye + n_bd[g] for g in P]
        nb = [bf(n_bd[g]) for g in P]
        out["pw"] = [bf(_dot(nb[g], nb[g])) for g in P]

    def inverse_level(t, pw):
        res = [_dot(pw[g], cat1(pw[g], bf(t[g]))) for g in P]
        return [t[g] + res[g][:, LANES:] for g in P], [bf(res[g][:, :LANES]) for g in P]

    def stage2(inp, out):
        t, pw = inp["t"], inp["pw"]
        for level in range(3):
            t, pw = inverse_level(t, pw)
            if level < 2:
                yield
        out["t"], out["pw"] = t, pw

    def stage3(inp, c, out):
        slot = c % 4
        t, pw = inverse_level(inp["t"], inp["pw"])
        vb = [ring64[slot, g, VB] for g in P]
        av = [_dot(ring128[slot, g, AK], cat0(vb[g], vb[g])) for g in P]
        yield
        t = [t[g] + _dot(pw[g], bf(t[g])) for g in P]
        av = [cat0(jnp.where(m0, av[g][:C], 0.0), jnp.where(m0, 0.0, av[g][C:])) for g in P]
        yield
        wu = [_dot(bf(t[g]), cat1(ring128[slot, g, ATB], bf(av[g]))) for g in P]
        out["w"] = [bf(wu[g][:C, :LANES] + wu[g][C:, :LANES]) for g in P]
        out["u0"] = [wu[g][:C, LANES:] + wu[g][C:, LANES:] for g in P]

    def stage4(inp, c, valid):
        rows = pl.ds(pl.multiple_of(c * C, C), C)
        slot = c % 4
        vb = [ring64[slot, g, VB] for g in P]
        s = [s_ref[g] for g in P]
        sb = [bf(s[g]) for g in P]
        ub = [bf(_dot_nt(inp["w"][g], sb[g]) + inp["u0"][g]) for g in P]
        ys = [_dot_nt(ring64[slot, g, RTB], sb[g]) for g in P]
        yield
        uv = [cat0(ub[g], vb[g]) for g in P]
        y = [ys[g] + jnp.where(m0, _dot(ring64[slot, g, MM0], uv[g]),
                               _dot(ring64[slot, g, MM1], cat0(vb[g], ub[g]))) for g in P]
        upd = [_dot_tn(uv[g], ring128[slot, g, BK]) for g in P]
        yield
        for g in P:
            s_new = s[g] * ringg[slot, g, 0:1, :] + jnp.where(blockdiag, upd[g], 0.0)
            s_ref[g] = jnp.where(valid, s_new, s[g])
            y_s[rows, csl[g]] = y[g]

    def interleave(gens):
        live = list(gens)
        while live:
            still = []
            for gen in live:
                try:
                    next(gen)
                    still.append(gen)
                except StopIteration:
                    pass
            live = still

    def body(it, carry):
        o1, o2, o3 = carry
        n1, n2, n3 = {}, {}, {}
        interleave([stage4(o3, jnp.maximum(it - 3, 0), it >= 3),
                    stage3(o2, jnp.maximum(it - 2, 0), n3),
                    stage2(o1, n2),
                    stage1(jnp.minimum(it, n_chunks - 1), n1)])
        return n1, n2, n3

    zero_ref[...] = jnp.zeros_like(zero_ref)
    zb = lambda r: [zero_ref[0:r, :].astype(BF16) for _ in P]
    zf = lambda r: [zero_ref[0:r, :] for _ in P]
    front = lambda: dict(t=zf(2 * C), pw=zb(2 * C))
    lax.fori_loop(0, n_chunks + 3, body, (front(), front(), dict(w=zb(C), u0=zf(C))))

    for g in range(n_pairs):
        cs = slice(g * LANES, (g + 1) * LANES)
        y = y_s[:, cs]
        mean = head_sum(y) * (1.0 / 64.0)
        yc = y - mean
        var = head_sum(yc * yc) * (1.0 / 64.0)
        yn = yc * lax.rsqrt(var + GN_EPS) * lg_ref[:, cs] + lb_ref[:, cs]
        y_ref[:, cs] = ((yn + bo_s[:, cs]) * g_s[:, cs]).astype(BF16)


def _wkv(p_rwkv, prm, batch, seq, n_pairs, tt):
    m = p_rwkv.shape[0]
    c = prm["w0"].shape[1]
    gw = n_pairs * LANES
    n_col_blocks = c // gw
    nt = seq // tt
    low_w = LANES + prm["wg"].shape[0]

    def pspec(off):
        return pl.BlockSpec((tt, gw), lambda b, g, t: (b * nt + t, off * n_col_blocks + g))

    def vspec():
        return pl.BlockSpec((1, gw), lambda b, g, t: (0, g))

    def wspec(rows):
        return pl.BlockSpec((rows, gw), lambda b, g, t: (0, g))

    kern = functools.partial(_wkv_kernel, n_pairs=n_pairs, tt=tt)
    tile = pltpu.VMEM((tt, gw), F32)
    return pl.pallas_call(
        kern,
        grid=(batch, n_col_blocks, nt),
        in_specs=[
            pspec(0), pspec(1), pspec(2),
            pl.BlockSpec((tt, low_w), lambda b, g, t: (b * nt + t, (3 * c) // low_w)),
            vspec(), vspec(), vspec(), vspec(), vspec(), vspec(), vspec(),
            wspec(LANES), wspec(LANES), wspec(low_w - LANES),
        ],
        out_specs=pl.BlockSpec((tt, gw), lambda b, g, t: (b * nt + t, g)),
        out_shape=jax.ShapeDtypeStruct((m, c), BF16),
        scratch_shapes=[
            pltpu.VMEM((n_pairs, LANES, LANES), F32),
            tile, tile, tile, tile, tile, tile, tile, tile, tile,
            pltpu.VMEM((LANES, LANES), F32),
            pltpu.VMEM((4, n_pairs, 4, WKV_CHUNK, LANES), BF16),
            pltpu.VMEM((4, n_pairs, 3, 2 * WKV_CHUNK, LANES), BF16),
            pltpu.VMEM((4, n_pairs, 8, LANES), F32),
        ],
        compiler_params=_cparams(("parallel", "parallel", "arbitrary")),
        name="wkv7",
    )(p_rwkv, p_rwkv, p_rwkv, p_rwkv,
      prm["w0"], prm["a0"], prm["k_k"], prm["k_a"], prm["r_k"], prm["lnx_g"], prm["lnx_b"],
      prm["wd"], prm["wa"], prm["wg"])


def _sb_kernel(q_ref, k_ref, v_ref, o_ref, acc_ref, right_ref, *, tq, scale):
    i = pl.program_id(1)
    row = lax.broadcasted_iota(jnp.int32, (tq, tq), 0)
    col = lax.broadcasted_iota(jnp.int32, (tq, tq), 1)
    r2 = lax.broadcasted_iota(jnp.int32, (tq, 2 * tq), 0)
    c2 = lax.broadcasted_iota(jnp.int32, (tq, 2 * tq), 1)
    after_and_total = jnp.where((c2 >= tq) | (r2 > c2), 1.0, 0.0).astype(BF16)
    diag = col < row

    heads = range(SB_HEADS)
    hsl = [slice(h * LANES, (h + 1) * LANES) for h in heads]

    def blocks(js, first):
        nb = len(js)
        it = [(b, h) for b in range(nb) for h in heads]
        on_diag = lambda p: first and p[0] == 0
        ks = [pl.ds(pl.multiple_of(j * tq, tq), tq) for j in js]
        z = {p: _dot_nt(q_ref[:, hsl[p[1]]], k_ref[ks[p[0]], hsl[p[1]]]) * scale for p in it}
        sp = {p: jnp.maximum(z[p], 0.0) + jnp.log(1.0 + jnp.exp(-jnp.abs(z[p]))) for p in it}
        log_keep = {p: jnp.where(diag, -sp[p], 0.0) if on_diag(p) else -sp[p] for p in it}
        hi = {p: log_keep[p].astype(BF16) for p in it}
        lo = {p: (log_keep[p] - hi[p].astype(F32)).astype(BF16) for p in it}
        sums = {p: _dot(hi[p], after_and_total) + _dot(lo[p], after_and_total) for p in it}
        right = {}
        for h in heads:
            run = None if first else right_ref[h]
            for b in range(nb):
                right[(b, h)] = run
                tot = sums[(b, h)][:, tq:]
                run = tot if run is None else run + tot
            right[("end", h)] = run
        after = {p: sums[p][:, :tq] if right[p] is None else sums[p][:, :tq] + right[p] for p in it}
        attn = {p: jnp.exp(z[p] - sp[p] + after[p]) for p in it}
        attn = {p: jnp.where(diag, attn[p], 0.0) if on_diag(p) else attn[p] for p in it}
        pv = {p: _dot(attn[p].astype(BF16), v_ref[ks[p[0]], hsl[p[1]]]) for p in it}
        for h in heads:
            tot = pv[(0, h)]
            for b in range(1, nb):
                tot = tot + pv[(b, h)]
            if first:
                acc_ref[h] = tot
            else:
                acc_ref[h] += tot
            right_ref[h] = right[("end", h)]

    per = SB_BLOCKS_PER_ITER
    first_size = i % per + 1
    for size in range(1, per + 1):
        @pl.when(first_size == size)
        def _(size=size):
            blocks([i - b for b in range(size)], True)

    def body(jj, _):
        j = i - first_size - per * jj
        blocks([j - b for b in range(per)], False)
        return 0

    lax.fori_loop(0, (i + 1 - first_size) // per, body, 0)

    for h in heads:
        o_ref[:, hsl[h]] = acc_ref[h].astype(BF16)


def _sb_attn(p_attn, batch, seq, tq):
    m = p_attn.shape[0]
    w = SB_HEADS * LANES
    nq = seq // tq
    kern = functools.partial(_sb_kernel, tq=tq, scale=LANES ** -0.5)
    return pl.pallas_call(
        kern,
        grid=(batch, nq),
        in_specs=[
            pl.BlockSpec((tq, w), lambda b, i: (b * nq + i, 0)),
            pl.BlockSpec((seq, w), lambda b, i: (b, 1)),
            pl.BlockSpec((seq, w), lambda b, i: (b, 2)),
        ],
        out_specs=pl.BlockSpec((tq, w), lambda b, i: (b * nq + i, 0)),
        out_shape=jax.ShapeDtypeStruct((m, w), BF16),
        scratch_shapes=[pltpu.VMEM((SB_HEADS, tq, tq), F32), pltpu.VMEM((SB_HEADS, tq, tq), F32)],
        compiler_params=_cparams(("parallel", "arbitrary")),
        name="sb_attn",
    )(p_attn, p_attn, p_attn)


def _mem_kv_kernel(m_ref, w_ref, o_ref):
    o_ref[...] = _dot(m_ref[...].astype(BF16), w_ref[...]).astype(BF16)


def _mem_kv(mem2, w_bf16, tm):
    m, d = mem2.shape
    n = w_bf16.shape[1]
    return pl.pallas_call(
        _mem_kv_kernel,
        grid=(m // tm,),
        in_specs=[pl.BlockSpec((tm, d), lambda i: (i, 0)),
                  pl.BlockSpec((d, n), lambda i: (0, 0))],
        out_specs=pl.BlockSpec((tm, n), lambda i: (i, 0)),
        out_shape=jax.ShapeDtypeStruct((m, n), BF16),
        compiler_params=_cparams(("parallel",)),
        name="mem_kv",
    )(mem2, w_bf16)


def _mem_attn_kernel(q_ref, k_ref, v_ref, o_ref, *, scale):
    for h in range(MEM_HEADS):
        hs = slice(h * LANES, (h + 1) * LANES)
        s = _dot_nt(q_ref[:, hs], k_ref[:, hs]) * scale
        s = s - jnp.max(s, axis=-1, keepdims=True)
        e = jnp.exp(s)
        p = e / jnp.sum(e, axis=-1, keepdims=True)
        o_ref[:, hs] = _dot(p.astype(BF16), v_ref[:, hs]).astype(BF16)


def _mem_attn(p_attn, kv, batch, seq, mem_len, tq):
    m = p_attn.shape[0]
    w = MEM_HEADS * LANES
    nq = seq // tq
    kern = functools.partial(_mem_attn_kernel, scale=LANES ** -0.5)
    return pl.pallas_call(
        kern,
        grid=(batch, nq),
        in_specs=[
            pl.BlockSpec((tq, w), lambda b, i: (b * nq + i, 3)),
            pl.BlockSpec((mem_len, w), lambda b, i: (b, 0)),
            pl.BlockSpec((mem_len, w), lambda b, i: (b, 1)),
        ],
        out_specs=pl.BlockSpec((tq, w), lambda b, i: (b * nq + i, 0)),
        out_shape=jax.ShapeDtypeStruct((m, w), BF16),
        compiler_params=_cparams(("parallel", "parallel")),
        name="mem_attn",
    )(p_attn, kv, kv)


def _outproj_kernel(yr_ref, ys_ref, ym_ref, x_ref, g0_ref, b0_ref, w_ref, g_ref, b_ref, rw_ref, rb_ref,
                    h1_ref, lg_ref):
    c0 = yr_ref.shape[1]
    c1 = c0 + ys_ref.shape[1]
    tm = x_ref.shape[0]
    part = min(tm, 256)
    for r0 in range(0, tm, part):
        rs = slice(r0, r0 + part)
        mix = (_dot(yr_ref[rs, :], w_ref[0:c0, :]) + _dot(ys_ref[rs, :], w_ref[c0:c1, :])
               + _dot(ym_ref[rs, :], w_ref[c1:, :]))
        h = _layer_norm(x_ref[rs, :], g0_ref[...], b0_ref[...])
        h1 = _layer_norm(DEEPNORM_ALPHA * h + mix, g_ref[...], b_ref[...])
        h1_ref[rs, :] = h1
        hi = h1.astype(BF16)
        lo = (h1 - hi.astype(F32)).astype(BF16)
        both = _dot(hi, rw_ref[...])
        lg_ref[rs, :] = both[:, :LANES] + both[:, LANES:] + _dot(lo, rw_ref[:, :LANES]) + rb_ref[...]


def _outproj(y_r, y_s, y_m, x2, g0, b0, w, g, b, r_w, r_b, tm):
    m, d = x2.shape
    full = lambda a: pl.BlockSpec(a.shape, lambda i: (0, 0))
    rows = lambda a: pl.BlockSpec((tm, a.shape[1]), lambda i: (i, 0))
    return pl.pallas_call(
        _outproj_kernel,
        grid=(m // tm,),
        in_specs=[rows(y_r), rows(y_s), rows(y_m), rows(x2), full(g0), full(b0), full(w), full(g), full(b),
                  full(r_w), full(r_b)],
        out_specs=[pl.BlockSpec((tm, d), lambda i: (i, 0)),
                   pl.BlockSpec((tm, LANES), lambda i: (i, 0))],
        out_shape=[jax.ShapeDtypeStruct((m, d), F32),
                   jax.ShapeDtypeStruct((m, LANES), F32)],
        compiler_params=_cparams(("parallel",)),
        name="outproj",
    )(y_r, y_s, y_m, x2, g0, b0, w, g, b, r_w, r_b)


def _route_kernel(lg_ref, id_ref, wt_ref, *, n_groups, per_group):
    lg = lg_ref[...]
    lane_i = lax.broadcasted_iota(jnp.int32, lg.shape, 1)
    lane = lane_i.astype(F32)
    neg = jnp.float32(-jnp.inf)
    big = jnp.float32(2 ** 20)

    def first_max(vals):
        mx = jnp.max(vals, axis=-1, keepdims=True)
        idx = jnp.min(jnp.where(vals == mx, lane, big), axis=-1, keepdims=True)
        return mx, idx

    is_group = lane < n_groups
    gmax, gidx = first_max(jnp.where(is_group, lg, neg))
    gsum = jnp.sum(jnp.where(is_group, jnp.exp(lg - gmax), 0.0), axis=-1, keepdims=True)
    group_w = 1.0 / gsum
    lo = n_groups + gidx * per_group
    in_group = (lane >= lo) & (lane < lo + per_group)
    v1, i1 = first_max(jnp.where(in_group, lg, neg))
    v2, i2 = first_max(jnp.where(in_group & (lane != i1), lg, neg))
    e2 = jnp.exp(v2 - v1)
    w1 = group_w / (1.0 + e2)
    w2 = group_w * e2 / (1.0 + e2)
    ids = jnp.where(lane_i == 0, i1 - n_groups, jnp.where(lane_i == 1, i2 - n_groups, 0.0))
    id_ref[...] = ids.T[0:8, :].astype(jnp.int32)
    wt_ref[...] = jnp.where(lane_i == 0, w1, jnp.where(lane_i == 1, w2, 0.0))


def _route(logits, n_groups, per_group, tm):
    m = logits.shape[0]
    kern = functools.partial(_route_kernel, n_groups=n_groups, per_group=per_group)
    spec = pl.BlockSpec((tm, LANES), lambda i: (i, 0))
    return pl.pallas_call(
        kern,
        grid=(m // tm,),
        in_specs=[spec],
        out_specs=[pl.BlockSpec((8, tm), lambda i: (0, i)), spec],
        out_shape=[jax.ShapeDtypeStruct((8, m), jnp.int32),
                   jax.ShapeDtypeStruct((m, LANES), F32)],
        compiler_params=_cparams(("parallel",)),
        name="route",
    )(logits)


def _plan_kernel(id_ref, pos_ref, meta_ref, cnt_ref, base_ref, start_ref, *, tb, rows, n_experts, nb):
    phase = pl.program_id(0)
    j = pl.program_id(1)
    sub = lax.broadcasted_iota(jnp.int32, (LANES, tb), 0)
    e1 = id_ref[0:1, :]
    e2 = id_ref[1:2, :]
    hit1 = jnp.where(sub == e1, 1.0, 0.0)
    hit2 = jnp.where(sub == e2, 1.0, 0.0)
    hits = hit1 + hit2
    per_expert = jnp.sum(hits, axis=1, keepdims=True)

    @pl.when((phase == 0) & (j == 0))
    def _():
        cnt_ref[...] = jnp.zeros_like(cnt_ref)

    @pl.when(phase == 0)
    def _():
        cnt_ref[...] += per_expert

    sq_r = lax.broadcasted_iota(jnp.int32, (LANES, LANES), 0)
    sq_c = lax.broadcasted_iota(jnp.int32, (LANES, LANES), 1)

    @pl.when((phase == 1) & (j == 0))
    def _():
        n_blk = jnp.floor((cnt_ref[...] + (rows - 1)) * (1.0 / rows))
        before = jnp.where(sq_c < sq_r, 1.0, 0.0).astype(BF16)
        blk_start = _dot(before, jnp.broadcast_to(n_blk, (LANES, LANES)).astype(BF16))
        start_ref[...] = blk_start[:, 0:1]
        base_ref[...] = jnp.zeros_like(base_ref)
        blk_end = blk_start + n_blk
        n_used = jnp.sum(jnp.where(sq_r[:, 0:1] < n_experts, n_blk, 0.0), axis=0, keepdims=True)
        owner = jnp.sum(jnp.where((sq_r < n_experts) & (blk_end <= sq_c.astype(F32)), 1.0, 0.0),
                        axis=0, keepdims=True)
        last_owner = jnp.max(jnp.where(n_blk > 0.0, sq_r[:, 0:1].astype(F32), 0.0), axis=0, keepdims=True)
        blk = sq_c[0:1, :].astype(F32)
        in_use = blk < n_used
        owner = jnp.where(in_use, owner, last_owner)
        mine = sq_r.astype(F32) == owner
        cnt_o = jnp.sum(jnp.where(mine, cnt_ref[...], 0.0), axis=0, keepdims=True)
        start_o = jnp.sum(jnp.where(mine, blk_start, 0.0), axis=0, keepdims=True)
        valid = jnp.clip(cnt_o - (blk - start_o) * rows, 0.0, float(rows))
        valid = jnp.where(in_use, valid, 0.0)
        row8 = lax.broadcasted_iota(jnp.int32, (8, LANES), 0)
        meta = jnp.where(row8 == 0, owner, jnp.where(row8 == 1, valid, jnp.where(row8 == 2, n_used, 0.0)))
        meta_ref[...] = meta.astype(jnp.int32)

    @pl.when(phase == 1)
    def _():
        tr = lax.broadcasted_iota(jnp.int32, (tb, tb), 0)
        tc = lax.broadcasted_iota(jnp.int32, (tb, tb), 1)
        earlier = jnp.where(tr < tc, 1.0, 0.0).astype(BF16)
        seen = _dot(hits.astype(BF16), earlier)
        slot = start_ref[...] * rows + base_ref[...] + seen
        p1 = jnp.sum(hit1 * slot, axis=0, keepdims=True)
        p2 = jnp.sum(hit2 * slot, axis=0, keepdims=True)
        row8 = lax.broadcasted_iota(jnp.int32, (8, tb), 0)
        pos_ref[...] = jnp.where(row8 == 0, p1, jnp.where(row8 == 1, p2, 0.0)).astype(jnp.int32)
        base_ref[...] += per_expert


def _plan(ids_t, n_experts, rows, tb):
    m = ids_t.shape[1]
    nb = m // tb
    kern = functools.partial(_plan_kernel, tb=tb, rows=rows, n_experts=n_experts, nb=nb)
    col = pltpu.VMEM((LANES, 1), F32)
    return pl.pallas_call(
        kern,
        grid=(2, nb),
        in_specs=[pl.BlockSpec((8, tb), lambda p, j: (0, j))],
        out_specs=[pl.BlockSpec((8, tb), lambda p, j: (0, j * p)),
                   pl.BlockSpec((8, LANES), lambda p, j: (0, 0))],
        out_shape=[jax.ShapeDtypeStruct((8, m), jnp.int32),
                   jax.ShapeDtypeStruct((8, LANES), jnp.int32)],
        scratch_shapes=[col, col, col],
        compiler_params=_cparams(("arbitrary", "arbitrary")),
        name="moe_plan",
    )(ids_t)


def _for_range(lo, hi, body):
    full = (hi - lo) // ROW_GROUP
    first = lo // ROW_GROUP

    def group(g, _):
        for u in range(ROW_GROUP):
            body(first + g, u)
        return 0

    lax.fori_loop(0, full, group, 0)

    def one(r, _):
        body(r // ROW_GROUP, r % ROW_GROUP)
        return 0

    lax.fori_loop(lo + full * ROW_GROUP, hi, one, 0)


def _row_in(src_hbm, dst_vmem, sem, src_row, tile, sub):
    return pltpu.make_async_copy(src_hbm.at[pl.ds(src_row, 1)], dst_vmem.at[tile, pl.ds(sub, 1)], sem)


def _row_out(src_vmem, dst_hbm, sem, tile, sub, dst_row):
    return pltpu.make_async_copy(src_vmem.at[tile, pl.ds(sub, 1)], dst_hbm.at[pl.ds(dst_row, 1)], sem)


def _gather_cparams(sem):
    return pltpu.CompilerParams(dimension_semantics=sem, vmem_limit_bytes=VMEM_LIMIT,
                                disable_bounds_checks=True)


def _ffn_kernel(asg_ref, first_ref, be_ref, nv_ref, nu_ref, h_hbm, wg_ref, wu_ref, wd_ref, y_hbm,
                xf_ref, xb_ref, acc_ref, sem_in, sem_out, *, rows, sub, nj, n_tok, n_blocks):
    i = pl.program_id(0)
    j = pl.program_id(1)
    n_used = nu_ref[0]
    used = i < n_used
    slot = i % 2

    def gather(block, lo, hi, buf):
        base = first_ref[block]

        def body(tile, sub_row):
            a = asg_ref[base + tile * ROW_GROUP + sub_row]
            tok = jnp.where(a >= n_tok, a - n_tok, a)
            _row_in(h_hbm, xf_ref.at[buf], sem_in.at[buf], tok, tile, sub_row).start()

        _for_range(lo, hi, body)

    def gather_wait(block, buf):
        _for_range(0, nv_ref[block],
                   lambda tile, sub_row: _row_in(h_hbm, xf_ref.at[buf], sem_in.at[buf], 0, tile, sub_row).wait())

    def emit(block, buf):
        base = first_ref[block]

        def body(tile, sub_row):
            a = asg_ref[base + tile * ROW_GROUP + sub_row]
            _row_out(acc_ref.at[buf], y_hbm, sem_out.at[buf], tile, sub_row, a).start()

        _for_range(0, nv_ref[block], body)

    def emit_wait(block, buf):
        _for_range(0, nv_ref[block],
                   lambda tile, sub_row: _row_out(acc_ref.at[buf], y_hbm, sem_out.at[buf], tile, sub_row, 0).wait())

    def as_rows(x):
        return x.reshape(x.shape[0] * ROW_GROUP, x.shape[2])

    def as_tiles(x):
        return x.reshape(x.shape[0] // ROW_GROUP, ROW_GROUP, x.shape[1])

    @pl.when((i == 0) & (j == 0))
    def _():
        xf_ref[...] = jnp.zeros_like(xf_ref)
        gather(0, 0, nv_ref[0], 0)

    @pl.when(used & (j == 0))
    def _():
        gather_wait(i, slot)

    @pl.when(i + 1 < n_used)
    def _():
        nxt = nv_ref[i + 1]
        q = rows // nj
        gather(i + 1, jnp.minimum(j * q, nxt), jnp.minimum((j + 1) * q, nxt), 1 - slot)

    n_sub = (nv_ref[i] + sub - 1) // sub
    for n in range(1, rows // sub + 1):
        @pl.when(used & (n_sub == n))
        def _(n=n):
            r = n * sub
            nt = r // ROW_GROUP

            @pl.when(j == 0)
            def _():
                xb_ref[0:r, :] = as_rows(xf_ref[slot, 0:nt]).astype(BF16)
                acc_ref[slot, 0:nt] = jnp.zeros((nt,) + acc_ref.shape[2:], F32)

            xb = xb_ref[0:r, :]
            gate = _dot(xb, wg_ref[...].astype(BF16))
            up = _dot(xb, wu_ref[...].astype(BF16))
            hid = (gate * jax.nn.sigmoid(gate)) * up
            acc_ref[slot, 0:nt] += as_tiles(_dot(hid.astype(BF16), wd_ref[...].astype(BF16)))

    @pl.when((j == nj - 1) & (i >= 1) & (i <= n_used))
    def _():
        emit_wait(i - 1, 1 - slot)

    @pl.when((j == nj - 1) & used)
    def _():
        emit(i, slot)

    @pl.when((j == nj - 1) & used & (i == n_blocks - 1))
    def _():
        emit_wait(i, slot)


def _moe_ffn(asg_order, block_first, block_expert, n_valid, n_used, h1, w_gate, w_up, w_down, n_blocks, rows,
             tf):
    n_tok, d = h1.shape
    de = w_gate.shape[2]
    nj = de // tf
    kern = functools.partial(_ffn_kernel, rows=rows, sub=MOE_SUB, nj=nj, n_tok=n_tok, n_blocks=n_blocks)

    def jidx(i, j, nu):
        return jnp.where(i < nu[0], j, nj - 1)

    return pl.pallas_call(
        kern,
        grid_spec=pltpu.PrefetchScalarGridSpec(
            num_scalar_prefetch=5,
            grid=(n_blocks, nj),
            in_specs=[
                pl.BlockSpec(memory_space=pl.ANY),
                pl.BlockSpec((None, d, tf), lambda i, j, asg, fi, be, nv, nu: (be[i], 0, jidx(i, j, nu))),
                pl.BlockSpec((None, d, tf), lambda i, j, asg, fi, be, nv, nu: (be[i], 0, jidx(i, j, nu))),
                pl.BlockSpec((None, tf, d), lambda i, j, asg, fi, be, nv, nu: (be[i], jidx(i, j, nu), 0)),
            ],
            out_specs=pl.BlockSpec(memory_space=pl.ANY),
            scratch_shapes=[pltpu.VMEM((2, rows // ROW_GROUP, ROW_GROUP, d), F32),
                            pltpu.VMEM((rows, d), BF16),
                            pltpu.VMEM((2, rows // ROW_GROUP, ROW_GROUP, d), F32),
                            pltpu.SemaphoreType.DMA((2,)),
                            pltpu.SemaphoreType.DMA((2,))],
        ),
        out_shape=jax.ShapeDtypeStruct((2 * n_tok, d), F32),
        compiler_params=_gather_cparams(("arbitrary", "arbitrary")),
        name="moe_ffn",
    )(asg_order, block_first, block_expert, n_valid, n_used, h1, w_gate, w_up, w_down)


def _combine_kernel(h_ref, y0_ref, y1_ref, wt_ref, g_ref, b_ref, o_ref):
    wt = wt_ref[...]
    ffn = y0_ref[...] * wt[:, 0:1] + y1_ref[...] * wt[:, 1:2]
    o_ref[...] = _layer_norm(DEEPNORM_ALPHA * h_ref[...] + ffn, g_ref[...], b_ref[...])


def _combine(h1, ys, wts, g, b, tm):
    m, d = h1.shape
    nt = m // tm
    rows = pl.BlockSpec((tm, d), lambda i: (i, 0))
    vec = pl.BlockSpec((1, d), lambda i: (0, 0))
    return pl.pallas_call(
        _combine_kernel,
        grid=(nt,),
        in_specs=[rows, rows, pl.BlockSpec((tm, d), lambda i: (nt + i, 0)),
                  pl.BlockSpec((tm, LANES), lambda i: (i, 0)), vec, vec],
        out_specs=rows,
        out_shape=jax.ShapeDtypeStruct((m, d), F32),
        compiler_params=_cparams(("parallel",)),
        name="moe_combine",
    )(h1, ys, ys, wts, g, b)


def _dispatch_plan(ids_t, n_experts, rows):
    m = ids_t.shape[1]
    n_blocks = -(-(2 * m) // rows) + n_experts
    assert n_blocks <= LANES
    pos_t, meta = _plan(ids_t, n_experts, rows, _pick(m, 512))
    pos = pos_t[:2].reshape(-1)
    asg_order = jnp.argsort(pos).astype(jnp.int32)
    n_valid = meta[1, :n_blocks]
    block_first = (jnp.cumsum(n_valid) - n_valid).astype(jnp.int32)
    return asg_order, block_first, meta[2, :1], meta[0, :n_blocks], n_valid, n_blocks


def _pick(n, pref):
    t = min(pref, n)
    while n % t:
        t //= 2
    return t


def kernel(x, mem, ln_in_g, ln_in_b, w_in, tshift_mu, w0, w_decay_up, a0, w_a_up, w_g_up, k_k, k_a, r_k,
           lnx_g, lnx_b, w_mem_kv, w_out, ln1_g, ln1_b, router_group, router_group_b, router_expert,
           router_expert_b, w_e_gate, w_e_up, w_e_down, ln2_g, ln2_b):
    batch, seq, d = x.shape
    mem_len = mem.shape[1]
    m = batch * seq
    c = w0.shape[1]
    dr, ar, gr = w_decay_up.shape[1], w_a_up.shape[1], w_g_up.shape[1]
    rwkv_cols = 3 * c + dr + ar + gr
    sb_w = SB_HEADS * LANES
    mem_w = MEM_HEADS * LANES
    assert dr + ar == LANES and c % (2 * LANES) == 0 and w_in.shape[0] == DEPTH
    assert w_in.shape[2] == rwkv_cols + 3 * sb_w + mem_w
    n_experts = router_expert.shape[2]
    row = lambda a: a.reshape(1, -1)

    tn = 512
    low_w = -(-(dr + ar + gr) // LANES) * LANES
    rw_pad = -(-(3 * c + low_w) // tn) * tn
    wi = w_in[0]
    w_rwkv = jnp.pad(wi[:, :rwkv_cols].astype(BF16), ((0, 0), (0, rw_pad - rwkv_cols)))
    w_attn = wi[:, rwkv_cols:].astype(BF16)
    mu = row(jnp.pad(tshift_mu[0], (0, rw_pad - rwkv_cols)))
    lp = low_w - (dr + ar + gr)
    prm = dict(
        w0=row(w0[0]), a0=row(a0[0]), k_k=row(k_k[0]), k_a=row(k_a[0]), r_k=row(r_k[0]),
        lnx_g=row(lnx_g[0]), lnx_b=row(lnx_b[0]),
        wd=jnp.concatenate([w_decay_up[0], jnp.zeros((ar, c), F32)], axis=0).astype(BF16),
        wa=jnp.concatenate([jnp.zeros((dr, c), F32), w_a_up[0]], axis=0).astype(BF16),
        wg=jnp.concatenate([w_g_up[0], jnp.zeros((lp, c), F32)], axis=0).astype(BF16),
    )
    assert (3 * c) % low_w == 0

    x2 = x.reshape(m, d)
    hb = _ln_in(x2, row(ln_in_g), row(ln_in_b), _pick(m, 1024))
    p_rwkv, p_attn = _inproj(hb, w_rwkv, w_attn, mu, seq, _pick(seq, 2048), tn)

    y_rwkv = _wkv(p_rwkv, prm, batch, seq, n_pairs=4, tt=_pick(seq, 1024))
    y_sb = _sb_attn(p_attn, batch, seq, tq=LANES)
    kv = _mem_kv(mem.reshape(batch * mem_len, d), w_mem_kv[0].astype(BF16), _pick(batch * mem_len, 256))
    y_mem = _mem_attn(p_attn, kv, batch, seq, mem_len, tq=_pick(seq, 512))

    wo = w_out[0].astype(BF16)
    r_w = jnp.concatenate([router_group[0], router_expert[0],
                           jnp.zeros((d, LANES - N_GROUPS - n_experts), F32)], axis=1)
    r_b = jnp.concatenate([router_group_b[0], router_expert_b[0],
                           jnp.zeros((LANES - N_GROUPS - n_experts,), F32)]).reshape(1, LANES)
    r_hi = r_w.astype(BF16)
    r_w2 = jnp.concatenate([r_hi, (r_w - r_hi.astype(F32)).astype(BF16)], axis=1)
    h1, logits = _outproj(y_rwkv, y_sb, y_mem, x2, row(ln_in_g), row(ln_in_b), wo,
                          row(ln1_g[0]), row(ln1_b[0]), r_w2, r_b, _pick(m, 512))

    ids_t, wts = _route(logits, N_GROUPS, n_experts // N_GROUPS, _pick(m, 512))
    asg_order, block_first, n_used, block_expert, n_valid, n_blocks = _dispatch_plan(ids_t, n_experts, MOE_ROWS)
    ys = _moe_ffn(asg_order, block_first, block_expert, n_valid, n_used, h1, w_e_gate[0], w_e_up[0], w_e_down[0],
                  n_blocks, MOE_ROWS, tf=512)
    out = _combine(h1, ys, wts, row(ln2_g[0]), row(ln2_b[0]), _pick(m, 512))
    return out.reshape(batch, seq, d)
```

```python
import functools

import jax
import jax.numpy as jnp
from jax import lax
from jax.experimental import pallas as pl
from jax.experimental.pallas import tpu as pltpu

F32 = jnp.float32
BF16 = jnp.bfloat16

SB_HEADS = 4
MEM_HEADS = 4
N_GROUPS = 8
DEPTH = 1
DEEPNORM_ALPHA = (2.0 * DEPTH) ** 0.25
LN_EPS = 1e-5
GN_EPS = 64e-5

LANES = 128
WKV_CHUNK = 64
MOE_ROWS = 512
MOE_SUB = 64
ROW_GROUP = 8
SB_BLOCKS_PER_ITER = 4
VMEM_LIMIT = 56 * 1024 * 1024


def _cparams(sem):
    return pltpu.CompilerParams(dimension_semantics=sem, vmem_limit_bytes=VMEM_LIMIT)


def _layer_norm(x, g, b):
    mu = jnp.mean(x, axis=-1, keepdims=True)
    xc = x - mu
    var = jnp.mean(xc * xc, axis=-1, keepdims=True)
    return xc * lax.rsqrt(var + LN_EPS) * g + b


def _split2(x):
    hi = x.astype(BF16)
    return hi, (x - hi.astype(F32)).astype(BF16)


def _dot(a, b):
    return jnp.dot(a, b, preferred_element_type=F32)


def _dot_nt(a, b):
    return lax.dot_general(a, b, (((1,), (1,)), ((), ())), preferred_element_type=F32)


def _dot_tn(a, b):
    return lax.dot_general(a, b, (((0,), (0,)), ((), ())), preferred_element_type=F32)


def _dot_exact_by_f32(m, x):
    hi, lo = _split2(x)
    return _dot(m, hi) + _dot(m, lo)


def _ln_kernel(x_ref, g_ref, b_ref, hb_ref):
    hb_ref[...] = _layer_norm(x_ref[...], g_ref[...], b_ref[...]).astype(BF16)


def _ln_in(x2, g, b, tm):
    m, d = x2.shape
    rows = pl.BlockSpec((tm, d), lambda i: (i, 0))
    vec = pl.BlockSpec((1, d), lambda i: (0, 0))
    return pl.pallas_call(
        _ln_kernel,
        grid=(m // tm,),
        in_specs=[rows, vec, vec],
        out_specs=rows,
        out_shape=jax.ShapeDtypeStruct((m, d), BF16),
        compiler_params=_cparams(("parallel",)),
        name="ln_in",
    )(x2, g, b)


def _inproj_kernel(hb_ref, wr_ref, wa_ref, mu_ref, pr_ref, pa_ref, last_ref, *, n_f32_tiles, tiles_per_seq):
    i = pl.program_id(0)
    n = pl.program_id(1)

    @pl.when((i == 0) & (n == 0))
    def _():
        last_ref[...] = jnp.zeros_like(last_ref)

    @pl.when(n < n_f32_tiles)
    def _():
        tm = pr_ref.shape[0]
        p = _dot(hb_ref[...], wr_ref[...])
        slot = jnp.minimum(n, n_f32_tiles - 1)
        carried = jnp.where(i % tiles_per_seq == 0, 0.0, last_ref[slot])
        prev = pltpu.roll(p, shift=1, axis=0)
        prev = jnp.where(lax.broadcasted_iota(jnp.int32, (tm, 1), 0) == 0, carried, prev)
        last_ref[slot] = p[tm - 1:tm, :]
        pr_ref[...] = p + (prev - p) * mu_ref[...]

    @pl.when(n >= n_f32_tiles)
    def _():
        pa_ref[...] = _dot(hb_ref[...], wa_ref[...]).astype(BF16)


def _inproj(hb, w_rwkv, w_attn, mu, seq, tm, tn):
    m, d = hb.shape
    n_rwkv_cols, n_attn_cols = w_rwkv.shape[1], w_attn.shape[1]
    nf = n_rwkv_cols // tn
    kern = functools.partial(_inproj_kernel, n_f32_tiles=nf, tiles_per_seq=seq // tm)
    first = lambda i, n: (0, jnp.minimum(n, nf - 1))
    second = lambda i, n: (0, jnp.maximum(n - nf, 0))
    return pl.pallas_call(
        kern,
        grid=(m // tm, (n_rwkv_cols + n_attn_cols) // tn),
        in_specs=[
            pl.BlockSpec((tm, d), lambda i, n: (i, 0)),
            pl.BlockSpec((d, tn), first),
            pl.BlockSpec((d, tn), second),
            pl.BlockSpec((1, tn), first),
        ],
        out_specs=[
            pl.BlockSpec((tm, tn), lambda i, n: (i, jnp.minimum(n, nf - 1))),
            pl.BlockSpec((tm, tn), lambda i, n: (i, jnp.maximum(n - nf, 0))),
        ],
        out_shape=[
            jax.ShapeDtypeStruct((m, n_rwkv_cols), F32),
            jax.ShapeDtypeStruct((m, n_attn_cols), BF16),
        ],
        scratch_shapes=[pltpu.VMEM((nf, 1, tn), F32)],
        compiler_params=_cparams(("arbitrary", "arbitrary")),
        name="inproj",
    )(hb, w_rwkv, w_attn, mu)


def _wkv_kernel(pr_ref, pk_ref, pv_ref, pl_ref,
                w0_ref, a0_ref, kk_ref, ka_ref, rk_ref, lg_ref, lb_ref,
                wd_ref, wa_ref, wg_ref,
                y_ref,
                s_ref,
                r_s, lw_s, k_s, v_s, a_s, b_s, y_s, g_s, bo_s, zero_ref, ring64, ring128, ringg,
                *, n_pairs, tt):
    C = WKV_CHUNK
    t_idx = pl.program_id(2)

    @pl.when(t_idx == 0)
    def _():
        s_ref[...] = jnp.zeros_like(s_ref)

    lane = lax.broadcasted_iota(jnp.int32, (LANES, LANES), 1)
    sub = lax.broadcasted_iota(jnp.int32, (LANES, LANES), 0)
    head_ones = jnp.where((lane // 64) == (sub // 64), 1.0, 0.0).astype(BF16)

    def head_sum(x):
        hi = x.astype(BF16)
        lo = (x - hi.astype(F32)).astype(BF16)
        return _dot(hi, head_ones) + _dot(lo, head_ones)

    da = pl_ref[:, 0:LANES]
    th = jnp.tanh(da).astype(BF16)
    sg = jax.nn.sigmoid(pl_ref[:, LANES:]).astype(BF16)
    da = da.astype(BF16)
    for g in range(n_pairs):
        cs = slice(g * LANES, (g + 1) * LANES)
        rg, kg, vg = pr_ref[:, cs], pk_ref[:, cs], pv_ref[:, cs]
        pre = w0_ref[:, cs] + _dot(th, wd_ref[:, cs])
        softplus_neg = jnp.maximum(-pre, 0.0) + jnp.log(1.0 + jnp.exp(-jnp.abs(pre)))
        w_log = -softplus_neg - 0.5
        lw = -jnp.exp(w_log)
        a = jax.nn.sigmoid(a0_ref[:, cs] + _dot(da, wa_ref[:, cs]))
        gate = _dot(sg, wg_ref[:, cs])
        kk = kg * kk_ref[:, cs]
        kk = kk * lax.rsqrt(jnp.maximum(head_sum(kk * kk), 1e-24))
        k2 = kg * (1.0 + (a - 1.0) * ka_ref[:, cs])
        bonus = head_sum(rg * k2 * rk_ref[:, cs]) * vg
        r_s[:, cs] = rg
        lw_s[:, cs] = lw
        k_s[:, cs] = k2
        v_s[:, cs] = vg
        a_s[:, cs] = -kk
        b_s[:, cs] = kk * a
        g_s[:, cs] = gate
        bo_s[:, cs] = bonus

    ci = lax.broadcasted_iota(jnp.int32, (C, 2 * C), 0)
    cj = lax.broadcasted_iota(jnp.int32, (C, 2 * C), 1)
    left = cj < C
    strict = (cj % C) < ci
    incl = (cj % C) <= ci
    tri_incl = jnp.where(lax.broadcasted_iota(jnp.int32, (C, C), 1)
                         <= lax.broadcasted_iota(jnp.int32, (C, C), 0), 1.0, 0.0).astype(BF16)
    lane_c = lax.broadcasted_iota(jnp.int32, (C, LANES), 1)
    m0 = lane_c < 64
    eye = jnp.where(lane == sub, 1.0, 0.0).astype(F32)
    blockdiag = (lane // 64) == (sub // 64)

    csl = [slice(g * LANES, (g + 1) * LANES) for g in range(n_pairs)]
    P = range(n_pairs)
    cat0 = lambda *xs: jnp.concatenate(xs, axis=0)
    cat1 = lambda *xs: jnp.concatenate(xs, axis=1)
    bf = lambda x: x.astype(BF16)

    n_chunks = tt // C
    MM0, MM1, RTB, VB = range(4)
    BK, AK, ATB = range(3)
    ring64[...] = jnp.zeros_like(ring64)
    ring128[...] = jnp.zeros_like(ring128)
    ringg[...] = jnp.zeros_like(ringg)

    def stage1(c, out):
        rows = pl.ds(pl.multiple_of(c * C, C), C)
        slot = c % 4
        ld = lambda ref: [ref[rows, csl[g]] for g in P]
        rc, lwc, kc, vc, ac, bc = ld(r_s), ld(lw_s), ld(k_s), ld(v_s), ld(a_s), ld(b_s)
        cum = [_dot_exact_by_f32(tri_incl, lwc[g]) for g in P]
        yield
        last = [cum[g][C - 1:C, :] for g in P]
        rt = [rc[g] * jnp.exp(cum[g]) for g in P]
        at = [ac[g] * jnp.exp(cum[g] - lwc[g]) for g in P]
        ginv = [jnp.exp(-cum[g]) for g in P]
        btb = [bf(bc[g] * ginv[g]) for g in P]
        ktb = [bf(kc[g] * ginv[g]) for g in P]
        ghat = [jnp.exp(last[g] - cum[g]) for g in P]
        lhs0 = [bf(cat0(jnp.where(m0, at[g], 0.0), jnp.where(m0, rt[g], 0.0))) for g in P]
        lhs1 = [bf(cat0(jnp.where(m0, 0.0, at[g]), jnp.where(m0, 0.0, rt[g]))) for g in P]
        for g in P:
            ring128[slot, g, BK] = cat0(bf(bc[g] * ghat[g]), bf(kc[g] * ghat[g]))
            ring128[slot, g, ATB] = cat0(lhs0[g][:C], lhs1[g][:C])
            ring64[slot, g, VB] = bf(vc[g])
            ring64[slot, g, RTB] = bf(rt[g])
            ringg[slot, g, 0:1, :] = jnp.exp(last[g])
        x0 = [_dot_nt(lhs0[g], cat0(btb[g], ktb[g])) for g in P]
        x1 = [_dot_nt(lhs1[g], cat0(ktb[g], btb[g])) for g in P]
        yield
        n_bd = [cat0(jnp.where(left & strict, x0[g][:C], 0.0),
                     jnp.where((~left) & strict, x1[g][:C], 0.0)) for g in P]
        for g in P:
            ring128[slot, g, AK] = bf(cat0(jnp.where((~left) & strict, x0[g][:C], 0.0),
                                           jnp.where(left & strict, x1[g][:C], 0.0)))
            ring64[slot, g, MM0] = bf(jnp.where(incl, x0[g][C:], 0.0))
            ring64[slot, g, MM1] = bf(jnp.where(incl, x1[g][C:], 0.0))
        out["t"] = [eye + n_bd[g] for g in P]
        nb = [bf(n_bd[g]) for g in P]
        out["pw"] = [bf(_dot(nb[g], nb[g])) for g in P]

    def inverse_level(t, pw):
        res = [_dot(pw[g], cat1(pw[g], bf(t[g]))) for g in P]
        return [t[g] + res[g][:, LANES:] for g in P], [bf(res[g][:, :LANES]) for g in P]

    def stage2(inp, out):
        t, pw = inp["t"], inp["pw"]
        for level in range(3):
            t, pw = inverse_level(t, pw)
            if level < 2:
                yield
        out["t"], out["pw"] = t, pw

    def stage3(inp, c, out):
        slot = c % 4
        t, pw = inverse_level(inp["t"], inp["pw"])
        vb = [ring64[slot, g, VB] for g in P]
        av = [_dot(ring128[slot, g, AK], cat0(vb[g], vb[g])) for g in P]
        yield
        t = [t[g] + _dot(pw[g], bf(t[g])) for g in P]
        av = [cat0(jnp.where(m0, av[g][:C], 0.0), jnp.where(m0, 0.0, av[g][C:])) for g in P]
        yield
        wu = [_dot(bf(t[g]), cat1(ring128[slot, g, ATB], bf(av[g]))) for g in P]
        out["w"] = [bf(wu[g][:C, :LANES] + wu[g][C:, :LANES]) for g in P]
        out["u0"] = [wu[g][:C, LANES:] + wu[g][C:, LANES:] for g in P]

    def stage4(inp, c, valid):
        rows = pl.ds(pl.multiple_of(c * C, C), C)
        slot = c % 4
        vb = [ring64[slot, g, VB] for g in P]
        s = [s_ref[g] for g in P]
        sb = [bf(s[g]) for g in P]
        ub = [bf(_dot_nt(inp["w"][g], sb[g]) + inp["u0"][g]) for g in P]
        ys = [_dot_nt(ring64[slot, g, RTB], sb[g]) for g in P]
        yield
        uv = [cat0(ub[g], vb[g]) for g in P]
        y = [ys[g] + jnp.where(m0, _dot(ring64[slot, g, MM0], uv[g]),
                               _dot(ring64[slot, g, MM1], cat0(vb[g], ub[g]))) for g in P]
        upd = [_dot_tn(uv[g], ring128[slot, g, BK]) for g in P]
        yield
        for g in P:
            s_new = s[g] * ringg[slot, g, 0:1, :] + jnp.where(blockdiag, upd[g], 0.0)
            s_ref[g] = jnp.where(valid, s_new, s[g])
            y_s[rows, csl[g]] = y[g]

    def interleave(gens):
        live = list(gens)
        while live:
            still = []
            for gen in live:
                try:
                    next(gen)
                    still.append(gen)
                except StopIteration:
                    pass
            live = still

    def body(it, carry):
        o1, o2, o3 = carry
        n1, n2, n3 = {}, {}, {}
        interleave([stage4(o3, jnp.maximum(it - 3, 0), it >= 3),
                    stage3(o2, jnp.maximum(it - 2, 0), n3),
                    stage2(o1, n2),
                    stage1(jnp.minimum(it, n_chunks - 1), n1)])
        return n1, n2, n3

    zero_ref[...] = jnp.zeros_like(zero_ref)
    zb = lambda r: [zero_ref[0:r, :].astype(BF16) for _ in P]
    zf = lambda r: [zero_ref[0:r, :] for _ in P]
    front = lambda: dict(t=zf(2 * C), pw=zb(2 * C))
    lax.fori_loop(0, n_chunks + 3, body, (front(), front(), dict(w=zb(C), u0=zf(C))))

    for g in range(n_pairs):
        cs = slice(g * LANES, (g + 1) * LANES)
        y = y_s[:, cs]
        mean = head_sum(y) * (1.0 / 64.0)
        yc = y - mean
        var = head_sum(yc * yc) * (1.0 / 64.0)
        yn = yc * lax.rsqrt(var + GN_EPS) * lg_ref[:, cs] + lb_ref[:, cs]
        y_ref[:, cs] = ((yn + bo_s[:, cs]) * g_s[:, cs]).astype(BF16)


def _wkv(p_rwkv, prm, batch, seq, n_pairs, tt):
    m = p_rwkv.shape[0]
    c = prm["w0"].shape[1]
    gw = n_pairs * LANES
    n_col_blocks = c // gw
    nt = seq // tt
    low_w = LANES + prm["wg"].shape[0]

    def pspec(off):
        return pl.BlockSpec((tt, gw), lambda b, g, t: (b * nt + t, off * n_col_blocks + g))

    def vspec():
        return pl.BlockSpec((1, gw), lambda b, g, t: (0, g))

    def wspec(rows):
        return pl.BlockSpec((rows, gw), lambda b, g, t: (0, g))

    kern = functools.partial(_wkv_kernel, n_pairs=n_pairs, tt=tt)
    tile = pltpu.VMEM((tt, gw), F32)
    return pl.pallas_call(
        kern,
        grid=(batch, n_col_blocks, nt),
        in_specs=[
            pspec(0), pspec(1), pspec(2),
            pl.BlockSpec((tt, low_w), lambda b, g, t: (b * nt + t, (3 * c) // low_w)),
            vspec(), vspec(), vspec(), vspec(), vspec(), vspec(), vspec(),
            wspec(LANES), wspec(LANES), wspec(low_w - LANES),
        ],
        out_specs=pl.BlockSpec((tt, gw), lambda b, g, t: (b * nt + t, g)),
        out_shape=jax.ShapeDtypeStruct((m, c), BF16),
        scratch_shapes=[
            pltpu.VMEM((n_pairs, LANES, LANES), F32),
            tile, tile, tile, tile, tile, tile, tile, tile, tile,
            pltpu.VMEM((LANES, LANES), F32),
            pltpu.VMEM((4, n_pairs, 4, WKV_CHUNK, LANES), BF16),
            pltpu.VMEM((4, n_pairs, 3, 2 * WKV_CHUNK, LANES), BF16),
            pltpu.VMEM((4, n_pairs, 8, LANES), F32),
        ],
        compiler_params=_cparams(("parallel", "parallel", "arbitrary")),
        name="wkv7",
    )(p_rwkv, p_rwkv, p_rwkv, p_rwkv,
      prm["w0"], prm["a0"], prm["k_k"], prm["k_a"], prm["r_k"], prm["lnx_g"], prm["lnx_b"],
      prm["wd"], prm["wa"], prm["wg"])


def _sb_kernel(q_ref, k_ref, v_ref, o_ref, acc_ref, right_ref, *, tq, scale):
    i = pl.program_id(1)
    row = lax.broadcasted_iota(jnp.int32, (tq, tq), 0)
    col = lax.broadcasted_iota(jnp.int32, (tq, tq), 1)
    r2 = lax.broadcasted_iota(jnp.int32, (tq, 2 * tq), 0)
    c2 = lax.broadcasted_iota(jnp.int32, (tq, 2 * tq), 1)
    after_and_total = jnp.where((c2 >= tq) | (r2 > c2), 1.0, 0.0).astype(BF16)
    diag = col < row

    heads = range(SB_HEADS)
    hsl = [slice(h * LANES, (h + 1) * LANES) for h in heads]

    def blocks(js, first):
        nb = len(js)
        it = [(b, h) for b in range(nb) for h in heads]
        on_diag = lambda p: first and p[0] == 0
        ks = [pl.ds(pl.multiple_of(j * tq, tq), tq) for j in js]
        z = {p: _dot_nt(q_ref[:, hsl[p[1]]], k_ref[ks[p[0]], hsl[p[1]]]) * scale for p in it}
        sp = {p: jnp.maximum(z[p], 0.0) + jnp.log(1.0 + jnp.exp(-jnp.abs(z[p]))) for p in it}
        log_keep = {p: jnp.where(diag, -sp[p], 0.0) if on_diag(p) else -sp[p] for p in it}
        hi = {p: log_keep[p].astype(BF16) for p in it}
        lo = {p: (log_keep[p] - hi[p].astype(F32)).astype(BF16) for p in it}
        sums = {p: _dot(hi[p], after_and_total) + _dot(lo[p], after_and_total) for p in it}
        right = {}
        for h in heads:
            run = None if first else right_ref[h]
            for b in range(nb):
                right[(b, h)] = run
                tot = sums[(b, h)][:, tq:]
                run = tot if run is None else run + tot
            right[("end", h)] = run
        after = {p: sums[p][:, :tq] if right[p] is None else sums[p][:, :tq] + right[p] for p in it}
        attn = {p: jnp.exp(z[p] - sp[p] + after[p]) for p in it}
        attn = {p: jnp.where(diag, attn[p], 0.0) if on_diag(p) else attn[p] for p in it}
        pv = {p: _dot(attn[p].astype(BF16), v_ref[ks[p[0]], hsl[p[1]]]) for p in it}
        for h in heads:
            tot = pv[(0, h)]
            for b in range(1, nb):
                tot = tot + pv[(b, h)]
            if first:
                acc_ref[h] = tot
            else:
                acc_ref[h] += tot
            right_ref[h] = right[("end", h)]

    per = SB_BLOCKS_PER_ITER
    first_size = i % per + 1
    for size in range(1, per + 1):
        @pl.when(first_size == size)
        def _(size=size):
            blocks([i - b for b in range(size)], True)

    def body(jj, _):
        j = i - first_size - per * jj
        blocks([j - b for b in range(per)], False)
        return 0

    lax.fori_loop(0, (i + 1 - first_size) // per, body, 0)

    for h in heads:
        o_ref[:, hsl[h]] = acc_ref[h].astype(BF16)


def _sb_attn(p_attn, batch, seq, tq):
    m = p_attn.shape[0]
    w = SB_HEADS * LANES
    nq = seq // tq
    kern = functools.partial(_sb_kernel, tq=tq, scale=LANES ** -0.5)
    return pl.pallas_call(
        kern,
        grid=(batch, nq),
        in_specs=[
            pl.BlockSpec((tq, w), lambda b, i: (b * nq + i, 0)),
            pl.BlockSpec((seq, w), lambda b, i: (b, 1)),
            pl.BlockSpec((seq, w), lambda b, i: (b, 2)),
        ],
        out_specs=pl.BlockSpec((tq, w), lambda b, i: (b * nq + i, 0)),
        out_shape=jax.ShapeDtypeStruct((m, w), BF16),
        scratch_shapes=[pltpu.VMEM((SB_HEADS, tq, tq), F32), pltpu.VMEM((SB_HEADS, tq, tq), F32)],
        compiler_params=_cparams(("parallel", "arbitrary")),
        name="sb_attn",
    )(p_attn, p_attn, p_attn)


def _mem_kv_kernel(m_ref, w_ref, o_ref):
    o_ref[...] = _dot(m_ref[...].astype(BF16), w_ref[...]).astype(BF16)


def _mem_kv(mem2, w_bf16, tm):
    m, d = mem2.shape
    n = w_bf16.shape[1]
    return pl.pallas_call(
        _mem_kv_kernel,
        grid=(m // tm,),
        in_specs=[pl.BlockSpec((tm, d), lambda i: (i, 0)),
                  pl.BlockSpec((d, n), lambda i: (0, 0))],
        out_specs=pl.BlockSpec((tm, n), lambda i: (i, 0)),
        out_shape=jax.ShapeDtypeStruct((m, n), BF16),
        compiler_params=_cparams(("parallel",)),
        name="mem_kv",
    )(mem2, w_bf16)


def _mem_attn_kernel(q_ref, k_ref, v_ref, o_ref, *, scale):
    for h in range(MEM_HEADS):
        hs = slice(h * LANES, (h + 1) * LANES)
        s = _dot_nt(q_ref[:, hs], k_ref[:, hs]) * scale
        s = s - jnp.max(s, axis=-1, keepdims=True)
        e = jnp.exp(s)
        p = e / jnp.sum(e, axis=-1, keepdims=True)
        o_ref[:, hs] = _dot(p.astype(BF16), v_ref[:, hs]).astype(BF16)


def _mem_attn(p_attn, kv, batch, seq, mem_len, tq):
    m = p_attn.shape[0]
    w = MEM_HEADS * LANES
    nq = seq // tq
    kern = functools.partial(_mem_attn_kernel, scale=LANES ** -0.5)
    return pl.pallas_call(
        kern,
        grid=(batch, nq),
        in_specs=[
            pl.BlockSpec((tq, w), lambda b, i: (b * nq + i, 3)),
            pl.BlockSpec((mem_len, w), lambda b, i: (b, 0)),
            pl.BlockSpec((mem_len, w), lambda b, i: (b, 1)),
        ],
        out_specs=pl.BlockSpec((tq, w), lambda b, i: (b * nq + i, 0)),
        out_shape=jax.ShapeDtypeStruct((m, w), BF16),
        compiler_params=_cparams(("parallel", "parallel")),
        name="mem_attn",
    )(p_attn, kv, kv)


def _outproj_kernel(yr_ref, ys_ref, ym_ref, x_ref, g0_ref, b0_ref, w_ref, g_ref, b_ref, rw_ref, rb_ref,
                    h1_ref, lg_ref):
    c0 = yr_ref.shape[1]
    c1 = c0 + ys_ref.shape[1]
    tm = x_ref.shape[0]
    part = min(tm, 256)
    for r0 in range(0, tm, part):
        rs = slice(r0, r0 + part)
        mix = (_dot(yr_ref[rs, :], w_ref[0:c0, :]) + _dot(ys_ref[rs, :], w_ref[c0:c1, :])
               + _dot(ym_ref[rs, :], w_ref[c1:, :]))
        h = _layer_norm(x_ref[rs, :], g0_ref[...], b0_ref[...])
        h1 = _layer_norm(DEEPNORM_ALPHA * h + mix, g_ref[...], b_ref[...])
        h1_ref[rs, :] = h1
        hi = h1.astype(BF16)
        lo = (h1 - hi.astype(F32)).astype(BF16)
        both = _dot(hi, rw_ref[...])
        lg_ref[rs, :] = both[:, :LANES] + both[:, LANES:] + _dot(lo, rw_ref[:, :LANES]) + rb_ref[...]


def _outproj(y_r, y_s, y_m, x2, g0, b0, w, g, b, r_w, r_b, tm):
    m, d = x2.shape
    full = lambda a: pl.BlockSpec(a.shape, lambda i: (0, 0))
    rows = lambda a: pl.BlockSpec((tm, a.shape[1]), lambda i: (i, 0))
    return pl.pallas_call(
        _outproj_kernel,
        grid=(m // tm,),
        in_specs=[rows(y_r), rows(y_s), rows(y_m), rows(x2), full(g0), full(b0), full(w), full(g), full(b),
                  full(r_w), full(r_b)],
        out_specs=[pl.BlockSpec((tm, d), lambda i: (i, 0)),
                   pl.BlockSpec((tm, LANES), lambda i: (i, 0))],
        out_shape=[jax.ShapeDtypeStruct((m, d), F32),
                   jax.ShapeDtypeStruct((m, LANES), F32)],
        compiler_params=_cparams(("parallel",)),
        name="outproj",
    )(y_r, y_s, y_m, x2, g0, b0, w, g, b, r_w, r_b)


def _route_kernel(lg_ref, id_ref, wt_ref, *, n_groups, per_group):
    lg = lg_ref[...]
    lane_i = lax.broadcasted_iota(jnp.int32, lg.shape, 1)
    lane = lane_i.astype(F32)
    neg = jnp.float32(-jnp.inf)
    big = jnp.float32(2 ** 20)

    def first_max(vals):
        mx = jnp.max(vals, axis=-1, keepdims=True)
        idx = jnp.min(jnp.where(vals == mx, lane, big), axis=-1, keepdims=True)
        return mx, idx

    is_group = lane < n_groups
    gmax, gidx = first_max(jnp.where(is_group, lg, neg))
    gsum = jnp.sum(jnp.where(is_group, jnp.exp(lg - gmax), 0.0), axis=-1, keepdims=True)
    group_w = 1.0 / gsum
    lo = n_groups + gidx * per_group
    in_group = (lane >= lo) & (lane < lo + per_group)
    v1, i1 = first_max(jnp.where(in_group, lg, neg))
    v2, i2 = first_max(jnp.where(in_group & (lane != i1), lg, neg))
    e2 = jnp.exp(v2 - v1)
    w1 = group_w / (1.0 + e2)
    w2 = group_w * e2 / (1.0 + e2)
    ids = jnp.where(lane_i == 0, i1 - n_groups, jnp.where(lane_i == 1, i2 - n_groups, 0.0))
    id_ref[...] = ids.T[0:8, :].astype(jnp.int32)
    wt_ref[...] = jnp.where(lane_i == 0, w1, jnp.where(lane_i == 1, w2, 0.0))


def _route(logits, n_groups, per_group, tm):
    m = logits.shape[0]
    kern = functools.partial(_route_kernel, n_groups=n_groups, per_group=per_group)
    spec = pl.BlockSpec((tm, LANES), lambda i: (i, 0))
    return pl.pallas_call(
        kern,
        grid=(m // tm,),
        in_specs=[spec],
        out_specs=[pl.BlockSpec((8, tm), lambda i: (0, i)), spec],
        out_shape=[jax.ShapeDtypeStruct((8, m), jnp.int32),
                   jax.ShapeDtypeStruct((m, LANES), F32)],
        compiler_params=_cparams(("parallel",)),
        name="route",
    )(logits)


def _plan_kernel(id_ref, pos_ref, meta_ref, cnt_ref, base_ref, start_ref, *, tb, rows, n_experts, nb):
    phase = pl.program_id(0)
    j = pl.program_id(1)
    sub = lax.broadcasted_iota(jnp.int32, (LANES, tb), 0)
    e1 = id_ref[0:1, :]
    e2 = id_ref[1:2, :]
    hit1 = jnp.where(sub == e1, 1.0, 0.0)
    hit2 = jnp.where(sub == e2, 1.0, 0.0)
    hits = hit1 + hit2
    per_expert = jnp.sum(hits, axis=1, keepdims=True)

    @pl.when((phase == 0) & (j == 0))
    def _():
        cnt_ref[...] = jnp.zeros_like(cnt_ref)

    @pl.when(phase == 0)
    def _():
        cnt_ref[...] += per_expert

    sq_r = lax.broadcasted_iota(jnp.int32, (LANES, LANES), 0)
    sq_c = lax.broadcasted_iota(jnp.int32, (LANES, LANES), 1)

    @pl.when((phase == 1) & (j == 0))
    def _():
        n_blk = jnp.floor((cnt_ref[...] + (rows - 1)) * (1.0 / rows))
        before = jnp.where(sq_c < sq_r, 1.0, 0.0).astype(BF16)
        blk_start = _dot(before, jnp.broadcast_to(n_blk, (LANES, LANES)).astype(BF16))
        start_ref[...] = blk_start[:, 0:1]
        base_ref[...] = jnp.zeros_like(base_ref)
        blk_end = blk_start + n_blk
        n_used = jnp.sum(jnp.where(sq_r[:, 0:1] < n_experts, n_blk, 0.0), axis=0, keepdims=True)
        owner = jnp.sum(jnp.where((sq_r < n_experts) & (blk_end <= sq_c.astype(F32)), 1.0, 0.0),
                        axis=0, keepdims=True)
        last_owner = jnp.max(jnp.where(n_blk > 0.0, sq_r[:, 0:1].astype(F32), 0.0), axis=0, keepdims=True)
        blk = sq_c[0:1, :].astype(F32)
        in_use = blk < n_used
        owner = jnp.where(in_use, owner, last_owner)
        mine = sq_r.astype(F32) == owner
        cnt_o = jnp.sum(jnp.where(mine, cnt_ref[...], 0.0), axis=0, keepdims=True)
        start_o = jnp.sum(jnp.where(mine, blk_start, 0.0), axis=0, keepdims=True)
        valid = jnp.clip(cnt_o - (blk - start_o) * rows, 0.0, float(rows))
        valid = jnp.where(in_use, valid, 0.0)
        row8 = lax.broadcasted_iota(jnp.int32, (8, LANES), 0)
        meta = jnp.where(row8 == 0, owner, jnp.where(row8 == 1, valid, jnp.where(row8 == 2, n_used, 0.0)))
        meta_ref[...] = meta.astype(jnp.int32)

    @pl.when(phase == 1)
    def _():
        tr = lax.broadcasted_iota(jnp.int32, (tb, tb), 0)
        tc = lax.broadcasted_iota(jnp.int32, (tb, tb), 1)
        earlier = jnp.where(tr < tc, 1.0, 0.0).astype(BF16)
        seen = _dot(hits.astype(BF16), earlier)
        slot = start_ref[...] * rows + base_ref[...] + seen
        p1 = jnp.sum(hit1 * slot, axis=0, keepdims=True)
        p2 = jnp.sum(hit2 * slot, axis=0, keepdims=True)
        row8 = lax.broadcasted_iota(jnp.int32, (8, tb), 0)
        pos_ref[...] = jnp.where(row8 == 0, p1, jnp.where(row8 == 1, p2, 0.0)).astype(jnp.int32)
        base_ref[...] += per_expert


def _plan(ids_t, n_experts, rows, tb):
    m = ids_t.shape[1]
    nb = m // tb
    kern = functools.partial(_plan_kernel, tb=tb, rows=rows, n_experts=n_experts, nb=nb)
    col = pltpu.VMEM((LANES, 1), F32)
    return pl.pallas_call(
        kern,
        grid=(2, nb),
        in_specs=[pl.BlockSpec((8, tb), lambda p, j: (0, j))],
        out_specs=[pl.BlockSpec((8, tb), lambda p, j: (0, j * p)),
                   pl.BlockSpec((8, LANES), lambda p, j: (0, 0))],
        out_shape=[jax.ShapeDtypeStruct((8, m), jnp.int32),
                   jax.ShapeDtypeStruct((8, LANES), jnp.int32)],
        scratch_shapes=[col, col, col],
        compiler_params=_cparams(("arbitrary", "arbitrary")),
        name="moe_plan",
    )(ids_t)


def _for_range(lo, hi, body):
    full = (hi - lo) // ROW_GROUP
    first = lo // ROW_GROUP

    def group(g, _):
        for u in range(ROW_GROUP):
            body(first + g, u)
        return 0

    lax.fori_loop(0, full, group, 0)

    def one(r, _):
        body(r // ROW_GROUP, r % ROW_GROUP)
        return 0

    lax.fori_loop(lo + full * ROW_GROUP, hi, one, 0)


def _row_in(src_hbm, dst_vmem, sem, src_row, tile, sub):
    return pltpu.make_async_copy(src_hbm.at[pl.ds(src_row, 1)], dst_vmem.at[tile, pl.ds(sub, 1)], sem)


def _row_out(src_vmem, dst_hbm, sem, tile, sub, dst_row):
    return pltpu.make_async_copy(src_vmem.at[tile, pl.ds(sub, 1)], dst_hbm.at[pl.ds(dst_row, 1)], sem)


def _dma_thread(sub_row):
    return sub_row % 2 if isinstance(sub_row, int) else 0


def _gather_cparams(sem):
    return pltpu.CompilerParams(dimension_semantics=sem, vmem_limit_bytes=VMEM_LIMIT,
                                disable_bounds_checks=True)


def _ffn_kernel(asg_ref, first_ref, be_ref, nv_ref, nu_ref, h_hbm, wg_ref, wu_ref, wd_ref, y_hbm,
                xf_ref, xb_ref, acc_ref, sem_in, sem_out, *, rows, sub, nj, n_tok, n_blocks):
    i = pl.program_id(0)
    j = pl.program_id(1)
    n_used = nu_ref[0]
    used = i < n_used
    slot = i % 2

    def gather(block, lo, hi, buf):
        base = first_ref[block]

        def body(tile, sub_row):
            a = asg_ref[base + tile * ROW_GROUP + sub_row]
            tok = jnp.where(a >= n_tok, a - n_tok, a)
            _row_in(h_hbm, xf_ref.at[buf], sem_in.at[buf], tok, tile, sub_row).start(priority=_dma_thread(sub_row))

        _for_range(lo, hi, body)

    def gather_wait(block, buf):
        _for_range(0, nv_ref[block],
                   lambda tile, sub_row: _row_in(h_hbm, xf_ref.at[buf], sem_in.at[buf], 0, tile, sub_row).wait())

    def emit(block, buf):
        base = first_ref[block]

        def body(tile, sub_row):
            a = asg_ref[base + tile * ROW_GROUP + sub_row]
            _row_out(acc_ref.at[buf], y_hbm, sem_out.at[buf], tile, sub_row, a).start(priority=_dma_thread(sub_row))

        _for_range(0, nv_ref[block], body)

    def emit_wait(block, buf):
        _for_range(0, nv_ref[block],
                   lambda tile, sub_row: _row_out(acc_ref.at[buf], y_hbm, sem_out.at[buf], tile, sub_row, 0).wait())

    def as_rows(x):
        return x.reshape(x.shape[0] * ROW_GROUP, x.shape[2])

    def as_tiles(x):
        return x.reshape(x.shape[0] // ROW_GROUP, ROW_GROUP, x.shape[1])

    @pl.when((i == 0) & (j == 0))
    def _():
        xf_ref[...] = jnp.zeros_like(xf_ref)
        gather(0, 0, nv_ref[0], 0)

    @pl.when(used & (j == 0))
    def _():
        gather_wait(i, slot)

    @pl.when(i + 1 < n_used)
    def _():
        nxt = nv_ref[i + 1]
        q = rows // nj
        gather(i + 1, jnp.minimum(j * q, nxt), jnp.minimum((j + 1) * q, nxt), 1 - slot)

    n_sub = (nv_ref[i] + sub - 1) // sub
    for n in range(1, rows // sub + 1):
        @pl.when(used & (n_sub == n))
        def _(n=n):
            r = n * sub
            nt = r // ROW_GROUP

            @pl.when(j == 0)
            def _():
                xb_ref[0:r, :] = as_rows(xf_ref[slot, 0:nt]).astype(BF16)
                acc_ref[slot, 0:nt] = jnp.zeros((nt,) + acc_ref.shape[2:], F32)

            xb = xb_ref[0:r, :]
            gate = _dot(xb, wg_ref[...].astype(BF16))
            up = _dot(xb, wu_ref[...].astype(BF16))
            hid = (gate * jax.nn.sigmoid(gate)) * up
            acc_ref[slot, 0:nt] += as_tiles(_dot(hid.astype(BF16), wd_ref[...].astype(BF16)))

    @pl.when((j == nj - 1) & (i >= 1) & (i <= n_used))
    def _():
        emit_wait(i - 1, 1 - slot)

    @pl.when((j == nj - 1) & used)
    def _():
        emit(i, slot)

    @pl.when((j == nj - 1) & used & (i == n_blocks - 1))
    def _():
        emit_wait(i, slot)


def _moe_ffn(asg_order, block_first, block_expert, n_valid, n_used, h1, w_gate, w_up, w_down, n_blocks, rows,
             tf):
    n_tok, d = h1.shape
    de = w_gate.shape[2]
    nj = de // tf
    kern = functools.partial(_ffn_kernel, rows=rows, sub=MOE_SUB, nj=nj, n_tok=n_tok, n_blocks=n_blocks)

    def jidx(i, j, nu):
        return jnp.where(i < nu[0], j, nj - 1)

    return pl.pallas_call(
        kern,
        grid_spec=pltpu.PrefetchScalarGridSpec(
            num_scalar_prefetch=5,
            grid=(n_blocks, nj),
            in_specs=[
                pl.BlockSpec(memory_space=pl.ANY),
                pl.BlockSpec((None, d, tf), lambda i, j, asg, fi, be, nv, nu: (be[i], 0, jidx(i, j, nu))),
                pl.BlockSpec((None, d, tf), lambda i, j, asg, fi, be, nv, nu: (be[i], 0, jidx(i, j, nu))),
                pl.BlockSpec((None, tf, d), lambda i, j, asg, fi, be, nv, nu: (be[i], jidx(i, j, nu), 0)),
            ],
            out_specs=pl.BlockSpec(memory_space=pl.ANY),
            scratch_shapes=[pltpu.VMEM((2, rows // ROW_GROUP, ROW_GROUP, d), F32),
                            pltpu.VMEM((rows, d), BF16),
                            pltpu.VMEM((2, rows // ROW_GROUP, ROW_GROUP, d), F32),
                            pltpu.SemaphoreType.DMA((2,)),
                            pltpu.SemaphoreType.DMA((2,))],
        ),
        out_shape=jax.ShapeDtypeStruct((2 * n_tok, d), F32),
        compiler_params=_gather_cparams(("arbitrary", "arbitrary")),
        name="moe_ffn",
    )(asg_order, block_first, block_expert, n_valid, n_used, h1, w_gate, w_up, w_down)


def _combine_kernel(h_ref, y0_ref, y1_ref, wt_ref, g_ref, b_ref, o_ref):
    wt = wt_ref[...]
    ffn = y0_ref[...] * wt[:, 0:1] + y1_ref[...] * wt[:, 1:2]
    o_ref[...] = _layer_norm(DEEPNORM_ALPHA * h_ref[...] + ffn, g_ref[...], b_ref[...])


def _combine(h1, ys, wts, g, b, tm):
    m, d = h1.shape
    nt = m // tm
    rows = pl.BlockSpec((tm, d), lambda i: (i, 0))
    vec = pl.BlockSpec((1, d), lambda i: (0, 0))
    return pl.pallas_call(
        _combine_kernel,
        grid=(nt,),
        in_specs=[rows, rows, pl.BlockSpec((tm, d), lambda i: (nt + i, 0)),
                  pl.BlockSpec((tm, LANES), lambda i: (i, 0)), vec, vec],
        out_specs=rows,
        out_shape=jax.ShapeDtypeStruct((m, d), F32),
        compiler_params=_cparams(("parallel",)),
        name="moe_combine",
    )(h1, ys, ys, wts, g, b)


def _dispatch_plan(ids_t, n_experts, rows):
    m = ids_t.shape[1]
    n_blocks = -(-(2 * m) // rows) + n_experts
    assert n_blocks <= LANES
    pos_t, meta = _plan(ids_t, n_experts, rows, _pick(m, 512))
    pos = pos_t[:2].reshape(-1)
    asg_order = jnp.argsort(pos).astype(jnp.int32)
    n_valid = meta[1, :n_blocks]
    block_first = (jnp.cumsum(n_valid) - n_valid).astype(jnp.int32)
    return asg_order, block_first, meta[2, :1], meta[0, :n_blocks], n_valid, n_blocks


def _pick(n, pref):
    t = min(pref, n)
    while n % t:
        t //= 2
    return t


def kernel(x, mem, ln_in_g, ln_in_b, w_in, tshift_mu, w0, w_decay_up, a0, w_a_up, w_g_up, k_k, k_a, r_k,
           lnx_g, lnx_b, w_mem_kv, w_out, ln1_g, ln1_b, router_group, router_group_b, router_expert,
           router_expert_b, w_e_gate, w_e_up, w_e_down, ln2_g, ln2_b):
    batch, seq, d = x.shape
    mem_len = mem.shape[1]
    m = batch * seq
    c = w0.shape[1]
    dr, ar, gr = w_decay_up.shape[1], w_a_up.shape[1], w_g_up.shape[1]
    rwkv_cols = 3 * c + dr + ar + gr
    sb_w = SB_HEADS * LANES
    mem_w = MEM_HEADS * LANES
    assert dr + ar == LANES and c % (2 * LANES) == 0 and w_in.shape[0] == DEPTH
    assert w_in.shape[2] == rwkv_cols + 3 * sb_w + mem_w
    n_experts = router_expert.shape[2]
    row = lambda a: a.reshape(1, -1)

    tn = 512
    low_w = -(-(dr + ar + gr) // LANES) * LANES
    rw_pad = -(-(3 * c + low_w) // tn) * tn
    wi = w_in[0]
    w_rwkv = jnp.pad(wi[:, :rwkv_cols].astype(BF16), ((0, 0), (0, rw_pad - rwkv_cols)))
    w_attn = wi[:, rwkv_cols:].astype(BF16)
    mu = row(jnp.pad(tshift_mu[0], (0, rw_pad - rwkv_cols)))
    lp = low_w - (dr + ar + gr)
    prm = dict(
        w0=row(w0[0]), a0=row(a0[0]), k_k=row(k_k[0]), k_a=row(k_a[0]), r_k=row(r_k[0]),
        lnx_g=row(lnx_g[0]), lnx_b=row(lnx_b[0]),
        wd=jnp.concatenate([w_decay_up[0], jnp.zeros((ar, c), F32)], axis=0).astype(BF16),
        wa=jnp.concatenate([jnp.zeros((dr, c), F32), w_a_up[0]], axis=0).astype(BF16),
        wg=jnp.concatenate([w_g_up[0], jnp.zeros((lp, c), F32)], axis=0).astype(BF16),
    )
    assert (3 * c) % low_w == 0

    x2 = x.reshape(m, d)
    hb = _ln_in(x2, row(ln_in_g), row(ln_in_b), _pick(m, 1024))
    p_rwkv, p_attn = _inproj(hb, w_rwkv, w_attn, mu, seq, _pick(seq, 2048), tn)

    y_rwkv = _wkv(p_rwkv, prm, batch, seq, n_pairs=4, tt=_pick(seq, 1024))
    y_sb = _sb_attn(p_attn, batch, seq, tq=LANES)
    kv = _mem_kv(mem.reshape(batch * mem_len, d), w_mem_kv[0].astype(BF16), _pick(batch * mem_len, 256))
    y_mem = _mem_attn(p_attn, kv, batch, seq, mem_len, tq=_pick(seq, 512))

    wo = w_out[0].astype(BF16)
    r_w = jnp.concatenate([router_group[0], router_expert[0],
                           jnp.zeros((d, LANES - N_GROUPS - n_experts), F32)], axis=1)
    r_b = jnp.concatenate([router_group_b[0], router_expert_b[0],
                           jnp.zeros((LANES - N_GROUPS - n_experts,), F32)]).reshape(1, LANES)
    r_hi = r_w.astype(BF16)
    r_w2 = jnp.concatenate([r_hi, (r_w - r_hi.astype(F32)).astype(BF16)], axis=1)
    h1, logits = _outproj(y_rwkv, y_sb, y_mem, x2, row(ln_in_g), row(ln_in_b), wo,
                          row(ln1_g[0]), row(ln1_b[0]), r_w2, r_b, _pick(m, 512))

    ids_t, wts = _route(logits, N_GROUPS, n_experts // N_GROUPS, _pick(m, 512))
    asg_order, block_first, n_used, block_expert, n_valid, n_blocks = _dispatch_plan(ids_t, n_experts, MOE_ROWS)
    ys = _moe_ffn(asg_order, block_first, block_expert, n_valid, n_used, h1, w_e_gate[0], w_e_up[0], w_e_down[0],
                  n_blocks, MOE_ROWS, tf=512)
    out = _combine(h1, ys, wts, row(ln2_g[0]), row(ln2_b[0]), _pick(m, 512))
    return out.reshape(batch, seq, d)
```
